```python
import math
import jax, jax.numpy as jnp
from jax import lax
import numpy as np

D_MODEL = 1024
BATCH = 8
SEQ = 2048
DEPTH = 1

HEAD_DIM = 64
SWA_HEADS = 8
SWA_KV_HEADS = 2
SWA_WINDOW = 128
MOBA_HEADS = 8
MOBA_KV_HEADS = 2
MOBA_BLOCK = 256
MOBA_TOPK = 3
Q_CHUNK = 128
D_FF = int(math.ceil(8 * D_MODEL / 3 / 128)) * 128
ALPHA = (2.0 * DEPTH) ** 0.25
BETA = (8.0 * DEPTH) ** -0.25
LN_EPS = 1e-5
NEG = -1e30

SWA_Q = SWA_HEADS * HEAD_DIM
SWA_KV = SWA_KV_HEADS * HEAD_DIM
MOBA_Q = MOBA_HEADS * HEAD_DIM
MOBA_KV = MOBA_KV_HEADS * HEAD_DIM
IN_SPLITS = (SWA_Q, SWA_KV, SWA_KV, MOBA_Q, MOBA_KV, MOBA_KV, D_MODEL, D_MODEL)
IN_COLS = sum(IN_SPLITS)
N_ALIBI = SWA_HEADS + MOBA_HEADS

kernel_name = "hybrid_swa_sink_moba_macaron_deepnorm"


def alibi_slopes():
    i = jnp.arange(1, N_ALIBI + 1, dtype=jnp.float32)
    return jnp.exp2(-8.0 * i / N_ALIBI)


def layer_norm(x, g, b):
    xf = x.astype(jnp.float32)
    mu = jnp.mean(xf, axis=-1, keepdims=True)
    var = jnp.mean(jnp.square(xf - mu), axis=-1, keepdims=True)
    y = (xf - mu) * lax.rsqrt(var + LN_EPS) * g.astype(jnp.float32) + b.astype(jnp.float32)
    return y.astype(x.dtype)


def swiglu_ffn(x, w_in, w_out):
    a, u = jnp.split(x @ w_in, 2, axis=-1)
    return (jax.nn.silu(a) * u) @ w_out


def swa_sink_attention(q, k, v, sinks, slopes):
    B, S, H, d = q.shape
    Hkv = k.shape[2]
    G = H // Hkv
    W = SWA_WINDOW
    nb = S // W
    qb = q.reshape(B, nb, W, Hkv, G, d)
    kb = k.reshape(B, nb, W, Hkv, d)
    vb = v.reshape(B, nb, W, Hkv, d)
    pad = ((0, 0), (1, 0), (0, 0), (0, 0), (0, 0))
    kwin = jnp.concatenate([jnp.pad(kb, pad)[:, :nb], kb], axis=2)
    vwin = jnp.concatenate([jnp.pad(vb, pad)[:, :nb], vb], axis=2)
    s = jnp.einsum('bnqkgd,bnskd->bnkgqs', qb, kwin).astype(jnp.float32) * (1.0 / math.sqrt(d))
    qi = jnp.arange(W)[:, None]
    kj = jnp.arange(2 * W)[None, :]
    dist = qi + W - kj
    kpos = jnp.arange(nb)[:, None, None] * W - W + kj[None]
    allowed = (dist >= 0) & (dist < W) & (kpos >= 0)
    sl = slopes.reshape(Hkv, G)
    s = s - sl[:, :, None, None] * dist.astype(jnp.float32)
    s = jnp.where(allowed[None, :, None, None], s, NEG)
    sink = sinks.astype(jnp.float32).reshape(Hkv, G)[None, None, :, :, None, None]
    m = jnp.maximum(jnp.max(s, axis=-1, keepdims=True), sink)
    p = jnp.exp(s - m)
    p = p / (jnp.sum(p, axis=-1, keepdims=True) + jnp.exp(sink - m))
    o = jnp.einsum('bnkgqs,bnskd->bnqkgd', p.astype(v.dtype), vwin)
    return o.reshape(B, S, H, d)


def moba_attention(q, k, v, slopes):
    B, S, H, d = q.shape
    Hkv = k.shape[2]
    G = H // Hkv
    BL = MOBA_BLOCK
    C = Q_CHUNK
    Sp = -(-S // BL) * BL
    if Sp != S:
        pw = ((0, 0), (0, Sp - S), (0, 0), (0, 0))
        q, k, v = jnp.pad(q, pw), jnp.pad(k, pw), jnp.pad(v, pw)
    nb = Sp // BL
    nc = Sp // C
    scale = 1.0 / math.sqrt(d)
    kb = k.reshape(B, nb, BL, Hkv, d)
    vb = v.reshape(B, nb, BL, Hkv, d)
    n_sel = min(MOBA_TOPK, nb - 1)
    q_blk = jnp.arange(Sp) // BL
    if n_sel > 0:
        kmean = jnp.mean(kb.astype(jnp.float32), axis=2)
        gate = jnp.einsum('bskgd,bnkd->bskgn', q.reshape(B, Sp, Hkv, G, d).astype(jnp.float32),
                          kmean).reshape(B, Sp, H, nb)
        past = jnp.arange(nb)[None, :] < q_blk[:, None]
        gate = jnp.where(past[None, :, None, :], gate, -jnp.inf)
        _, idx = lax.top_k(gate, n_sel)
        valid = idx < q_blk[None, :, None, None]
    else:
        idx = jnp.zeros((B, Sp, H, 0), jnp.int32)
        valid = jnp.zeros((B, Sp, H, 0), bool)
    kh = jnp.repeat(kb, G, axis=3).transpose(0, 1, 3, 2, 4)
    vh = jnp.repeat(vb, G, axis=3).transpose(0, 1, 3, 2, 4)
    bidx = jnp.arange(B)[:, None, None, None]
    hidx = jnp.arange(H)[None, None, :, None]

    def to_chunks(a):
        return a.reshape((B, nc, C) + a.shape[2:]).transpose((1, 0, 2) + tuple(range(3, a.ndim + 1)))

    def one_chunk(xs):
        c, qc, idxc, validc = xs
        pos_q = c * C + jnp.arange(C)
        own = (c * C) // BL
        k_own = lax.dynamic_index_in_dim(kb, own, axis=1, keepdims=False)
        v_own = lax.dynamic_index_in_dim(vb, own, axis=1, keepdims=False)
        pos_own = own * BL + jnp.arange(BL)
        s_own = jnp.einsum('bqkgd,bskd->bqkgs', qc.reshape(B, C, Hkv, G, d), k_own)
        s_own = s_own.astype(jnp.float32).reshape(B, C, H, BL) * scale
        dist_own = (pos_q[:, None] - pos_own[None, :]).astype(jnp.float32)
        s_own = s_own - slopes[None, None, :, None] * dist_own[None, :, None, :]
        s_own = jnp.where((dist_own >= 0)[None, :, None, :], s_own, NEG)
        k_sel = kh[bidx, idxc, hidx]
        v_sel = vh[bidx, idxc, hidx]
        s_sel = jnp.einsum('bqhd,bqhnsd->bqhns', qc, k_sel).astype(jnp.float32) * scale
        pos_sel = idxc[..., None] * BL + jnp.arange(BL)
        dist_sel = (pos_q[None, :, None, None, None] - pos_sel).astype(jnp.float32)
        s_sel = s_sel - slopes[None, None, :, None, None] * dist_sel
        s_sel = jnp.where(validc[..., None], s_sel, NEG).reshape(B, C, H, n_sel * BL)
        p = jax.nn.softmax(jnp.concatenate([s_sel, s_own], axis=-1), axis=-1)
        p_sel = p[..., :n_sel * BL].reshape(B, C, H, n_sel, BL).astype(qc.dtype)
        p_own = p[..., n_sel * BL:].reshape(B, C, Hkv, G, BL).astype(qc.dtype)
        o = jnp.einsum('bqhns,bqhnsd->bqhd', p_sel, v_sel)
        o = o + jnp.einsum('bqkgs,bskd->bqkgd', p_own, v_own).reshape(B, C, H, d)
        return o

    xs = (jnp.arange(nc, dtype=jnp.int32), to_chunks(q), to_chunks(idx), to_chunks(valid))
    out = lax.map(one_chunk, xs)
    return out.transpose(1, 0, 2, 3, 4).reshape(B, Sp, H, d)[:, :S]


def hybrid_mixer(x, w_in, sinks, w_branch_a, w_branch_b, w_o):
    B, S, _ = x.shape
    h = x @ w_in
    offs = list(np.cumsum(IN_SPLITS)[:-1])
    qa, ka, va, qb, kb, vb, ga, gb = jnp.split(h, offs, axis=-1)
    slopes = alibi_slopes()
    ya = swa_sink_attention(qa.reshape(B, S, SWA_HEADS, HEAD_DIM),
                            ka.reshape(B, S, SWA_KV_HEADS, HEAD_DIM),
                            va.reshape(B, S, SWA_KV_HEADS, HEAD_DIM),
                            sinks, slopes[:SWA_HEADS])
    yb = moba_attention(qb.reshape(B, S, MOBA_HEADS, HEAD_DIM),
                        kb.reshape(B, S, MOBA_KV_HEADS, HEAD_DIM),
                        vb.reshape(B, S, MOBA_KV_HEADS, HEAD_DIM),
                        slopes[SWA_HEADS:])
    ya = ya.reshape(B, S, SWA_Q) @ w_branch_a
    yb = yb.reshape(B, S, MOBA_Q) @ w_branch_b
    y = jax.nn.sigmoid(ga) * ya + jax.nn.sigmoid(gb) * yb
    return y @ w_o


def setup_inputs(seed: int = 0) -> dict:
    key = jax.random.key(seed)
    ks = jax.random.split(key, 18)
    L, D = DEPTH, D_MODEL

    def nrm(k, shape, fan_in, gain=1.0):
        return jax.random.normal(k, shape, jnp.float32) * (gain * fan_in ** -0.5)

    def gain(k):
        return 1.0 + 0.02 * jax.random.normal(k, (L, D), jnp.float32)

    def bias(k):
        return 0.02 * jax.random.normal(k, (L, D), jnp.float32)

    col_scale = jnp.concatenate([
        jnp.ones((SWA_Q + SWA_KV,), jnp.float32), jnp.full((SWA_KV,), BETA, jnp.float32),
        jnp.ones((MOBA_Q + MOBA_KV,), jnp.float32), jnp.full((MOBA_KV,), BETA, jnp.float32),
        jnp.ones((2 * D,), jnp.float32)])
    return {
        "x": jax.random.normal(ks[0], (BATCH, SEQ, D), jnp.float32),
        "ffn1_w_in": nrm(ks[1], (L, D, 2 * D_FF), D),
        "ffn1_w_out": nrm(ks[2], (L, D_FF, D), D_FF, BETA),
        "ln1_g": gain(ks[3]),
        "ln1_b": bias(ks[4]),
        "mix_w_in": nrm(ks[5], (L, D, IN_COLS), D) * col_scale,
        "swa_sinks": 0.5 * jax.random.normal(ks[6], (L, SWA_HEADS), jnp.float32),
        "w_branch_a": nrm(ks[7], (L, SWA_Q, D), SWA_Q),
        "w_branch_b": nrm(ks[8], (L, MOBA_Q, D), MOBA_Q),
        "mix_w_o": nrm(ks[9], (L, D, D), D, BETA),
        "ln2_g": gain(ks[10]),
        "ln2_b": bias(ks[11]),
        "ffn2_w_in": nrm(ks[12], (L, D, 2 * D_FF), D),
        "ffn2_w_out": nrm(ks[13], (L, D_FF, D), D_FF, BETA),
        "ln3_g": gain(ks[14]),
        "ln3_b": bias(ks[15]),
    }


def reference(x, ffn1_w_in, ffn1_w_out, ln1_g, ln1_b, mix_w_in, swa_sinks, w_branch_a, w_branch_b,
              mix_w_o, ln2_g, ln2_b, ffn2_w_in, ffn2_w_out, ln3_g, ln3_b):
    for l in range(DEPTH):
        x = layer_norm(ALPHA * x + 0.5 * swiglu_ffn(x, ffn1_w_in[l], ffn1_w_out[l]), ln1_g[l], ln1_b[l])
        x = layer_norm(ALPHA * x + hybrid_mixer(x, mix_w_in[l], swa_sinks[l], w_branch_a[l],
                                                w_branch_b[l], mix_w_o[l]), ln2_g[l], ln2_b[l])
        x = layer_norm(ALPHA * x + 0.5 * swiglu_ffn(x, ffn2_w_in[l], ffn2_w_out[l]), ln3_g[l], ln3_b[l])
    return x
```

```python
import functools

import numpy as np
import jax
import jax.numpy as jnp
from jax import lax
from jax.experimental import pallas as pl
from jax.experimental.pallas import tpu as pltpu

F32 = jnp.float32
BF16 = jnp.bfloat16

D_MODEL = 1024
BATCH = 8
SEQ = 2048
TOKENS = BATCH * SEQ
HEAD_DIM = 64
N_HEADS = 8
KV_COLS = 128
Q_COLS = 512
SWA_WINDOW = 128
MOBA_BLOCK = 256
MOBA_TOPK = 3
N_MOBA_BLOCKS = SEQ // MOBA_BLOCK
D_FF = 2816
FFN_CHUNK = 256
N_FFN_CHUNKS = D_FF // FFN_CHUNK
ALPHA = 2.0 ** 0.25
LN_EPS = 1e-5
NEG = -1e30
Q_SCALE = 0.125
ALIBI_SLOPES = tuple(float(2.0 ** (-8.0 * i / 16.0)) for i in range(1, 17))

HEAD_OF_SLOT = (0, 4, 1, 5, 2, 6, 3, 7)

V7X_VMEM_LIMIT_BYTES = 56 * 1024 * 1024
TOKEN_TILE = 512


def _const_spec(shape):
    zeros = (0,) * len(shape)
    return pl.BlockSpec(shape, lambda *_: zeros, pipeline_mode=pl.Buffered(1))


def _layer_norm(y, g, b):
    mu = jnp.mean(y, axis=-1, keepdims=True)
    yc = y - mu
    var = jnp.mean(yc * yc, axis=-1, keepdims=True)
    return yc * lax.rsqrt(var + LN_EPS) * g + b


def _swiglu(xb, w_in_ref, w_out_ref):
    tm = xb.shape[0]

    def body(c, acc):
        h = jnp.dot(xb, w_in_ref[c], preferred_element_type=F32)
        a = h[:, :FFN_CHUNK]
        u = h[:, FFN_CHUNK:]
        g = (a * jax.nn.sigmoid(a) * u).astype(BF16)
        return acc + jnp.dot(g, w_out_ref[c], preferred_element_type=F32)

    return lax.fori_loop(0, N_FFN_CHUNKS, body, jnp.zeros((tm, D_MODEL), F32))


def _ffn_qkv_kernel(x_ref, w_in_ref, w_out_ref, g_ref, b_ref, wqkv_ref,
                    x1_ref, qa_ref, ka_ref, va_ref, qb_ref, kb_ref, vb_ref):
    x = x_ref[...]
    f = _swiglu(x.astype(BF16), w_in_ref, w_out_ref)
    x1 = _layer_norm(ALPHA * x + 0.5 * f, g_ref[...], b_ref[...])
    x1_ref[...] = x1
    h = jnp.dot(x1.astype(BF16), wqkv_ref[...], preferred_element_type=F32)
    qa_ref[...] = (h[:, 0:512] * Q_SCALE).astype(BF16)
    ka_ref[...] = h[:, 512:640].astype(BF16)
    va_ref[...] = h[:, 640:768].astype(BF16)
    qb_ref[...] = (h[:, 768:1280] * Q_SCALE).astype(BF16)
    kb_ref[...] = h[:, 1280:1408].astype(BF16)
    vb_ref[...] = h[:, 1408:1536].astype(BF16)


def _ffn_qkv(x, w_in, w_out, g, b, wqkv):
    tm = TOKEN_TILE
    row = lambda n: pl.BlockSpec((tm, n), lambda i: (i, 0))
    out_cols = (D_MODEL, Q_COLS, KV_COLS, KV_COLS, Q_COLS, KV_COLS, KV_COLS)
    out_dtypes = (F32,) + (BF16,) * 6
    return pl.pallas_call(
        _ffn_qkv_kernel,
        grid=(TOKENS // tm,),
        in_specs=[row(D_MODEL), _const_spec(w_in.shape), _const_spec(w_out.shape),
                  _const_spec(g.shape), _const_spec(b.shape), _const_spec(wqkv.shape)],
        out_specs=[row(n) for n in out_cols],
        out_shape=[jax.ShapeDtypeStruct((TOKENS, n), dt) for n, dt in zip(out_cols, out_dtypes)],
        compiler_params=pltpu.CompilerParams(
            dimension_semantics=("arbitrary",), vmem_limit_bytes=V7X_VMEM_LIMIT_BYTES),
        name="ffn_qkv",
    )(x, w_in, w_out, g, b, wqkv)


def _swa_kernel(sink_ref, q_ref, kp_ref, kc_ref, vp_ref, vc_ref, o_ref):
    n = pl.program_id(1)
    w = SWA_WINDOW
    k = jnp.concatenate([kp_ref[...], kc_ref[...]], axis=0)
    v = jnp.concatenate([vp_ref[...], vc_ref[...]], axis=0)
    lane = lax.broadcasted_iota(jnp.int32, (w, 128), 1)
    row = lax.broadcasted_iota(jnp.int32, (w, 2 * w), 0)
    col = lax.broadcasted_iota(jnp.int32, (w, 2 * w), 1)
    dist = row + w - col
    allowed = (dist >= 0) & (dist < w) & ((col >= w) | (n > 0))
    distf = dist.astype(F32)
    tiles = []
    for pair in range(4):
        qp = q_ref[:, pair * 128:(pair + 1) * 128]
        halves = []
        for half in range(2):
            head = HEAD_OF_SLOT[2 * pair + half]
            keep = (lane < 64) if half == 0 else (lane >= 64)
            qm = jnp.where(keep, qp, jnp.zeros_like(qp))
            s = lax.dot_general(qm, k, (((1,), (1,)), ((), ())), preferred_element_type=F32)
            s = s - ALIBI_SLOPES[head] * distf
            s = jnp.where(allowed, s, NEG)
            sink = sink_ref[head]
            m = jnp.maximum(jnp.max(s, axis=-1, keepdims=True), sink)
            p = jnp.exp(s - m)
            denom = jnp.sum(p, axis=-1, keepdims=True) + jnp.exp(sink - m)
            o = jnp.dot(p.astype(BF16), v, preferred_element_type=F32)
            halves.append(o / denom)
        tiles.append(jnp.where(lane < 64, halves[0], halves[1]))
    o_ref[...] = jnp.concatenate(tiles, axis=1).astype(BF16)


def _swa(sinks, qa, ka, va):
    w = SWA_WINDOW
    nb = SEQ // w
    cur = lambda b, n: (b, n, 0)
    prev = lambda b, n: (b, jnp.maximum(n - 1, 0), 0)
    return pl.pallas_call(
        _swa_kernel,
        grid=(BATCH, nb),
        in_specs=[pl.BlockSpec(memory_space=pltpu.SMEM),
                  pl.BlockSpec((None, w, Q_COLS), cur),
                  pl.BlockSpec((None, w, KV_COLS), prev),
                  pl.BlockSpec((None, w, KV_COLS), cur),
                  pl.BlockSpec((None, w, KV_COLS), prev),
                  pl.BlockSpec((None, w, KV_COLS), cur)],
        out_specs=pl.BlockSpec((None, w, Q_COLS), cur),
        out_shape=jax.ShapeDtypeStruct((BATCH, SEQ, Q_COLS), BF16),
        compiler_params=pltpu.CompilerParams(dimension_semantics=("arbitrary", "arbitrary")),
        name="swa",
    )(sinks, qa, ka, ka, va, va)


def _moba_kernel(q_ref, k_ref, v_ref, o_ref, km_ref, qaug_ref, m_ref, l_ref, acc_ref):
    t = pl.program_id(1)
    bl = MOBA_BLOCK
    nbk = N_MOBA_BLOCKS
    lane = lax.broadcasted_iota(jnp.int32, (bl, 128), 1)
    lane_blk = lane % nbk
    lane_slot = lane // nbk

    @pl.when(t == 0)
    def _():
        rsel = lax.broadcasted_iota(jnp.int32, (128, SEQ), 0)
        tsel = lax.broadcasted_iota(jnp.int32, (128, SEQ), 1)
        avg = jnp.where((rsel % nbk) == (tsel // bl), 1.0 / bl, 0.0).astype(BF16)
        kmean_rows = jnp.dot(avg, k_ref[...], preferred_element_type=F32)
        kmean_t = kmean_rows.T
        r128 = lax.broadcasted_iota(jnp.int32, (128, 128), 0)
        c128 = lax.broadcasted_iota(jnp.int32, (128, 128), 1)
        for pair in range(4):
            slot_of_row = 2 * pair + (r128 >= 64).astype(jnp.int32)
            blk = jnp.where((c128 // nbk) == slot_of_row, kmean_t, 0.0)
            hi = blk.astype(BF16)
            r1 = blk - hi.astype(F32)
            mid = r1.astype(BF16)
            lo = (r1 - mid.astype(F32)).astype(BF16)
            km_ref[0, pair * 128:(pair + 1) * 128, :] = hi
            km_ref[1, pair * 128:(pair + 1) * 128, :] = mid
            km_ref[2, pair * 128:(pair + 1) * 128, :] = lo

    q = q_ref[...]
    gate = (jnp.dot(q, km_ref[0], preferred_element_type=F32)
            + jnp.dot(q, km_ref[1], preferred_element_type=F32)
            + jnp.dot(q, km_ref[2], preferred_element_type=F32))
    past = lane_blk < t
    gm = jnp.where(past, gate, -jnp.inf)
    rank = jnp.zeros((bl, 128), jnp.int32)
    for d in range(1, nbk):
        wrapped = lane_blk + d >= nbk
        partner = jnp.where(wrapped, pltpu.roll(gm, nbk - d, 1), pltpu.roll(gm, 128 - d, 1))
        beats = (partner > gm) | ((partner == gm) & wrapped)
        rank = rank + beats.astype(jnp.int32)
    selected = past & (rank < MOBA_TOPK) & (lane < 64)
    selb = jnp.where(selected | (lane >= 64), 0.0, NEG).astype(BF16)

    for slot in range(N_HEADS):
        pair, half = divmod(slot, 2)
        qp = q[:, pair * 128:(pair + 1) * 128]
        keep = (lane < 64) if half == 0 else (lane >= 64)
        qm = jnp.where(keep, qp, jnp.zeros_like(qp))
        sb = jnp.where(lane_slot == slot, selb, jnp.zeros_like(selb))
        qaug_ref[slot * bl:(slot + 1) * bl, :] = jnp.concatenate([qm, sb], axis=1)

    row = lax.broadcasted_iota(jnp.int32, (bl, bl), 0)
    col = lax.broadcasted_iota(jnp.int32, (bl, bl), 1)
    dlocal = (row - col).astype(F32)

    def attend(j, diag):
        start = pl.multiple_of(j * bl, bl)
        kj = k_ref[pl.ds(start, bl), :]
        vj = v_ref[pl.ds(start, bl), :]
        if diag:
            onehot = jnp.zeros((bl, 128), BF16)
        else:
            onehot = jnp.where((lane_blk == j) & (lane < 64), 1.0, 0.0).astype(BF16)
        kaug = jnp.concatenate([kj, onehot], axis=1)
        s_all = lax.dot_general(qaug_ref[...], kaug, (((1,), (1,)), ((), ())),
                                preferred_element_type=F32)
        offset = ((t - j) * bl).astype(F32)
        for slot in range(N_HEADS):
            slope = ALIBI_SLOPES[N_HEADS + HEAD_OF_SLOT[slot]]
            s = s_all[slot * bl:(slot + 1) * bl, :] - slope * (dlocal + offset)
            if diag:
                s = jnp.where(dlocal >= 0, s, NEG)
                m_new = jnp.max(s, axis=-1, keepdims=True)
                p = jnp.exp(s - m_new)
                l_ref[slot] = jnp.sum(p, axis=-1, keepdims=True)
                acc_ref[slot] = jnp.dot(p.astype(BF16), vj, preferred_element_type=F32)
            else:
                m_old = m_ref[slot]
                m_new = jnp.maximum(m_old, jnp.max(s, axis=-1, keepdims=True))
                p = jnp.exp(s - m_new)
                alpha = jnp.exp(m_old - m_new)
                l_ref[slot] = alpha * l_ref[slot] + jnp.sum(p, axis=-1, keepdims=True)
                acc_ref[slot] = alpha * acc_ref[slot] + jnp.dot(p.astype(BF16), vj,
                                                                preferred_element_type=F32)
            m_ref[slot] = m_new

    attend(t, True)

    def past_body(j, carry):
        attend(j, False)
        return carry

    lax.fori_loop(0, t, past_body, 0)

    lane_o = lax.broadcasted_iota(jnp.int32, (bl, 128), 1)
    tiles = []
    for pair in range(4):
        lo = acc_ref[2 * pair] / l_ref[2 * pair]
        hi = acc_ref[2 * pair + 1] / l_ref[2 * pair + 1]
        tiles.append(jnp.where(lane_o < 64, lo, hi))
    o_ref[...] = jnp.concatenate(tiles, axis=1).astype(BF16)


def _moba(qb, kb, vb):
    bl = MOBA_BLOCK
    tile = lambda b, t: (b, t, 0)
    whole = lambda b, t: (b, 0, 0)
    return pl.pallas_call(
        _moba_kernel,
        grid=(BATCH, N_MOBA_BLOCKS),
        in_specs=[pl.BlockSpec((None, bl, Q_COLS), tile),
                  pl.BlockSpec((None, SEQ, KV_COLS), whole),
                  pl.BlockSpec((None, SEQ, KV_COLS), whole)],
        out_specs=pl.BlockSpec((None, bl, Q_COLS), tile),
        out_shape=jax.ShapeDtypeStruct((BATCH, SEQ, Q_COLS), BF16),
        scratch_shapes=[pltpu.VMEM((3, Q_COLS, 128), BF16),
                        pltpu.VMEM((N_HEADS * bl, 2 * 128), BF16),
                        pltpu.VMEM((N_HEADS, bl, 1), F32),
                        pltpu.VMEM((N_HEADS, bl, 1), F32),
                        pltpu.VMEM((N_HEADS, bl, 128), F32)],
        compiler_params=pltpu.CompilerParams(
            dimension_semantics=("arbitrary", "arbitrary"), vmem_limit_bytes=V7X_VMEM_LIMIT_BYTES),
        name="moba",
    )(qb, kb, vb)


def _mix_ffn_kernel(x1_ref, ya_ref, yb_ref, wg_ref, wa_ref, wb_ref, wo_ref, g2_ref, b2_ref,
                    w_in_ref, w_out_ref, g3_ref, b3_ref, o_ref):
    x1 = x1_ref[...]
    gates = jnp.dot(x1.astype(BF16), wg_ref[...], preferred_element_type=F32)
    ya = jnp.dot(ya_ref[...], wa_ref[...], preferred_element_type=F32)
    yb = jnp.dot(yb_ref[...], wb_ref[...], preferred_element_type=F32)
    y = jax.nn.sigmoid(gates[:, :D_MODEL]) * ya + jax.nn.sigmoid(gates[:, D_MODEL:]) * yb
    z = jnp.dot(y.astype(BF16), wo_ref[...], preferred_element_type=F32)
    x2 = _layer_norm(ALPHA * x1 + z, g2_ref[...], b2_ref[...])
    f = _swiglu(x2.astype(BF16), w_in_ref, w_out_ref)
    o_ref[...] = _layer_norm(ALPHA * x2 + 0.5 * f, g3_ref[...], b3_ref[...])


def _mix_ffn(x1, ya, yb, wg, wa, wb, wo, g2, b2, w_in, w_out, g3, b3):
    tm = TOKEN_TILE
    row = lambda n: pl.BlockSpec((tm, n), lambda i: (i, 0))
    consts = (wg, wa, wb, wo, g2, b2, w_in, w_out, g3, b3)
    return pl.pallas_call(
        _mix_ffn_kernel,
        grid=(TOKENS // tm,),
        in_specs=[row(D_MODEL), row(Q_COLS), row(Q_COLS)] + [_const_spec(c.shape) for c in consts],
        out_specs=row(D_MODEL),
        out_shape=jax.ShapeDtypeStruct((TOKENS, D_MODEL), F32),
        compiler_params=pltpu.CompilerParams(
            dimension_semantics=("arbitrary",), vmem_limit_bytes=V7X_VMEM_LIMIT_BYTES),
        name="mix_ffn",
    )(x1, ya, yb, *consts)


def _ffn_weights(w_in, w_out):
    a = w_in[:, :D_FF].reshape(D_MODEL, N_FFN_CHUNKS, FFN_CHUNK)
    u = w_in[:, D_FF:].reshape(D_MODEL, N_FFN_CHUNKS, FFN_CHUNK)
    w_in_c = jnp.concatenate([a, u], axis=-1).transpose(1, 0, 2).astype(BF16)
    w_out_c = w_out.reshape(N_FFN_CHUNKS, FFN_CHUNK, D_MODEL).astype(BF16)
    return w_in_c, w_out_c


_SLOT_COLS = np.concatenate([np.arange(HEAD_DIM * h, HEAD_DIM * (h + 1)) for h in HEAD_OF_SLOT])


def kernel(x, ffn1_w_in, ffn1_w_out, ln1_g, ln1_b, mix_w_in, swa_sinks, w_branch_a, w_branch_b,
           mix_w_o, ln2_g, ln2_b, ffn2_w_in, ffn2_w_out, ln3_g, ln3_b):
    assert x.shape == (BATCH, SEQ, D_MODEL) and ffn1_w_in.shape[0] == 1
    w1_in, w1_out = _ffn_weights(ffn1_w_in[0], ffn1_w_out[0])
    w2_in, w2_out = _ffn_weights(ffn2_w_in[0], ffn2_w_out[0])
    wm = mix_w_in[0]
    wqkv = jnp.concatenate([wm[:, 0:512][:, _SLOT_COLS], wm[:, 512:768],
                            wm[:, 768:1280][:, _SLOT_COLS], wm[:, 1280:1536]], axis=1).astype(BF16)
    wg = wm[:, 1536:].astype(BF16)
    wa = w_branch_a[0][_SLOT_COLS, :].astype(BF16)
    wb = w_branch_b[0][_SLOT_COLS, :].astype(BF16)
    wo = mix_w_o[0].astype(BF16)

    x1, qa, ka, va, qb, kb, vb = _ffn_qkv(x.reshape(TOKENS, D_MODEL), w1_in, w1_out,
                                          ln1_g, ln1_b, wqkv)
    seq3 = lambda a: a.reshape(BATCH, SEQ, a.shape[-1])
    ya = _swa(swa_sinks[0], seq3(qa), seq3(ka), seq3(va))
    yb = _moba(seq3(qb), seq3(kb), seq3(vb))
    out = _mix_ffn(x1, ya.reshape(TOKENS, Q_COLS), yb.reshape(TOKENS, Q_COLS), wg, wa, wb, wo,
                   ln2_g, ln2_b, w2_in, w2_out, ln3_g, ln3_b)
    return out.reshape(BATCH, SEQ, D_MODEL)
```

```python
import numpy as np
import jax
import jax.numpy as jnp
from jax import lax
from jax.experimental import pallas as pl
from jax.experimental.pallas import tpu as pltpu

F32 = jnp.float32
BF16 = jnp.bfloat16

D_MODEL = 1024
BATCH = 8
SEQ = 2048
TOKENS = BATCH * SEQ
HEAD_DIM = 64
N_HEADS = 8
KV_COLS = 128
Q_COLS = 512
SWA_WINDOW = 128
MOBA_BLOCK = 256
MOBA_TOPK = 3
N_MOBA_BLOCKS = SEQ // MOBA_BLOCK
D_FF = 2816
FFN_CHUNK = 256
N_FFN_CHUNKS = D_FF // FFN_CHUNK
ALPHA = 2.0 ** 0.25
LN_EPS = 1e-5
NEG = -1e30
Q_SCALE = 0.125
ALIBI_SLOPES = tuple(float(2.0 ** (-8.0 * i / 16.0)) for i in range(1, 17))

HEAD_OF_SLOT = (0, 4, 1, 5, 2, 6, 3, 7)

V7X_VMEM_LIMIT_BYTES = 56 * 1024 * 1024
TOKEN_TILE = 512

SEL_LANES = 64
ROW_LANE = 64
COL_LANE = 67
BLK_LANE = 70


def _const_spec(shape):
    zeros = (0,) * len(shape)
    return pl.BlockSpec(shape, lambda *_: zeros, pipeline_mode=pl.Buffered(1))


def _split3(x):
    hi = x.astype(BF16)
    r1 = x - hi.astype(F32)
    mid = r1.astype(BF16)
    lo = (r1 - mid.astype(F32)).astype(BF16)
    return hi, mid, lo


def _layer_norm(y, g, b):
    mu = jnp.mean(y, axis=-1, keepdims=True)
    yc = y - mu
    var = jnp.mean(yc * yc, axis=-1, keepdims=True)
    return yc * lax.rsqrt(var + LN_EPS) * g + b


def _swiglu(xb, w_in_ref, w_out_ref, g_ref):
    for c in range(N_FFN_CHUNKS):
        lo = c * FFN_CHUNK
        a = jnp.dot(xb, w_in_ref[:, lo:lo + FFN_CHUNK], preferred_element_type=F32)
        u = jnp.dot(xb, w_in_ref[:, D_FF + lo:D_FF + lo + FFN_CHUNK], preferred_element_type=F32)
        g_ref[:, lo:lo + FFN_CHUNK] = (a * jax.nn.sigmoid(a) * u).astype(BF16)
    return jnp.dot(g_ref[...], w_out_ref[...], preferred_element_type=F32)


def _ffn_qkv_kernel(x_ref, w_in_ref, w_out_ref, g_ref, b_ref, wqkv_ref,
                    x1_ref, qa_ref, ka_ref, va_ref, qb_ref, kb_ref, vb_ref, act_ref):
    x = x_ref[...]
    f = _swiglu(x.astype(BF16), w_in_ref, w_out_ref, act_ref)
    x1 = _layer_norm(ALPHA * x + 0.5 * f, g_ref[...], b_ref[...])
    x1_ref[...] = x1
    h = jnp.dot(x1.astype(BF16), wqkv_ref[...], preferred_element_type=F32)
    qa_ref[...] = (h[:, 0:512] * Q_SCALE).astype(BF16)
    ka_ref[...] = h[:, 512:640].astype(BF16)
    va_ref[...] = h[:, 640:768].astype(BF16)
    qb_ref[...] = (h[:, 768:1280] * Q_SCALE).astype(BF16)
    kb_ref[...] = h[:, 1280:1408].astype(BF16)
    vb_ref[...] = h[:, 1408:1536].astype(BF16)


def _ffn_qkv(x, w_in, w_out, g, b, wqkv):
    tm = TOKEN_TILE
    row = lambda n: pl.BlockSpec((tm, n), lambda i: (i, 0))
    out_cols = (D_MODEL, Q_COLS, KV_COLS, KV_COLS, Q_COLS, KV_COLS, KV_COLS)
    out_dtypes = (F32,) + (BF16,) * 6
    return pl.pallas_call(
        _ffn_qkv_kernel,
        grid=(TOKENS // tm,),
        in_specs=[row(D_MODEL), _const_spec(w_in.shape), _const_spec(w_out.shape),
                  _const_spec(g.shape), _const_spec(b.shape), _const_spec(wqkv.shape)],
        out_specs=[row(n) for n in out_cols],
        out_shape=[jax.ShapeDtypeStruct((TOKENS, n), dt) for n, dt in zip(out_cols, out_dtypes)],
        scratch_shapes=[pltpu.VMEM((tm, D_FF), BF16)],
        compiler_params=pltpu.CompilerParams(
            dimension_semantics=("arbitrary",), vmem_limit_bytes=V7X_VMEM_LIMIT_BYTES),
        name="ffn_qkv",
    )(x, w_in, w_out, g, b, wqkv)


def _swa_kernel(sink_ref, q_ref, kp_ref, kc_ref, vp_ref, vc_ref, o_ref):
    n = pl.program_id(1)
    w = SWA_WINDOW
    k = jnp.concatenate([kp_ref[...], kc_ref[...]], axis=0)
    v = jnp.concatenate([vp_ref[...], vc_ref[...]], axis=0)
    lane = lax.broadcasted_iota(jnp.int32, (w, 128), 1)
    row = lax.broadcasted_iota(jnp.int32, (w, 2 * w), 0)
    col = lax.broadcasted_iota(jnp.int32, (w, 2 * w), 1)
    dist = row + w - col
    allowed = (dist >= 0) & (dist < w) & ((col >= w) | (n > 0))
    distf = dist.astype(F32)
    tiles = []
    for pair in range(4):
        qp = q_ref[:, pair * 128:(pair + 1) * 128]
        halves = []
        for half in range(2):
            head = HEAD_OF_SLOT[2 * pair + half]
            keep = (lane < 64) if half == 0 else (lane >= 64)
            qm = jnp.where(keep, qp, jnp.zeros_like(qp))
            s = lax.dot_general(qm, k, (((1,), (1,)), ((), ())), preferred_element_type=F32)
            s = s - ALIBI_SLOPES[head] * distf
            s = jnp.where(allowed, s, NEG)
            sink = sink_ref[head]
            m = jnp.maximum(jnp.max(s, axis=-1, keepdims=True), sink)
            p = jnp.exp(s - m)
            denom = jnp.sum(p, axis=-1, keepdims=True) + jnp.exp(sink - m)
            o = jnp.dot(p.astype(BF16), v, preferred_element_type=F32)
            halves.append(o / denom)
        tiles.append(jnp.where(lane < 64, halves[0], halves[1]))
    o_ref[...] = jnp.concatenate(tiles, axis=1).astype(BF16)


def _swa(sinks, qa, ka, va):
    w = SWA_WINDOW
    nb = SEQ // w
    cur = lambda b, n: (b, n, 0)
    prev = lambda b, n: (b, jnp.maximum(n - 1, 0), 0)
    return pl.pallas_call(
        _swa_kernel,
        grid=(BATCH, nb),
        in_specs=[pl.BlockSpec(memory_space=pltpu.SMEM),
                  pl.BlockSpec((None, w, Q_COLS), cur),
                  pl.BlockSpec((None, w, KV_COLS), prev),
                  pl.BlockSpec((None, w, KV_COLS), cur),
                  pl.BlockSpec((None, w, KV_COLS), prev),
                  pl.BlockSpec((None, w, KV_COLS), cur)],
        out_specs=pl.BlockSpec((None, w, Q_COLS), cur),
        out_shape=jax.ShapeDtypeStruct((BATCH, SEQ, Q_COLS), BF16),
        compiler_params=pltpu.CompilerParams(dimension_semantics=("arbitrary", "arbitrary")),
        name="swa",
    )(sinks, qa, ka, ka, va, va)


def _moba_kernel(q_ref, k_ref, v_ref, o_ref,
                 km_ref, qconst_ref, qaug_ref, u_ref, mpart_ref, lpart_ref, acc_ref):
    t = pl.program_id(1)
    bl = MOBA_BLOCK
    nbk = N_MOBA_BLOCKS
    rows = N_HEADS * bl
    lane = lax.broadcasted_iota(jnp.int32, (bl, 128), 1)
    lane_blk = lane % nbk
    lane_slot = lane // nbk
    sel_lane = lane < SEL_LANES
    in3 = lambda first: (lane >= first) & (lane < first + 3)

    def lanes3(first, terms):
        out = jnp.zeros((bl, 128), F32)
        for i, term in enumerate(terms):
            out = jnp.where(lane == first + i, term.astype(F32), out)
        return out

    @pl.when((pl.program_id(0) == 0) & (t == 0))
    def _():
        rowf = lax.broadcasted_iota(jnp.int32, (bl, 128), 0).astype(F32)
        for slot in range(N_HEADS):
            slope = jnp.full((bl, 128), ALIBI_SLOPES[N_HEADS + HEAD_OF_SLOT[slot]], F32)
            qc = (lanes3(ROW_LANE, _split3(-slope * rowf)) + lanes3(COL_LANE, _split3(slope))
                  + lanes3(BLK_LANE, _split3(-slope)))
            qconst_ref[slot * bl:(slot + 1) * bl, :] = qc.astype(BF16)

    @pl.when(t == 0)
    def _():
        rsel = lax.broadcasted_iota(jnp.int32, (128, SEQ), 0)
        tsel = lax.broadcasted_iota(jnp.int32, (128, SEQ), 1)
        avg = jnp.where((rsel % nbk) == (tsel // bl), 1.0 / bl, 0.0).astype(BF16)
        kmean_rows = jnp.dot(avg, k_ref[...], preferred_element_type=F32)
        kmean_t = kmean_rows.T
        r128 = lax.broadcasted_iota(jnp.int32, (128, 128), 0)
        c128 = lax.broadcasted_iota(jnp.int32, (128, 128), 1)
        for pair in range(4):
            slot_of_row = 2 * pair + (r128 >= 64).astype(jnp.int32)
            blk = jnp.where((c128 // nbk) == slot_of_row, kmean_t, 0.0)
            for i, term in enumerate(_split3(blk)):
                km_ref[i, pair * 128:(pair + 1) * 128, :] = term

    q = q_ref[...]
    gate = (jnp.dot(q, km_ref[0], preferred_element_type=F32)
            + jnp.dot(q, km_ref[1], preferred_element_type=F32)
            + jnp.dot(q, km_ref[2], preferred_element_type=F32))
    past = lane_blk < t
    gm = jnp.where(past, gate, -jnp.inf)
    rank = jnp.zeros((bl, 128), jnp.int32)
    for d in range(1, nbk):
        wrapped = lane_blk + d >= nbk
        partner = jnp.where(wrapped, pltpu.roll(gm, nbk - d, 1), pltpu.roll(gm, 128 - d, 1))
        beats = (partner > gm) | ((partner == gm) & wrapped)
        rank = rank + beats.astype(jnp.int32)
    selected = past & (rank < MOBA_TOPK) & sel_lane
    selb = jnp.where(selected, 0.0, NEG)

    for slot in range(N_HEADS):
        pair, half = divmod(slot, 2)
        qp = q[:, pair * 128:(pair + 1) * 128]
        keep = (lane < 64) if half == 0 else (lane >= 64)
        qm = jnp.where(keep, qp, jnp.zeros_like(qp))
        extra = jnp.where(sel_lane & (lane_slot == slot), selb,
                          qconst_ref[slot * bl:(slot + 1) * bl, :].astype(F32))
        qaug_ref[slot * bl:(slot + 1) * bl, :] = jnp.concatenate([qm, extra.astype(BF16)], axis=1)

    key_col = lax.broadcasted_iota(jnp.int32, (bl, 128), 0).astype(F32)

    def scores(j, diag):
        kj = k_ref[pl.ds(pl.multiple_of(j * bl, bl), bl), :]
        if diag:
            extra = jnp.where(in3(ROW_LANE), 1.0, jnp.where(in3(COL_LANE), key_col, 0.0))
        else:
            offset = ((t - j) * bl).astype(F32)
            extra = jnp.where(sel_lane, jnp.where(lane_blk == j, 1.0, 0.0),
                              jnp.where(in3(ROW_LANE), 1.0,
                                        jnp.where(in3(COL_LANE), key_col,
                                                  jnp.where(in3(BLK_LANE), offset, 0.0))))
        kaug = jnp.concatenate([kj, extra.astype(BF16)], axis=1)
        return lax.dot_general(qaug_ref[...], kaug, (((1,), (1,)), ((), ())),
                               preferred_element_type=F32)

    def fold(s):
        return jnp.maximum(s[:, :128], s[:, 128:])

    s = scores(t, True)
    qrow = lax.broadcasted_iota(jnp.int32, (rows, bl), 0) & (bl - 1)
    kcol = lax.broadcasted_iota(jnp.int32, (rows, bl), 1)
    s = jnp.where(qrow >= kcol, s, NEG)
    u_ref[t] = s
    mpart_ref[...] = fold(s)

    def pass1(j, carry):
        sj = scores(j, False)
        u_ref[j] = sj
        mpart_ref[...] = jnp.maximum(mpart_ref[...], fold(sj))
        return carry

    lax.fori_loop(0, t, pass1, 0)

    mpart_ref[...] = jnp.broadcast_to(jnp.max(mpart_ref[...], axis=-1, keepdims=True), (rows, 128))

    def pv(j):
        sj = u_ref[j]
        m = mpart_ref[...]
        p_lo = jnp.exp(sj[:, :128] - m)
        p_hi = jnp.exp(sj[:, 128:] - m)
        vj = v_ref[pl.ds(pl.multiple_of(j * bl, bl), bl), :]
        pb = jnp.concatenate([p_lo, p_hi], axis=1).astype(BF16)
        return p_lo + p_hi, jnp.dot(pb, vj, preferred_element_type=F32)

    lpart_ref[...], acc_ref[...] = pv(t)

    def pass2(j, carry):
        lp, pvj = pv(j)
        lpart_ref[...] += lp
        acc_ref[...] += pvj
        return carry

    lax.fori_loop(0, t, pass2, 0)

    out = acc_ref[...] / jnp.sum(lpart_ref[...], axis=-1, keepdims=True)
    tiles = []
    for pair in range(4):
        lo = out[(2 * pair) * bl:(2 * pair + 1) * bl, :]
        hi = out[(2 * pair + 1) * bl:(2 * pair + 2) * bl, :]
        tiles.append(jnp.where(lane < 64, lo, hi))
    o_ref[...] = jnp.concatenate(tiles, axis=1).astype(BF16)


def _moba(qb, kb, vb):
    bl = MOBA_BLOCK
    rows = N_HEADS * bl
    tile = lambda b, t: (b, t, 0)
    whole = lambda b, t: (b, 0, 0)
    return pl.pallas_call(
        _moba_kernel,
        grid=(BATCH, N_MOBA_BLOCKS),
        in_specs=[pl.BlockSpec((None, bl, Q_COLS), tile),
                  pl.BlockSpec((None, SEQ, KV_COLS), whole),
                  pl.BlockSpec((None, SEQ, KV_COLS), whole)],
        out_specs=pl.BlockSpec((None, bl, Q_COLS), tile),
        out_shape=jax.ShapeDtypeStruct((BATCH, SEQ, Q_COLS), BF16),
        scratch_shapes=[pltpu.VMEM((3, Q_COLS, 128), BF16),
                        pltpu.VMEM((rows, 128), BF16),
                        pltpu.VMEM((rows, 256), BF16),
                        pltpu.VMEM((N_MOBA_BLOCKS, rows, bl), F32),
                        pltpu.VMEM((rows, 128), F32),
                        pltpu.VMEM((rows, 128), F32),
                        pltpu.VMEM((rows, 128), F32)],
        compiler_params=pltpu.CompilerParams(
            dimension_semantics=("arbitrary", "arbitrary"), vmem_limit_bytes=V7X_VMEM_LIMIT_BYTES),
        name="moba",
    )(qb, kb, vb)


def _mix_ffn_kernel(x1_ref, ya_ref, yb_ref, wg_ref, wa_ref, wb_ref, wo_ref, g2_ref, b2_ref,
                    w_in_ref, w_out_ref, g3_ref, b3_ref, o_ref, act_ref):
    x1 = x1_ref[...]
    x1b = x1.astype(BF16)
    ga = jnp.dot(x1b, wg_ref[:, :D_MODEL], preferred_element_type=F32)
    ya = jnp.dot(ya_ref[...], wa_ref[...], preferred_element_type=F32)
    y = jax.nn.sigmoid(ga) * ya
    gb = jnp.dot(x1b, wg_ref[:, D_MODEL:], preferred_element_type=F32)
    yb = jnp.dot(yb_ref[...], wb_ref[...], preferred_element_type=F32)
    y = y + jax.nn.sigmoid(gb) * yb
    z = jnp.dot(y.astype(BF16), wo_ref[...], preferred_element_type=F32)
    x2 = _layer_norm(ALPHA * x1 + z, g2_ref[...], b2_ref[...])
    f = _swiglu(x2.astype(BF16), w_in_ref, w_out_ref, act_ref)
    o_ref[...] = _layer_norm(ALPHA * x2 + 0.5 * f, g3_ref[...], b3_ref[...])


def _mix_ffn(x1, ya, yb, wg, wa, wb, wo, g2, b2, w_in, w_out, g3, b3):
    tm = TOKEN_TILE
    row = lambda n: pl.BlockSpec((tm, n), lambda i: (i, 0))
    consts = (wg, wa, wb, wo, g2, b2, w_in, w_out, g3, b3)
    return pl.pallas_call(
        _mix_ffn_kernel,
        grid=(TOKENS // tm,),
        in_specs=[row(D_MODEL), row(Q_COLS), row(Q_COLS)] + [_const_spec(c.shape) for c in consts],
        out_specs=row(D_MODEL),
        out_shape=jax.ShapeDtypeStruct((TOKENS, D_MODEL), F32),
        scratch_shapes=[pltpu.VMEM((tm, D_FF), BF16)],
        compiler_params=pltpu.CompilerParams(
            dimension_semantics=("arbitrary",), vmem_limit_bytes=V7X_VMEM_LIMIT_BYTES),
        name="mix_ffn",
    )(x1, ya, yb, *consts)


_SLOT_COLS = np.concatenate([np.arange(HEAD_DIM * h, HEAD_DIM * (h + 1)) for h in HEAD_OF_SLOT])


def kernel(x, ffn1_w_in, ffn1_w_out, ln1_g, ln1_b, mix_w_in, swa_sinks, w_branch_a, w_branch_b,
           mix_w_o, ln2_g, ln2_b, ffn2_w_in, ffn2_w_out, ln3_g, ln3_b):
    assert x.shape == (BATCH, SEQ, D_MODEL) and ffn1_w_in.shape[0] == 1
    wm = mix_w_in[0]
    wqkv = jnp.concatenate([wm[:, 0:512][:, _SLOT_COLS], wm[:, 512:768],
                            wm[:, 768:1280][:, _SLOT_COLS], wm[:, 1280:1536]], axis=1).astype(BF16)
    wg = wm[:, 1536:].astype(BF16)
    wa = w_branch_a[0][_SLOT_COLS, :].astype(BF16)
    wb = w_branch_b[0][_SLOT_COLS, :].astype(BF16)
    wo = mix_w_o[0].astype(BF16)

    x1, qa, ka, va, qb, kb, vb = _ffn_qkv(x.reshape(TOKENS, D_MODEL), ffn1_w_in[0].astype(BF16),
                                          ffn1_w_out[0].astype(BF16), ln1_g, ln1_b, wqkv)
    seq3 = lambda a: a.reshape(BATCH, SEQ, a.shape[-1])
    ya = _swa(swa_sinks[0], seq3(qa), seq3(ka), seq3(va))
    yb = _moba(seq3(qb), seq3(kb), seq3(vb))
    out = _mix_ffn(x1, ya.reshape(TOKENS, Q_COLS), yb.reshape(TOKENS, Q_COLS), wg, wa, wb, wo,
                   ln2_g, ln2_b, ffn2_w_in[0].astype(BF16), ffn2_w_out[0].astype(BF16), ln3_g, ln3_b)
    return out.reshape(BATCH, SEQ, D_MODEL)
```

```python
import numpy as np
import jax
import jax.numpy as jnp
from jax import lax
from jax.experimental import pallas as pl
from jax.experimental.pallas import tpu as pltpu

F32 = jnp.float32
BF16 = jnp.bfloat16

D_MODEL = 1024
BATCH = 8
SEQ = 2048
TOKENS = BATCH * SEQ
HEAD_DIM = 64
N_HEADS = 8
KV_COLS = 128
Q_COLS = 512
SWA_WINDOW = 128
SWA_TILE = 512
MOBA_BLOCK = 256
MOBA_TOPK = 3
N_MOBA_BLOCKS = SEQ // MOBA_BLOCK
D_FF = 2816
FFN_CHUNK = 256
N_FFN_CHUNKS = D_FF // FFN_CHUNK
ALPHA = 2.0 ** 0.25
LN_EPS = 1e-5
NEG = -1e30
LOG2E = 1.4426950408889634
Q_SCALE = 0.125 * LOG2E
ALIBI_SLOPES = tuple(float(2.0 ** (-8.0 * i / 16.0)) * LOG2E for i in range(1, 17))

HEAD_OF_SLOT = (0, 4, 1, 5, 2, 6, 3, 7)

V7X_VMEM_LIMIT_BYTES = 56 * 1024 * 1024
TOKEN_TILE = 512

SEL_LANES = 64
ROW_LANE = 64
COL_LANE = 67
BLK_LANE = 70


def _const_spec(shape):
    zeros = (0,) * len(shape)
    return pl.BlockSpec(shape, lambda *_: zeros, pipeline_mode=pl.Buffered(1))


def _split3(x):
    hi = x.astype(BF16)
    r1 = x - hi.astype(F32)
    mid = r1.astype(BF16)
    lo = (r1 - mid.astype(F32)).astype(BF16)
    return hi, mid, lo


def _layer_norm(y, g, b):
    mu = jnp.mean(y, axis=-1, keepdims=True)
    yc = y - mu
    var = jnp.mean(yc * yc, axis=-1, keepdims=True)
    return yc * lax.rsqrt(var + LN_EPS) * g + b


def _swiglu(xb, w_in_ref, w_out_ref, g_ref):
    for c in range(N_FFN_CHUNKS):
        lo = c * FFN_CHUNK
        a = jnp.dot(xb, w_in_ref[:, lo:lo + FFN_CHUNK], preferred_element_type=F32)
        u = jnp.dot(xb, w_in_ref[:, D_FF + lo:D_FF + lo + FFN_CHUNK], preferred_element_type=F32)
        g_ref[:, lo:lo + FFN_CHUNK] = (a * jax.nn.sigmoid(a) * u).astype(BF16)
    return jnp.dot(g_ref[...], w_out_ref[...], preferred_element_type=F32)


def _ffn_qkv_kernel(x_ref, w_in_ref, w_out_ref, g_ref, b_ref, wqkv_ref,
                    x1_ref, qa_ref, ka_ref, va_ref, qb_ref, kb_ref, vb_ref, act_ref):
    x = x_ref[...]
    f = _swiglu(x.astype(BF16), w_in_ref, w_out_ref, act_ref)
    x1 = _layer_norm(ALPHA * x + 0.5 * f, g_ref[...], b_ref[...])
    x1_ref[...] = x1
    h = jnp.dot(x1.astype(BF16), wqkv_ref[...], preferred_element_type=F32)
    qa_ref[...] = (h[:, 0:512] * Q_SCALE).astype(BF16)
    ka_ref[...] = h[:, 512:640].astype(BF16)
    va_ref[...] = h[:, 640:768].astype(BF16)
    qb_ref[...] = (h[:, 768:1280] * Q_SCALE).astype(BF16)
    kb_ref[...] = h[:, 1280:1408].astype(BF16)
    vb_ref[...] = h[:, 1408:1536].astype(BF16)


def _ffn_qkv(x, w_in, w_out, g, b, wqkv):
    tm = TOKEN_TILE
    row = lambda n: pl.BlockSpec((tm, n), lambda i: (i, 0))
    out_cols = (D_MODEL, Q_COLS, KV_COLS, KV_COLS, Q_COLS, KV_COLS, KV_COLS)
    out_dtypes = (F32,) + (BF16,) * 6
    return pl.pallas_call(
        _ffn_qkv_kernel,
        grid=(TOKENS // tm,),
        in_specs=[row(D_MODEL), _const_spec(w_in.shape), _const_spec(w_out.shape),
                  _const_spec(g.shape), _const_spec(b.shape), _const_spec(wqkv.shape)],
        out_specs=[row(n) for n in out_cols],
        out_shape=[jax.ShapeDtypeStruct((TOKENS, n), dt) for n, dt in zip(out_cols, out_dtypes)],
        scratch_shapes=[pltpu.VMEM((tm, D_FF), BF16)],
        compiler_params=pltpu.CompilerParams(
            dimension_semantics=("arbitrary",), vmem_limit_bytes=V7X_VMEM_LIMIT_BYTES),
        name="ffn_qkv",
    )(x, w_in, w_out, g, b, wqkv)


SWA_ROW_SLOTS = (0, 2, 4, 6, 1, 3, 5, 7)


def _swa_kernel(sink_ref, q_ref, kp_ref, kc_ref, vp_ref, vc_ref, o_ref, qconst_ref, bias_ref):
    i = pl.program_id(1)
    w = SWA_WINDOW
    rows = N_HEADS * w
    lane = lax.broadcasted_iota(jnp.int32, (w, 128), 1)

    @pl.when((pl.program_id(0) == 0) & (i == 0))
    def _():
        rowf = lax.broadcasted_iota(jnp.int32, (w, 128), 0).astype(F32)
        for rb, slot in enumerate(SWA_ROW_SLOTS):
            slope = jnp.full((w, 128), ALIBI_SLOPES[HEAD_OF_SLOT[slot]], F32)
            qc = jnp.zeros((w, 128), F32)
            for first, terms in ((ROW_LANE, _split3(-slope * (rowf + w))), (COL_LANE, _split3(slope))):
                for n, term in enumerate(terms):
                    qc = jnp.where(lane == first + n, term.astype(F32), qc)
            qconst_ref[rb * w:(rb + 1) * w, :] = qc.astype(BF16)
        dist = (lax.broadcasted_iota(jnp.int32, (w, 2 * w), 0) + w
                - lax.broadcasted_iota(jnp.int32, (w, 2 * w), 1))
        bias_ref[...] = jnp.where((dist >= 0) & (dist < w), 0.0, NEG)

    k_all = jnp.concatenate([kp_ref[...], kc_ref[...]], axis=0)
    v_all = jnp.concatenate([vp_ref[...], vc_ref[...]], axis=0)
    key_col = lax.broadcasted_iota(jnp.int32, (2 * w, 128), 0).astype(F32)
    lane2 = lax.broadcasted_iota(jnp.int32, (2 * w, 128), 1)
    k_extra = jnp.where((lane2 >= ROW_LANE) & (lane2 < ROW_LANE + 3), 1.0,
                        jnp.where((lane2 >= COL_LANE) & (lane2 < COL_LANE + 3), key_col, 0.0)).astype(BF16)
    sink = jnp.concatenate([jnp.full((w, 128), sink_ref[HEAD_OF_SLOT[slot]] * LOG2E, F32)
                            for slot in SWA_ROW_SLOTS], axis=0)
    ones = jnp.ones((2 * w, 128), BF16)
    bias = bias_ref[...]
    kcol = lax.broadcasted_iota(jnp.int32, (w, 2 * w), 1)
    bias_first = jnp.where((kcol >= w) | (i > 0), bias, NEG)

    for r in range(SWA_TILE // w):
        q = q_ref[r * w:(r + 1) * w, :]
        q_aug = jnp.concatenate(
            [jnp.concatenate([q[:, (slot // 2) * 128:(slot // 2 + 1) * 128],
                              qconst_ref[rb * w:(rb + 1) * w, :]], axis=1)
             for rb, slot in enumerate(SWA_ROW_SLOTS)], axis=0)
        k = k_all[r * w:(r + 2) * w, :]
        zero = jnp.zeros_like(k)
        nt = (((1,), (1,)), ((), ()))
        s0 = lax.dot_general(q_aug[:rows // 2], jnp.concatenate([jnp.where(lane2 < 64, k, zero), k_extra], axis=1),
                             nt, preferred_element_type=F32)
        s1 = lax.dot_general(q_aug[rows // 2:], jnp.concatenate([jnp.where(lane2 >= 64, k, zero), k_extra], axis=1),
                             nt, preferred_element_type=F32)
        b = bias_first if r == 0 else bias
        s = (jnp.concatenate([s0, s1], axis=0).reshape(N_HEADS, w, 2 * w) + b[None]).reshape(rows, 2 * w)
        row_max = jnp.max(jnp.maximum(s[:, :128], s[:, 128:]), axis=-1, keepdims=True)
        m = jnp.maximum(jnp.broadcast_to(row_max, (rows, 128)), sink)
        p = jnp.exp2(s - jnp.concatenate([m, m], axis=1)).astype(BF16)
        v_aug = jnp.concatenate([v_all[r * w:(r + 2) * w, :], ones], axis=1)
        pv = jnp.dot(p, v_aug, preferred_element_type=F32)
        out = pv[:, :128] / (pv[:, 128:] + jnp.exp2(sink - m))
        tiles = []
        for pair in range(4):
            lo = out[pair * w:(pair + 1) * w, :]
            hi = out[(4 + pair) * w:(5 + pair) * w, :]
            tiles.append(jnp.where(lane < 64, lo, hi))
        o_ref[r * w:(r + 1) * w, :] = jnp.concatenate(tiles, axis=1).astype(BF16)


def _swa(sinks, qa, ka, va):
    w = SWA_WINDOW
    per_tile = SWA_TILE // w
    cur = lambda b, i: (b, i, 0)
    prev = lambda b, i: (b, jnp.maximum(i * per_tile - 1, 0), 0)
    return pl.pallas_call(
        _swa_kernel,
        grid=(BATCH, SEQ // SWA_TILE),
        in_specs=[pl.BlockSpec(memory_space=pltpu.SMEM),
                  pl.BlockSpec((None, SWA_TILE, Q_COLS), cur),
                  pl.BlockSpec((None, w, KV_COLS), prev),
                  pl.BlockSpec((None, SWA_TILE, KV_COLS), cur),
                  pl.BlockSpec((None, w, KV_COLS), prev),
                  pl.BlockSpec((None, SWA_TILE, KV_COLS), cur)],
        out_specs=pl.BlockSpec((None, SWA_TILE, Q_COLS), cur),
        out_shape=jax.ShapeDtypeStruct((BATCH, SEQ, Q_COLS), BF16),
        scratch_shapes=[pltpu.VMEM((N_HEADS * w, 128), BF16),
                        pltpu.VMEM((w, 2 * w), F32)],
        compiler_params=pltpu.CompilerParams(dimension_semantics=("arbitrary", "arbitrary")),
        name="swa",
    )(sinks, qa, ka, ka, va, va)


def _moba_kernel(q_ref, k_ref, v_ref, o_ref,
                 km_ref, qconst_ref, qaug_ref, u_ref, mpart_ref, lpart_ref, acc_ref):
    t = pl.program_id(1)
    bl = MOBA_BLOCK
    nbk = N_MOBA_BLOCKS
    rows = N_HEADS * bl
    lane = lax.broadcasted_iota(jnp.int32, (bl, 128), 1)
    lane_blk = lane % nbk
    lane_slot = lane // nbk
    sel_lane = lane < SEL_LANES
    in3 = lambda first: (lane >= first) & (lane < first + 3)

    def lanes3(first, terms):
        out = jnp.zeros((bl, 128), F32)
        for i, term in enumerate(terms):
            out = jnp.where(lane == first + i, term.astype(F32), out)
        return out

    @pl.when((pl.program_id(0) == 0) & (t == 0))
    def _():
        rowf = lax.broadcasted_iota(jnp.int32, (bl, 128), 0).astype(F32)
        for slot in range(N_HEADS):
            slope = jnp.full((bl, 128), ALIBI_SLOPES[N_HEADS + HEAD_OF_SLOT[slot]], F32)
            qc = (lanes3(ROW_LANE, _split3(-slope * rowf)) + lanes3(COL_LANE, _split3(slope))
                  + lanes3(BLK_LANE, _split3(-slope)))
            qconst_ref[slot * bl:(slot + 1) * bl, :] = qc.astype(BF16)

    @pl.when(t == 0)
    def _():
        rsel = lax.broadcasted_iota(jnp.int32, (128, SEQ), 0)
        tsel = lax.broadcasted_iota(jnp.int32, (128, SEQ), 1)
        avg = jnp.where((rsel % nbk) == (tsel // bl), 1.0 / bl, 0.0).astype(BF16)
        kmean_rows = jnp.dot(avg, k_ref[...], preferred_element_type=F32)
        kmean_t = kmean_rows.T
        r128 = lax.broadcasted_iota(jnp.int32, (128, 128), 0)
        c128 = lax.broadcasted_iota(jnp.int32, (128, 128), 1)
        for pair in range(4):
            slot_of_row = 2 * pair + (r128 >= 64).astype(jnp.int32)
            blk = jnp.where((c128 // nbk) == slot_of_row, kmean_t, 0.0)
            for i, term in enumerate(_split3(blk)):
                km_ref[i, pair * 128:(pair + 1) * 128, :] = term

    q = q_ref[...]
    gate = (jnp.dot(q, km_ref[0], preferred_element_type=F32)
            + jnp.dot(q, km_ref[1], preferred_element_type=F32)
            + jnp.dot(q, km_ref[2], preferred_element_type=F32))
    past = lane_blk < t
    gm = jnp.where(past, gate, -jnp.inf)
    rank = jnp.zeros((bl, 128), jnp.int32)
    for d in range(1, nbk):
        wrapped = lane_blk + d >= nbk
        partner = jnp.where(wrapped, pltpu.roll(gm, nbk - d, 1), pltpu.roll(gm, 128 - d, 1))
        beats = (partner > gm) | ((partner == gm) & wrapped)
        rank = rank + beats.astype(jnp.int32)
    selected = past & (rank < MOBA_TOPK) & sel_lane
    selb = jnp.where(selected, 0.0, NEG)

    for slot in range(N_HEADS):
        pair, half = divmod(slot, 2)
        qp = q[:, pair * 128:(pair + 1) * 128]
        keep = (lane < 64) if half == 0 else (lane >= 64)
        qm = jnp.where(keep, qp, jnp.zeros_like(qp))
        extra = jnp.where(sel_lane & (lane_slot == slot), selb,
                          qconst_ref[slot * bl:(slot + 1) * bl, :].astype(F32))
        qaug_ref[slot * bl:(slot + 1) * bl, :] = jnp.concatenate([qm, extra.astype(BF16)], axis=1)

    key_col = lax.broadcasted_iota(jnp.int32, (bl, 128), 0).astype(F32)

    def scores(j, diag):
        kj = k_ref[pl.ds(pl.multiple_of(j * bl, bl), bl), :]
        if diag:
            extra = jnp.where(in3(ROW_LANE), 1.0, jnp.where(in3(COL_LANE), key_col, 0.0))
        else:
            offset = ((t - j) * bl).astype(F32)
            extra = jnp.where(sel_lane, jnp.where(lane_blk == j, 1.0, 0.0),
                              jnp.where(in3(ROW_LANE), 1.0,
                                        jnp.where(in3(COL_LANE), key_col,
                                                  jnp.where(in3(BLK_LANE), offset, 0.0))))
        kaug = jnp.concatenate([kj, extra.astype(BF16)], axis=1)
        return lax.dot_general(qaug_ref[...], kaug, (((1,), (1,)), ((), ())),
                               preferred_element_type=F32)

    def fold(s):
        return jnp.maximum(s[:, :128], s[:, 128:])

    s = scores(t, True)
    qrow = lax.broadcasted_iota(jnp.int32, (rows, bl), 0) & (bl - 1)
    kcol = lax.broadcasted_iota(jnp.int32, (rows, bl), 1)
    s = jnp.where(qrow >= kcol, s, NEG)
    u_ref[t] = s
    mpart_ref[...] = fold(s)

    def pass1(j, carry):
        sj = scores(j, False)
        u_ref[j] = sj
        mpart_ref[...] = jnp.maximum(mpart_ref[...], fold(sj))
        return carry

    lax.fori_loop(0, t, pass1, 0)

    mpart_ref[...] = jnp.broadcast_to(jnp.max(mpart_ref[...], axis=-1, keepdims=True), (rows, 128))

    def pv(j):
        sj = u_ref[j]
        m = mpart_ref[...]
        p_lo = jnp.exp2(sj[:, :128] - m)
        p_hi = jnp.exp2(sj[:, 128:] - m)
        vj = v_ref[pl.ds(pl.multiple_of(j * bl, bl), bl), :]
        pb = jnp.concatenate([p_lo, p_hi], axis=1).astype(BF16)
        return p_lo + p_hi, jnp.dot(pb, vj, preferred_element_type=F32)

    lpart_ref[...], acc_ref[...] = pv(t)

    def pass2(j, carry):
        lp, pvj = pv(j)
        lpart_ref[...] += lp
        acc_ref[...] += pvj
        return carry

    lax.fori_loop(0, t, pass2, 0)

    out = acc_ref[...] / jnp.sum(lpart_ref[...], axis=-1, keepdims=True)
    tiles = []
    for pair in range(4):
        lo = out[(2 * pair) * bl:(2 * pair + 1) * bl, :]
        hi = out[(2 * pair + 1) * bl:(2 * pair + 2) * bl, :]
        tiles.append(jnp.where(lane < 64, lo, hi))
    o_ref[...] = jnp.concatenate(tiles, axis=1).astype(BF16)


def _moba(qb, kb, vb):
    bl = MOBA_BLOCK
    rows = N_HEADS * bl
    tile = lambda b, t: (b, t, 0)
    whole = lambda b, t: (b, 0, 0)
    return pl.pallas_call(
        _moba_kernel,
        grid=(BATCH, N_MOBA_BLOCKS),
        in_specs=[pl.BlockSpec((None, bl, Q_COLS), tile),
                  pl.BlockSpec((None, SEQ, KV_COLS), whole),
                  pl.BlockSpec((None, SEQ, KV_COLS), whole)],
        out_specs=pl.BlockSpec((None, bl, Q_COLS), tile),
        out_shape=jax.ShapeDtypeStruct((BATCH, SEQ, Q_COLS), BF16),
        scratch_shapes=[pltpu.VMEM((3, Q_COLS, 128), BF16),
                        pltpu.VMEM((rows, 128), BF16),
                        pltpu.VMEM((rows, 256), BF16),
                        pltpu.VMEM((N_MOBA_BLOCKS, rows, bl), F32),
                        pltpu.VMEM((rows, 128), F32),
                        pltpu.VMEM((rows, 128), F32),
                        pltpu.VMEM((rows, 128), F32)],
        compiler_params=pltpu.CompilerParams(
            dimension_semantics=("arbitrary", "arbitrary"), vmem_limit_bytes=V7X_VMEM_LIMIT_BYTES),
        name="moba",
    )(qb, kb, vb)


def _mix_ffn_kernel(x1_ref, ya_ref, yb_ref, wg_ref, wa_ref, wb_ref, wo_ref, g2_ref, b2_ref,
                    w_in_ref, w_out_ref, g3_ref, b3_ref, o_ref, act_ref):
    x1 = x1_ref[...]
    x1b = x1.astype(BF16)
    ga = jnp.dot(x1b, wg_ref[:, :D_MODEL], preferred_element_type=F32)
    ya = jnp.dot(ya_ref[...], wa_ref[...], preferred_element_type=F32)
    y = jax.nn.sigmoid(ga) * ya
    gb = jnp.dot(x1b, wg_ref[:, D_MODEL:], preferred_element_type=F32)
    yb = jnp.dot(yb_ref[...], wb_ref[...], preferred_element_type=F32)
    y = y + jax.nn.sigmoid(gb) * yb
    z = jnp.dot(y.astype(BF16), wo_ref[...], preferred_element_type=F32)
    x2 = _layer_norm(ALPHA * x1 + z, g2_ref[...], b2_ref[...])
    f = _swiglu(x2.astype(BF16), w_in_ref, w_out_ref, act_ref)
    o_ref[...] = _layer_norm(ALPHA * x2 + 0.5 * f, g3_ref[...], b3_ref[...])


def _mix_ffn(x1, ya, yb, wg, wa, wb, wo, g2, b2, w_in, w_out, g3, b3):
    tm = TOKEN_TILE
    row = lambda n: pl.BlockSpec((tm, n), lambda i: (i, 0))
    consts = (wg, wa, wb, wo, g2, b2, w_in, w_out, g3, b3)
    return pl.pallas_call(
        _mix_ffn_kernel,
        grid=(TOKENS // tm,),
        in_specs=[row(D_MODEL), row(Q_COLS), row(Q_COLS)] + [_const_spec(c.shape) for c in consts],
        out_specs=row(D_MODEL),
        out_shape=jax.ShapeDtypeStruct((TOKENS, D_MODEL), F32),
        scratch_shapes=[pltpu.VMEM((tm, D_FF), BF16)],
        compiler_params=pltpu.CompilerParams(
            dimension_semantics=("arbitrary",), vmem_limit_bytes=V7X_VMEM_LIMIT_BYTES),
        name="mix_ffn",
    )(x1, ya, yb, *consts)


_SLOT_COLS = np.concatenate([np.arange(HEAD_DIM * h, HEAD_DIM * (h + 1)) for h in HEAD_OF_SLOT])


def kernel(x, ffn1_w_in, ffn1_w_out, ln1_g, ln1_b, mix_w_in, swa_sinks, w_branch_a, w_branch_b,
           mix_w_o, ln2_g, ln2_b, ffn2_w_in, ffn2_w_out, ln3_g, ln3_b):
    assert x.shape == (BATCH, SEQ, D_MODEL) and ffn1_w_in.shape[0] == 1
    wm = mix_w_in[0]
    wqkv = jnp.concatenate([wm[:, 0:512][:, _SLOT_COLS], wm[:, 512:768],
                            wm[:, 768:1280][:, _SLOT_COLS], wm[:, 1280:1536]], axis=1).astype(BF16)
    wg = wm[:, 1536:].astype(BF16)
    wa = w_branch_a[0][_SLOT_COLS, :].astype(BF16)
    wb = w_branch_b[0][_SLOT_COLS, :].astype(BF16)
    wo = mix_w_o[0].astype(BF16)

    x1, qa, ka, va, qb, kb, vb = _ffn_qkv(x.reshape(TOKENS, D_MODEL), ffn1_w_in[0].astype(BF16),
                                          ffn1_w_out[0].astype(BF16), ln1_g, ln1_b, wqkv)
    seq3 = lambda a: a.reshape(BATCH, SEQ, a.shape[-1])
    ya = _swa(swa_sinks[0], seq3(qa), seq3(ka), seq3(va))
    yb = _moba(seq3(qb), seq3(kb), seq3(vb))
    out = _mix_ffn(x1, ya.reshape(TOKENS, Q_COLS), yb.reshape(TOKENS, Q_COLS), wg, wa, wb, wo,
                   ln2_g, ln2_b, ffn2_w_in[0].astype(BF16), ffn2_w_out[0].astype(BF16), ln3_g, ln3_b)
    return out.reshape(BATCH, SEQ, D_MODEL)
```

```python
import numpy as np
import jax
import jax.numpy as jnp
from jax import lax
from jax.experimental import pallas as pl
from jax.experimental.pallas import tpu as pltpu

F32 = jnp.float32
BF16 = jnp.bfloat16

D_MODEL = 1024
BATCH = 8
SEQ = 2048
TOKENS = BATCH * SEQ
HEAD_DIM = 64
N_HEADS = 8
KV_COLS = 128
Q_COLS = 512
SWA_WINDOW = 128
SWA_TILE = 512
MOBA_BLOCK = 256
MOBA_TOPK = 3
N_MOBA_BLOCKS = SEQ // MOBA_BLOCK
D_FF = 2816
FFN_CHUNK = 256
N_FFN_CHUNKS = D_FF // FFN_CHUNK
ALPHA = 2.0 ** 0.25
LN_EPS = 1e-5
NEG = -1e30
LOG2E = 1.4426950408889634
Q_SCALE = 0.125 * LOG2E
ALIBI_SLOPES = tuple(float(2.0 ** (-8.0 * i / 16.0)) * LOG2E for i in range(1, 17))

HEAD_OF_SLOT = (0, 4, 1, 5, 2, 6, 3, 7)

V7X_VMEM_LIMIT_BYTES = 56 * 1024 * 1024
TOKEN_TILE = 512

SEL_LANES = 64
ROW_LANE = 64
COL_LANE = 67
BLK_LANE = 70


def _const_spec(shape):
    zeros = (0,) * len(shape)
    return pl.BlockSpec(shape, lambda *_: zeros, pipeline_mode=pl.Buffered(1))


def _split3(x):
    hi = x.astype(BF16)
    r1 = x - hi.astype(F32)
    mid = r1.astype(BF16)
    lo = (r1 - mid.astype(F32)).astype(BF16)
    return hi, mid, lo


def _layer_norm(y, g, b):
    mu = jnp.mean(y, axis=-1, keepdims=True)
    yc = y - mu
    var = jnp.mean(yc * yc, axis=-1, keepdims=True)
    return yc * lax.rsqrt(var + LN_EPS) * g + b


def _swiglu(xb, w_in_ref, w_out_ref, g_ref):
    for c in range(N_FFN_CHUNKS):
        lo = c * FFN_CHUNK
        a = jnp.dot(xb, w_in_ref[:, lo:lo + FFN_CHUNK], preferred_element_type=F32)
        u = jnp.dot(xb, w_in_ref[:, D_FF + lo:D_FF + lo + FFN_CHUNK], preferred_element_type=F32)
        g_ref[:, lo:lo + FFN_CHUNK] = (a * jax.nn.sigmoid(a) * u).astype(BF16)
    return jnp.dot(g_ref[...], w_out_ref[...], preferred_element_type=F32)


def _ffn_qkv_kernel(x_ref, w_in_ref, w_out_ref, g_ref, b_ref, wqkv_ref,
                    x1_ref, qa_ref, ka_ref, va_ref, qb_ref, kb_ref, vb_ref, act_ref):
    x = x_ref[...]
    f = _swiglu(x.astype(BF16), w_in_ref, w_out_ref, act_ref)
    x1 = _layer_norm(ALPHA * x + 0.5 * f, g_ref[...], b_ref[...])
    x1_ref[...] = x1
    h = jnp.dot(x1.astype(BF16), wqkv_ref[...], preferred_element_type=F32)
    qa_ref[...] = (h[:, 0:512] * Q_SCALE).astype(BF16)
    ka_ref[...] = h[:, 512:640].astype(BF16)
    va_ref[...] = h[:, 640:768].astype(BF16)
    qb_ref[...] = (h[:, 768:1280] * Q_SCALE).astype(BF16)
    kb_ref[...] = h[:, 1280:1408].astype(BF16)
    vb_ref[...] = h[:, 1408:1536].astype(BF16)


def _ffn_qkv(x, w_in, w_out, g, b, wqkv):
    tm = TOKEN_TILE
    row = lambda n: pl.BlockSpec((tm, n), lambda i: (i, 0))
    out_cols = (D_MODEL, Q_COLS, KV_COLS, KV_COLS, Q_COLS, KV_COLS, KV_COLS)
    out_dtypes = (F32,) + (BF16,) * 6
    return pl.pallas_call(
        _ffn_qkv_kernel,
        grid=(TOKENS // tm,),
        in_specs=[row(D_MODEL), _const_spec(w_in.shape), _const_spec(w_out.shape),
                  _const_spec(g.shape), _const_spec(b.shape), _const_spec(wqkv.shape)],
        out_specs=[row(n) for n in out_cols],
        out_shape=[jax.ShapeDtypeStruct((TOKENS, n), dt) for n, dt in zip(out_cols, out_dtypes)],
        scratch_shapes=[pltpu.VMEM((tm, D_FF), BF16)],
        compiler_params=pltpu.CompilerParams(
            dimension_semantics=("arbitrary",), vmem_limit_bytes=V7X_VMEM_LIMIT_BYTES),
        name="ffn_qkv",
    )(x, w_in, w_out, g, b, wqkv)


SWA_ROW_SLOTS = (0, 2, 4, 6, 1, 3, 5, 7)


def _swa_kernel(sink_ref, q_ref, kp_ref, kc_ref, vp_ref, vc_ref, o_ref, qconst_ref, bias_ref):
    i = pl.program_id(1)
    w = SWA_WINDOW
    rows = N_HEADS * w
    lane = lax.broadcasted_iota(jnp.int32, (w, 128), 1)

    @pl.when((pl.program_id(0) == 0) & (i == 0))
    def _():
        rowf = lax.broadcasted_iota(jnp.int32, (w, 128), 0).astype(F32)
        for rb, slot in enumerate(SWA_ROW_SLOTS):
            slope = jnp.full((w, 128), ALIBI_SLOPES[HEAD_OF_SLOT[slot]], F32)
            qc = jnp.zeros((w, 128), F32)
            for first, terms in ((ROW_LANE, _split3(-slope * (rowf + w))), (COL_LANE, _split3(slope))):
                for n, term in enumerate(terms):
                    qc = jnp.where(lane == first + n, term.astype(F32), qc)
            qconst_ref[rb * w:(rb + 1) * w, :] = qc.astype(BF16)
        dist = (lax.broadcasted_iota(jnp.int32, (w, 2 * w), 0) + w
                - lax.broadcasted_iota(jnp.int32, (w, 2 * w), 1))
        bias_ref[...] = jnp.where((dist >= 0) & (dist < w), 0.0, NEG)

    k_all = jnp.concatenate([kp_ref[...], kc_ref[...]], axis=0)
    v_all = jnp.concatenate([vp_ref[...], vc_ref[...]], axis=0)
    key_col = lax.broadcasted_iota(jnp.int32, (2 * w, 128), 0).astype(F32)
    lane2 = lax.broadcasted_iota(jnp.int32, (2 * w, 128), 1)
    k_extra = jnp.where((lane2 >= ROW_LANE) & (lane2 < ROW_LANE + 3), 1.0,
                        jnp.where((lane2 >= COL_LANE) & (lane2 < COL_LANE + 3), key_col, 0.0)).astype(BF16)
    sink = jnp.concatenate([jnp.full((w, 128), sink_ref[HEAD_OF_SLOT[slot]] * LOG2E, F32)
                            for slot in SWA_ROW_SLOTS], axis=0)
    ones = jnp.ones((2 * w, 128), BF16)
    bias = bias_ref[...]
    kcol = lax.broadcasted_iota(jnp.int32, (w, 2 * w), 1)
    bias_first = jnp.where((kcol >= w) | (i > 0), bias, NEG)

    for r in range(SWA_TILE // w):
        q = q_ref[r * w:(r + 1) * w, :]
        q_aug = jnp.concatenate(
            [jnp.concatenate([q[:, (slot // 2) * 128:(slot // 2 + 1) * 128],
                              qconst_ref[rb * w:(rb + 1) * w, :]], axis=1)
             for rb, slot in enumerate(SWA_ROW_SLOTS)], axis=0)
        k = k_all[r * w:(r + 2) * w, :]
        zero = jnp.zeros_like(k)
        nt = (((1,), (1,)), ((), ()))
        s0 = lax.dot_general(q_aug[:rows // 2], jnp.concatenate([jnp.where(lane2 < 64, k, zero), k_extra], axis=1),
                             nt, preferred_element_type=F32)
        s1 = lax.dot_general(q_aug[rows // 2:], jnp.concatenate([jnp.where(lane2 >= 64, k, zero), k_extra], axis=1),
                             nt, preferred_element_type=F32)
        b = bias_first if r == 0 else bias
        s = (jnp.concatenate([s0, s1], axis=0).reshape(N_HEADS, w, 2 * w) + b[None]).reshape(rows, 2 * w)
        row_max = jnp.max(jnp.maximum(s[:, :128], s[:, 128:]), axis=-1, keepdims=True)
        m = jnp.maximum(jnp.broadcast_to(row_max, (rows, 128)), sink)
        p = jnp.exp2(s - jnp.concatenate([m, m], axis=1)).astype(BF16)
        v_aug = jnp.concatenate([v_all[r * w:(r + 2) * w, :], ones], axis=1)
        pv = jnp.dot(p, v_aug, preferred_element_type=F32)
        out = pv[:, :128] / (pv[:, 128:] + jnp.exp2(sink - m))
        tiles = []
        for pair in range(4):
            lo = out[pair * w:(pair + 1) * w, :]
            hi = out[(4 + pair) * w:(5 + pair) * w, :]
            tiles.append(jnp.where(lane < 64, lo, hi))
        o_ref[r * w:(r + 1) * w, :] = jnp.concatenate(tiles, axis=1).astype(BF16)


def _swa(sinks, qa, ka, va):
    w = SWA_WINDOW
    per_tile = SWA_TILE // w
    cur = lambda b, i: (b, i, 0)
    prev = lambda b, i: (b, jnp.maximum(i * per_tile - 1, 0), 0)
    return pl.pallas_call(
        _swa_kernel,
        grid=(BATCH, SEQ // SWA_TILE),
        in_specs=[pl.BlockSpec(memory_space=pltpu.SMEM),
                  pl.BlockSpec((None, SWA_TILE, Q_COLS), cur),
                  pl.BlockSpec((None, w, KV_COLS), prev),
                  pl.BlockSpec((None, SWA_TILE, KV_COLS), cur),
                  pl.BlockSpec((None, w, KV_COLS), prev),
                  pl.BlockSpec((None, SWA_TILE, KV_COLS), cur)],
        out_specs=pl.BlockSpec((None, SWA_TILE, Q_COLS), cur),
        out_shape=jax.ShapeDtypeStruct((BATCH, SEQ, Q_COLS), BF16),
        scratch_shapes=[pltpu.VMEM((N_HEADS * w, 128), BF16),
                        pltpu.VMEM((w, 2 * w), F32)],
        compiler_params=pltpu.CompilerParams(dimension_semantics=("arbitrary", "arbitrary")),
        name="swa",
    )(sinks, qa, ka, ka, va, va)


def _moba_kernel(q_ref, k_ref, v_ref, o_ref,
                 selb_ref, qconst_ref, qaug_ref, u_ref, mpart_ref, acc_ref):
    t = pl.program_id(1)
    bl = MOBA_BLOCK
    nbk = N_MOBA_BLOCKS
    rows = N_HEADS * bl
    lane = lax.broadcasted_iota(jnp.int32, (bl, 128), 1)
    lane_blk = lane % nbk
    lane_slot = lane // nbk
    sel_lane = lane < SEL_LANES
    in3 = lambda first: (lane >= first) & (lane < first + 3)

    def lanes3(first, terms):
        out = jnp.zeros((bl, 128), F32)
        for i, term in enumerate(terms):
            out = jnp.where(lane == first + i, term.astype(F32), out)
        return out

    @pl.when((pl.program_id(0) == 0) & (t == 0))
    def _():
        rowf = lax.broadcasted_iota(jnp.int32, (bl, 128), 0).astype(F32)
        for slot in range(N_HEADS):
            slope = jnp.full((bl, 128), ALIBI_SLOPES[N_HEADS + HEAD_OF_SLOT[slot]], F32)
            qc = (lanes3(ROW_LANE, _split3(-slope * rowf)) + lanes3(COL_LANE, _split3(slope))
                  + lanes3(BLK_LANE, _split3(-slope)))
            qconst_ref[slot * bl:(slot + 1) * bl, :] = qc.astype(BF16)

    @pl.when(t == 0)
    def _():
        rsel = lax.broadcasted_iota(jnp.int32, (128, SEQ), 0)
        tsel = lax.broadcasted_iota(jnp.int32, (128, SEQ), 1)
        avg = jnp.where((rsel % nbk) == (tsel // bl), 1.0 / bl, 0.0).astype(BF16)
        kmean_rows = jnp.dot(avg, k_ref[...], preferred_element_type=F32)
        kmean_t = kmean_rows.T
        r128 = lax.broadcasted_iota(jnp.int32, (128, 128), 0)
        c128 = lax.broadcasted_iota(jnp.int32, (128, 128), 1)
        km = jnp.concatenate(
            [jnp.where((c128 // nbk) == 2 * pair + (r128 >= 64).astype(jnp.int32), kmean_t, 0.0)
             for pair in range(4)], axis=0)
        q_all = q_ref[...]
        gate = sum(jnp.dot(q_all, term, preferred_element_type=F32) for term in _split3(km))
        gate_t = gate.T[:SEL_LANES].reshape(N_HEADS, nbk, SEQ)
        blk = lax.broadcasted_iota(jnp.int32, (N_HEADS, nbk, SEQ), 1)
        own = lax.broadcasted_iota(jnp.int32, (N_HEADS, nbk, SEQ), 2) // bl
        past = blk < own
        gm = jnp.where(past, gate_t, -jnp.inf)
        rank = jnp.zeros((N_HEADS, nbk, SEQ), jnp.int32)
        for d in range(1, nbk):
            partner = pltpu.roll(gm, nbk - d, 1)
            wrapped = blk + d >= nbk
            beats = (partner > gm) | ((partner == gm) & wrapped)
            rank = rank + beats.astype(jnp.int32)
        selected = past & (rank < MOBA_TOPK)
        selb_t = jnp.where(selected, 0.0, NEG).reshape(SEL_LANES, SEQ)
        selb_ref[...] = jnp.concatenate([selb_t, jnp.zeros_like(selb_t)], axis=0).T

    q = q_ref[pl.ds(pl.multiple_of(t * bl, bl), bl), :]
    selb = selb_ref[pl.ds(pl.multiple_of(t * bl, bl), bl), :]

    for slot in range(N_HEADS):
        pair, half = divmod(slot, 2)
        qp = q[:, pair * 128:(pair + 1) * 128]
        keep = (lane < 64) if half == 0 else (lane >= 64)
        qm = jnp.where(keep, qp, jnp.zeros_like(qp))
        extra = jnp.where(sel_lane & (lane_slot == slot), selb,
                          qconst_ref[slot * bl:(slot + 1) * bl, :].astype(F32))
        qaug_ref[slot * bl:(slot + 1) * bl, :] = jnp.concatenate([qm, extra.astype(BF16)], axis=1)

    key_col = lax.broadcasted_iota(jnp.int32, (bl, 128), 0).astype(F32)

    def scores(j, diag):
        kj = k_ref[pl.ds(pl.multiple_of(j * bl, bl), bl), :]
        if diag:
            extra = jnp.where(in3(ROW_LANE), 1.0, jnp.where(in3(COL_LANE), key_col, 0.0))
        else:
            offset = ((t - j) * bl).astype(F32)
            extra = jnp.where(sel_lane, jnp.where(lane_blk == j, 1.0, 0.0),
                              jnp.where(in3(ROW_LANE), 1.0,
                                        jnp.where(in3(COL_LANE), key_col,
                                                  jnp.where(in3(BLK_LANE), offset, 0.0))))
        kaug = jnp.concatenate([kj, extra.astype(BF16)], axis=1)
        return lax.dot_general(qaug_ref[...], kaug, (((1,), (1,)), ((), ())),
                               preferred_element_type=F32)

    def fold(s):
        return jnp.maximum(s[:, :128], s[:, 128:])

    qrow = lax.broadcasted_iota(jnp.int32, (rows, bl), 0) & (bl - 1)
    kcol = lax.broadcasted_iota(jnp.int32, (rows, bl), 1)

    def pass1(j, own):
        sj = scores(j, own)
        if own:
            sj = jnp.where(qrow >= kcol, sj, NEG)
        u_ref[j] = sj
        return fold(sj)

    def pass2(j):
        sj = u_ref[j]
        m = mpart_ref[...]
        pb = jnp.concatenate([jnp.exp2(sj[:, :128] - m), jnp.exp2(sj[:, 128:] - m)], axis=1).astype(BF16)
        vj = v_ref[pl.ds(pl.multiple_of(j * bl, bl), bl), :]
        return jnp.dot(pb, jnp.concatenate([vj, jnp.ones_like(vj)], axis=1), preferred_element_type=F32)

    odd_tile = (t & 1) == 1
    first_pair = t & 1
    n_pairs = t // 2

    @pl.when(jnp.logical_not(odd_tile))
    def _():
        mpart_ref[...] = pass1(t, True)

    @pl.when(odd_tile)
    def _():
        mpart_ref[...] = jnp.maximum(pass1(t, True), pass1(0, False))

    def pass1_pair(n, carry):
        j = first_pair + 2 * n
        mpart_ref[...] = jnp.maximum(mpart_ref[...], jnp.maximum(pass1(j, False), pass1(j + 1, False)))
        return carry

    lax.fori_loop(0, n_pairs, pass1_pair, 0)

    mpart_ref[...] = jnp.broadcast_to(jnp.max(mpart_ref[...], axis=-1, keepdims=True), (rows, 128))

    @pl.when(jnp.logical_not(odd_tile))
    def _():
        acc_ref[...] = pass2(t)

    @pl.when(odd_tile)
    def _():
        acc_ref[...] = pass2(t) + pass2(0)

    def pass2_pair(n, carry):
        j = first_pair + 2 * n
        acc_ref[...] += pass2(j) + pass2(j + 1)
        return carry

    lax.fori_loop(0, n_pairs, pass2_pair, 0)

    out = acc_ref[:, :128] / acc_ref[:, 128:]
    tiles = []
    for pair in range(4):
        lo = out[(2 * pair) * bl:(2 * pair + 1) * bl, :]
        hi = out[(2 * pair + 1) * bl:(2 * pair + 2) * bl, :]
        tiles.append(jnp.where(lane < 64, lo, hi))
    o_ref[...] = jnp.concatenate(tiles, axis=1).astype(BF16)


def _moba(qb, kb, vb):
    bl = MOBA_BLOCK
    rows = N_HEADS * bl
    tile = lambda b, t: (b, t, 0)
    whole = lambda b, t: (b, 0, 0)
    return pl.pallas_call(
        _moba_kernel,
        grid=(BATCH, N_MOBA_BLOCKS),
        in_specs=[pl.BlockSpec((None, SEQ, Q_COLS), whole),
                  pl.BlockSpec((None, SEQ, KV_COLS), whole),
                  pl.BlockSpec((None, SEQ, KV_COLS), whole)],
        out_specs=pl.BlockSpec((None, bl, Q_COLS), tile),
        out_shape=jax.ShapeDtypeStruct((BATCH, SEQ, Q_COLS), BF16),
        scratch_shapes=[pltpu.VMEM((SEQ, 128), F32),
                        pltpu.VMEM((rows, 128), BF16),
                        pltpu.VMEM((rows, 256), BF16),
                        pltpu.VMEM((N_MOBA_BLOCKS, rows, bl), F32),
                        pltpu.VMEM((rows, 128), F32),
                        pltpu.VMEM((rows, 256), F32)],
        compiler_params=pltpu.CompilerParams(
            dimension_semantics=("arbitrary", "arbitrary"), vmem_limit_bytes=V7X_VMEM_LIMIT_BYTES),
        name="moba",
    )(qb, kb, vb)


def _mix_ffn_kernel(x1_ref, ya_ref, yb_ref, wg_ref, wa_ref, wb_ref, wo_ref, g2_ref, b2_ref,
                    w_in_ref, w_out_ref, g3_ref, b3_ref, o_ref, act_ref):
    x1 = x1_ref[...]
    x1b = x1.astype(BF16)
    ga = jnp.dot(x1b, wg_ref[:, :D_MODEL], preferred_element_type=F32)
    ya = jnp.dot(ya_ref[...], wa_ref[...], preferred_element_type=F32)
    y = jax.nn.sigmoid(ga) * ya
    gb = jnp.dot(x1b, wg_ref[:, D_MODEL:], preferred_element_type=F32)
    yb = jnp.dot(yb_ref[...], wb_ref[...], preferred_element_type=F32)
    y = y + jax.nn.sigmoid(gb) * yb
    z = jnp.dot(y.astype(BF16), wo_ref[...], preferred_element_type=F32)
    x2 = _layer_norm(ALPHA * x1 + z, g2_ref[...], b2_ref[...])
    f = _swiglu(x2.astype(BF16), w_in_ref, w_out_ref, act_ref)
    o_ref[...] = _layer_norm(ALPHA * x2 + 0.5 * f, g3_ref[...], b3_ref[...])


def _mix_ffn(x1, ya, yb, wg, wa, wb, wo, g2, b2, w_in, w_out, g3, b3):
    tm = TOKEN_TILE
    row = lambda n: pl.BlockSpec((tm, n), lambda i: (i, 0))
    consts = (wg, wa, wb, wo, g2, b2, w_in, w_out, g3, b3)
    return pl.pallas_call(
        _mix_ffn_kernel,
        grid=(TOKENS // tm,),
        in_specs=[row(D_MODEL), row(Q_COLS), row(Q_COLS)] + [_const_spec(c.shape) for c in consts],
        out_specs=row(D_MODEL),
        out_shape=jax.ShapeDtypeStruct((TOKENS, D_MODEL), F32),
        scratch_shapes=[pltpu.VMEM((tm, D_FF), BF16)],
        compiler_params=pltpu.CompilerParams(
            dimension_semantics=("arbitrary",), vmem_limit_bytes=V7X_VMEM_LIMIT_BYTES),
        name="mix_ffn",
    )(x1, ya, yb, *consts)


_SLOT_COLS = np.concatenate([np.arange(HEAD_DIM * h, HEAD_DIM * (h + 1)) for h in HEAD_OF_SLOT])


def kernel(x, ffn1_w_in, ffn1_w_out, ln1_g, ln1_b, mix_w_in, swa_sinks, w_branch_a, w_branch_b,
           mix_w_o, ln2_g, ln2_b, ffn2_w_in, ffn2_w_out, ln3_g, ln3_b):
    assert x.shape == (BATCH, SEQ, D_MODEL) and ffn1_w_in.shape[0] == 1
    wm = mix_w_in[0]
    wqkv = jnp.concatenate([wm[:, 0:512][:, _SLOT_COLS], wm[:, 512:768],
                            wm[:, 768:1280][:, _SLOT_COLS], wm[:, 1280:1536]], axis=1).astype(BF16)
    wg = wm[:, 1536:].astype(BF16)
    wa = w_branch_a[0][_SLOT_COLS, :].astype(BF16)
    wb = w_branch_b[0][_SLOT_COLS, :].astype(BF16)
    wo = mix_w_o[0].astype(BF16)

    x1, qa, ka, va, qb, kb, vb = _ffn_qkv(x.reshape(TOKENS, D_MODEL), ffn1_w_in[0].astype(BF16),
                                          ffn1_w_out[0].astype(BF16), ln1_g, ln1_b, wqkv)
    seq3 = lambda a: a.reshape(BATCH, SEQ, a.shape[-1])
    ya = _swa(swa_sinks[0], seq3(qa), seq3(ka), seq3(va))
    yb = _moba(seq3(qb), seq3(kb), seq3(vb))
    out = _mix_ffn(x1, ya.reshape(TOKENS, Q_COLS), yb.reshape(TOKENS, Q_COLS), wg, wa, wb, wo,
                   ln2_g, ln2_b, ffn2_w_in[0].astype(BF16), ffn2_w_out[0].astype(BF16), ln3_g, ln3_b)
    return out.reshape(BATCH, SEQ, D_MODEL)
```

```python
import jax
import jax.numpy as jnp
from jax import lax
from jax.experimental import pallas as pl
from jax.experimental.pallas import tpu as pltpu

F32 = jnp.float32
BF16 = jnp.bfloat16

D_MODEL = 1024
BATCH = 8
SEQ = 2048
TOKENS = BATCH * SEQ
HEAD_DIM = 64
N_HEADS = 8
KV_COLS = 128
Q_COLS = 512
QKV_COLS = 2 * (Q_COLS + 2 * KV_COLS)
MIX_IN_COLS = QKV_COLS + 2 * D_MODEL
SWA_WINDOW = 128
SWA_TILE = 512
MOBA_BLOCK = 256
MOBA_TOPK = 3
N_MOBA_BLOCKS = SEQ // MOBA_BLOCK
D_FF = 2816
FFN_CHUNK = 256
N_FFN_CHUNKS = D_FF // FFN_CHUNK
ALPHA = 2.0 ** 0.25
LN_EPS = 1e-5
NEG = -1e30
LOG2E = 1.4426950408889634
Q_SCALE = 0.125 * LOG2E
ALIBI_SLOPES = tuple(float(2.0 ** (-8.0 * i / 16.0)) * LOG2E for i in range(1, 17))

HEAD_OF_SLOT = (0, 4, 1, 5, 2, 6, 3, 7)

V7X_VMEM_LIMIT_BYTES = 56 * 1024 * 1024
TOKEN_TILE = 512

SEL_LANES = 64
ROW_LANE = 64
COL_LANE = 67
BLK_LANE = 70


def _const_spec(shape):
    zeros = (0,) * len(shape)
    return pl.BlockSpec(shape, lambda *_: zeros, pipeline_mode=pl.Buffered(1))


def _split3(x):
    hi = x.astype(BF16)
    r1 = x - hi.astype(F32)
    mid = r1.astype(BF16)
    lo = (r1 - mid.astype(F32)).astype(BF16)
    return hi, mid, lo


def _layer_norm(y, g, b):
    mu = jnp.mean(y, axis=-1, keepdims=True)
    yc = y - mu
    var = jnp.mean(yc * yc, axis=-1, keepdims=True)
    return yc * lax.rsqrt(var + LN_EPS) * g + b


def _swiglu(xb, w_in_ref, w_out_ref, g_ref):
    for c in range(N_FFN_CHUNKS):
        lo = c * FFN_CHUNK
        a = jnp.dot(xb, w_in_ref[:, lo:lo + FFN_CHUNK], preferred_element_type=F32)
        u = jnp.dot(xb, w_in_ref[:, D_FF + lo:D_FF + lo + FFN_CHUNK], preferred_element_type=F32)
        g_ref[:, lo:lo + FFN_CHUNK] = (a * jax.nn.sigmoid(a) * u).astype(BF16)
    return jnp.dot(g_ref[...], w_out_ref[...], preferred_element_type=F32)


def _stream_cast(n_chunks, src_view, store, stage_ref, sem_ref):
    def copy(c):
        return pltpu.make_async_copy(src_view(c), stage_ref.at[c % 2], sem_ref.at[c % 2])

    copy(0).start()
    for c in range(n_chunks):
        if c + 1 < n_chunks:
            copy(c + 1).start()
        copy(c).wait()
        store(c, stage_ref[c % 2])


def _cast_rows(dst_ref, rows_per_chunk):
    def store(c, tile):
        dst_ref[c * rows_per_chunk:(c + 1) * rows_per_chunk, :] = tile.astype(BF16)
    return store


def _to_slot_order(q):
    lane = lax.broadcasted_iota(jnp.int32, (q.shape[0], 128), 1)
    t = [q[:, i * 128:(i + 1) * 128] for i in range(4)]
    swap = lambda a: pltpu.roll(a, 64, 1)
    return jnp.concatenate([jnp.where(lane < 64, t[0], swap(t[2])), jnp.where(lane < 64, swap(t[0]), t[2]),
                            jnp.where(lane < 64, t[1], swap(t[3])), jnp.where(lane < 64, swap(t[1]), t[3])],
                           axis=1)


W_IN_CHUNK = 64
W_ROW_CHUNK = 128


def _ffn_qkv_kernel(x_ref, w_in_hbm, w_out_hbm, g_ref, b_ref, w_mix_hbm,
                    x1_ref, qa_ref, ka_ref, va_ref, qb_ref, kb_ref, vb_ref,
                    act_ref, w_in_ref, w_out_ref, wqkv_ref, st_in, st_out, st_qkv, sem_in, sem_out, sem_qkv):
    @pl.when(pl.program_id(0) == 0)
    def _():
        _stream_cast(D_MODEL // W_IN_CHUNK, lambda c: w_in_hbm.at[0, pl.ds(c * W_IN_CHUNK, W_IN_CHUNK), :],
                     _cast_rows(w_in_ref, W_IN_CHUNK), st_in, sem_in)
        _stream_cast(D_FF // W_ROW_CHUNK, lambda c: w_out_hbm.at[0, pl.ds(c * W_ROW_CHUNK, W_ROW_CHUNK), :],
                     _cast_rows(w_out_ref, W_ROW_CHUNK), st_out, sem_out)

        def store_qkv(c, tile):
            out = jnp.concatenate([_to_slot_order(tile[:, 0:512]), tile[:, 512:768],
                                   _to_slot_order(tile[:, 768:1280]), tile[:, 1280:1536]], axis=1)
            wqkv_ref[c * W_ROW_CHUNK:(c + 1) * W_ROW_CHUNK, :] = out.astype(BF16)

        _stream_cast(D_MODEL // W_ROW_CHUNK,
                     lambda c: w_mix_hbm.at[0, pl.ds(c * W_ROW_CHUNK, W_ROW_CHUNK), pl.ds(0, QKV_COLS)],
                     store_qkv, st_qkv, sem_qkv)

    x = x_ref[...]
    f = _swiglu(x.astype(BF16), w_in_ref, w_out_ref, act_ref)
    x1 = _layer_norm(ALPHA * x + 0.5 * f, g_ref[...], b_ref[...])
    x1_ref[...] = x1
    h = jnp.dot(x1.astype(BF16), wqkv_ref[...], preferred_element_type=F32)
    qa_ref[...] = (h[:, 0:512] * Q_SCALE).astype(BF16)
    ka_ref[...] = h[:, 512:640].astype(BF16)
    va_ref[...] = h[:, 640:768].astype(BF16)
    qb_ref[...] = (h[:, 768:1280] * Q_SCALE).astype(BF16)
    kb_ref[...] = h[:, 1280:1408].astype(BF16)
    vb_ref[...] = h[:, 1408:1536].astype(BF16)


def _ffn_qkv(x, w_in, w_out, g, b, w_mix):
    tm = TOKEN_TILE
    row = lambda n: pl.BlockSpec((tm, n), lambda i: (i, 0))
    hbm = pl.BlockSpec(memory_space=pl.ANY)
    out_cols = (D_MODEL, Q_COLS, KV_COLS, KV_COLS, Q_COLS, KV_COLS, KV_COLS)
    out_dtypes = (F32,) + (BF16,) * 6
    return pl.pallas_call(
        _ffn_qkv_kernel,
        grid=(TOKENS // tm,),
        in_specs=[row(D_MODEL), hbm, hbm, _const_spec(g.shape), _const_spec(b.shape), hbm],
        out_specs=[row(n) for n in out_cols],
        out_shape=[jax.ShapeDtypeStruct((TOKENS, n), dt) for n, dt in zip(out_cols, out_dtypes)],
        scratch_shapes=[pltpu.VMEM((tm, D_FF), BF16),
                        pltpu.VMEM((D_MODEL, 2 * D_FF), BF16),
                        pltpu.VMEM((D_FF, D_MODEL), BF16),
                        pltpu.VMEM((D_MODEL, QKV_COLS), BF16),
                        pltpu.VMEM((2, W_IN_CHUNK, 2 * D_FF), F32),
                        pltpu.VMEM((2, W_ROW_CHUNK, D_MODEL), F32),
                        pltpu.VMEM((2, W_ROW_CHUNK, QKV_COLS), F32),
                        pltpu.SemaphoreType.DMA((2,)), pltpu.SemaphoreType.DMA((2,)),
                        pltpu.SemaphoreType.DMA((2,))],
        compiler_params=pltpu.CompilerParams(
            dimension_semantics=("arbitrary",), vmem_limit_bytes=V7X_VMEM_LIMIT_BYTES),
        name="ffn_qkv",
    )(x, w_in, w_out, g, b, w_mix)


SWA_ROW_SLOTS = (0, 2, 4, 6, 1, 3, 5, 7)


def _swa_kernel(sink_ref, q_ref, kp_ref, kc_ref, vp_ref, vc_ref, o_ref, qconst_ref, bias_ref):
    i = pl.program_id(1)
    w = SWA_WINDOW
    rows = N_HEADS * w
    lane = lax.broadcasted_iota(jnp.int32, (w, 128), 1)

    @pl.when((pl.program_id(0) == 0) & (i == 0))
    def _():
        rowf = lax.broadcasted_iota(jnp.int32, (w, 128), 0).astype(F32)
        for rb, slot in enumerate(SWA_ROW_SLOTS):
            slope = jnp.full((w, 128), ALIBI_SLOPES[HEAD_OF_SLOT[slot]], F32)
            qc = jnp.zeros((w, 128), F32)
            for first, terms in ((ROW_LANE, _split3(-slope * (rowf + w))), (COL_LANE, _split3(slope))):
                for n, term in enumerate(terms):
                    qc = jnp.where(lane == first + n, term.astype(F32), qc)
            qconst_ref[rb * w:(rb + 1) * w, :] = qc.astype(BF16)
        dist = (lax.broadcasted_iota(jnp.int32, (w, 2 * w), 0) + w
                - lax.broadcasted_iota(jnp.int32, (w, 2 * w), 1))
        bias_ref[...] = jnp.where((dist >= 0) & (dist < w), 0.0, NEG)

    k_all = jnp.concatenate([kp_ref[...], kc_ref[...]], axis=0)
    v_all = jnp.concatenate([vp_ref[...], vc_ref[...]], axis=0)
    key_col = lax.broadcasted_iota(jnp.int32, (2 * w, 128), 0).astype(F32)
    lane2 = lax.broadcasted_iota(jnp.int32, (2 * w, 128), 1)
    k_extra = jnp.where((lane2 >= ROW_LANE) & (lane2 < ROW_LANE + 3), 1.0,
                        jnp.where((lane2 >= COL_LANE) & (lane2 < COL_LANE + 3), key_col, 0.0)).astype(BF16)
    sink = jnp.concatenate([jnp.full((w, 128), sink_ref[HEAD_OF_SLOT[slot]] * LOG2E, F32)
                            for slot in SWA_ROW_SLOTS], axis=0)
    ones = jnp.ones((2 * w, 128), BF16)
    bias = bias_ref[...]
    kcol = lax.broadcasted_iota(jnp.int32, (w, 2 * w), 1)
    bias_first = jnp.where((kcol >= w) | (i > 0), bias, NEG)

    for r in range(SWA_TILE // w):
        q = q_ref[r * w:(r + 1) * w, :]
        q_aug = jnp.concatenate(
            [jnp.concatenate([q[:, (slot // 2) * 128:(slot // 2 + 1) * 128],
                              qconst_ref[rb * w:(rb + 1) * w, :]], axis=1)
             for rb, slot in enumerate(SWA_ROW_SLOTS)], axis=0)
        k = k_all[r * w:(r + 2) * w, :]
        zero = jnp.zeros_like(k)
        nt = (((1,), (1,)), ((), ()))
        s0 = lax.dot_general(q_aug[:rows // 2], jnp.concatenate([jnp.where(lane2 < 64, k, zero), k_extra], axis=1),
                             nt, preferred_element_type=F32)
        s1 = lax.dot_general(q_aug[rows // 2:], jnp.concatenate([jnp.where(lane2 >= 64, k, zero), k_extra], axis=1),
                             nt, preferred_element_type=F32)
        b = bias_first if r == 0 else bias
        s = (jnp.concatenate([s0, s1], axis=0).reshape(N_HEADS, w, 2 * w) + b[None]).reshape(rows, 2 * w)
        row_max = jnp.max(jnp.maximum(s[:, :128], s[:, 128:]), axis=-1, keepdims=True)
        m = jnp.maximum(jnp.broadcast_to(row_max, (rows, 128)), sink)
        p = jnp.exp2(s - jnp.concatenate([m, m], axis=1)).astype(BF16)
        v_aug = jnp.concatenate([v_all[r * w:(r + 2) * w, :], ones], axis=1)
        pv = jnp.dot(p, v_aug, preferred_element_type=F32)
        out = pv[:, :128] / (pv[:, 128:] + jnp.exp2(sink - m))
        tiles = []
        for pair in range(4):
            lo = out[pair * w:(pair + 1) * w, :]
            hi = out[(4 + pair) * w:(5 + pair) * w, :]
            tiles.append(jnp.where(lane < 64, lo, hi))
        o_ref[r * w:(r + 1) * w, :] = jnp.concatenate(tiles, axis=1).astype(BF16)


def _swa(sinks, qa, ka, va):
    w = SWA_WINDOW
    per_tile = SWA_TILE // w
    cur = lambda b, i: (b, i, 0)
    prev = lambda b, i: (b, jnp.maximum(i * per_tile - 1, 0), 0)
    return pl.pallas_call(
        _swa_kernel,
        grid=(BATCH, SEQ // SWA_TILE),
        in_specs=[pl.BlockSpec(memory_space=pltpu.SMEM),
                  pl.BlockSpec((None, SWA_TILE, Q_COLS), cur),
                  pl.BlockSpec((None, w, KV_COLS), prev),
                  pl.BlockSpec((None, SWA_TILE, KV_COLS), cur),
                  pl.BlockSpec((None, w, KV_COLS), prev),
                  pl.BlockSpec((None, SWA_TILE, KV_COLS), cur)],
        out_specs=pl.BlockSpec((None, SWA_TILE, Q_COLS), cur),
        out_shape=jax.ShapeDtypeStruct((BATCH, SEQ, Q_COLS), BF16),
        scratch_shapes=[pltpu.VMEM((N_HEADS * w, 128), BF16),
                        pltpu.VMEM((w, 2 * w), F32)],
        compiler_params=pltpu.CompilerParams(dimension_semantics=("arbitrary", "arbitrary")),
        name="swa",
    )(sinks, qa, ka, ka, va, va)


def _moba_kernel(q_ref, k_ref, v_ref, o_ref,
                 selb_ref, qconst_ref, qaug_ref, u_ref, mpart_ref, acc_ref):
    t = pl.program_id(1)
    bl = MOBA_BLOCK
    nbk = N_MOBA_BLOCKS
    rows = N_HEADS * bl
    lane = lax.broadcasted_iota(jnp.int32, (bl, 128), 1)
    lane_blk = lane % nbk
    lane_slot = lane // nbk
    sel_lane = lane < SEL_LANES
    in3 = lambda first: (lane >= first) & (lane < first + 3)

    def lanes3(first, terms):
        out = jnp.zeros((bl, 128), F32)
        for i, term in enumerate(terms):
            out = jnp.where(lane == first + i, term.astype(F32), out)
        return out

    @pl.when((pl.program_id(0) == 0) & (t == 0))
    def _():
        rowf = lax.broadcasted_iota(jnp.int32, (bl, 128), 0).astype(F32)
        for slot in range(N_HEADS):
            slope = jnp.full((bl, 128), ALIBI_SLOPES[N_HEADS + HEAD_OF_SLOT[slot]], F32)
            qc = (lanes3(ROW_LANE, _split3(-slope * rowf)) + lanes3(COL_LANE, _split3(slope))
                  + lanes3(BLK_LANE, _split3(-slope)))
            qconst_ref[slot * bl:(slot + 1) * bl, :] = qc.astype(BF16)

    @pl.when(t == 0)
    def _():
        rsel = lax.broadcasted_iota(jnp.int32, (128, SEQ), 0)
        tsel = lax.broadcasted_iota(jnp.int32, (128, SEQ), 1)
        avg = jnp.where((rsel % nbk) == (tsel // bl), 1.0 / bl, 0.0).astype(BF16)
        kmean_rows = jnp.dot(avg, k_ref[...], preferred_element_type=F32)
        kmean_t = kmean_rows.T
        r128 = lax.broadcasted_iota(jnp.int32, (128, 128), 0)
        c128 = lax.broadcasted_iota(jnp.int32, (128, 128), 1)
        km = jnp.concatenate(
            [jnp.where((c128 // nbk) == 2 * pair + (r128 >= 64).astype(jnp.int32), kmean_t, 0.0)
             for pair in range(4)], axis=0)
        q_all = q_ref[...]
        gate = sum(jnp.dot(q_all, term, preferred_element_type=F32) for term in _split3(km)[:2])
        gate_t = gate.T[:SEL_LANES].reshape(N_HEADS, nbk, SEQ)
        blk = lax.broadcasted_iota(jnp.int32, (N_HEADS, nbk, SEQ), 1)
        own = lax.broadcasted_iota(jnp.int32, (N_HEADS, nbk, SEQ), 2) // bl
        past = blk < own
        gm = jnp.where(past, gate_t, -jnp.inf)
        rank = jnp.zeros((N_HEADS, nbk, SEQ), jnp.int32)
        for d in range(1, nbk):
            partner = pltpu.roll(gm, nbk - d, 1)
            wrapped = blk + d >= nbk
            beats = (partner > gm) | ((partner == gm) & wrapped)
            rank = rank + beats.astype(jnp.int32)
        selected = past & (rank < MOBA_TOPK)
        selb_t = jnp.where(selected, 0.0, NEG).reshape(SEL_LANES, SEQ)
        selb_ref[...] = jnp.concatenate([selb_t, jnp.zeros_like(selb_t)], axis=0).T

    q = q_ref[pl.ds(pl.multiple_of(t * bl, bl), bl), :]
    selb = selb_ref[pl.ds(pl.multiple_of(t * bl, bl), bl), :]

    for slot in range(N_HEADS):
        pair, half = divmod(slot, 2)
        qp = q[:, pair * 128:(pair + 1) * 128]
        keep = (lane < 64) if half == 0 else (lane >= 64)
        qm = jnp.where(keep, qp, jnp.zeros_like(qp))
        extra = jnp.where(sel_lane & (lane_slot == slot), selb,
                          qconst_ref[slot * bl:(slot + 1) * bl, :].astype(F32))
        qaug_ref[slot * bl:(slot + 1) * bl, :] = jnp.concatenate([qm, extra.astype(BF16)], axis=1)

    key_col = lax.broadcasted_iota(jnp.int32, (bl, 128), 0).astype(F32)

    def scores(j, diag):
        kj = k_ref[pl.ds(pl.multiple_of(j * bl, bl), bl), :]
        if diag:
            extra = jnp.where(in3(ROW_LANE), 1.0, jnp.where(in3(COL_LANE), key_col, 0.0))
        else:
            offset = ((t - j) * bl).astype(F32)
            extra = jnp.where(sel_lane, jnp.where(lane_blk == j, 1.0, 0.0),
                              jnp.where(in3(ROW_LANE), 1.0,
                                        jnp.where(in3(COL_LANE), key_col,
                                                  jnp.where(in3(BLK_LANE), offset, 0.0))))
        kaug = jnp.concatenate([kj, extra.astype(BF16)], axis=1)
        return lax.dot_general(qaug_ref[...], kaug, (((1,), (1,)), ((), ())),
                               preferred_element_type=F32)

    def fold(s):
        return jnp.maximum(s[:, :128], s[:, 128:])

    qrow = lax.broadcasted_iota(jnp.int32, (rows, bl), 0) & (bl - 1)
    kcol = lax.broadcasted_iota(jnp.int32, (rows, bl), 1)

    def pass1(j, own):
        sj = scores(j, own)
        if own:
            sj = jnp.where(qrow >= kcol, sj, NEG)
        u_ref[j] = sj
        return fold(sj)

    def pass2(j):
        sj = u_ref[j]
        m = mpart_ref[...]
        pb = jnp.concatenate([jnp.exp2(sj[:, :128] - m), jnp.exp2(sj[:, 128:] - m)], axis=1).astype(BF16)
        vj = v_ref[pl.ds(pl.multiple_of(j * bl, bl), bl), :]
        return jnp.dot(pb, jnp.concatenate([vj, jnp.ones_like(vj)], axis=1), preferred_element_type=F32)

    odd_tile = (t & 1) == 1
    first_pair = t & 1
    n_pairs = t // 2

    @pl.when(jnp.logical_not(odd_tile))
    def _():
        mpart_ref[...] = pass1(t, True)

    @pl.when(odd_tile)
    def _():
        mpart_ref[...] = jnp.maximum(pass1(t, True), pass1(0, False))

    def pass1_pair(n, carry):
        j = first_pair + 2 * n
        mpart_ref[...] = jnp.maximum(mpart_ref[...], jnp.maximum(pass1(j, False), pass1(j + 1, False)))
        return carry

    lax.fori_loop(0, n_pairs, pass1_pair, 0)

    mpart_ref[...] = jnp.broadcast_to(jnp.max(mpart_ref[...], axis=-1, keepdims=True), (rows, 128))

    @pl.when(jnp.logical_not(odd_tile))
    def _():
        acc_ref[...] = pass2(t)

    @pl.when(odd_tile)
    def _():
        acc_ref[...] = pass2(t) + pass2(0)

    def pass2_pair(n, carry):
        j = first_pair + 2 * n
        acc_ref[...] += pass2(j) + pass2(j + 1)
        return carry

    lax.fori_loop(0, n_pairs, pass2_pair, 0)

    out = acc_ref[:, :128] / acc_ref[:, 128:]
    tiles = []
    for pair in range(4):
        lo = out[(2 * pair) * bl:(2 * pair + 1) * bl, :]
        hi = out[(2 * pair + 1) * bl:(2 * pair + 2) * bl, :]
        tiles.append(jnp.where(lane < 64, lo, hi))
    o_ref[...] = jnp.concatenate(tiles, axis=1).astype(BF16)


def _moba(qb, kb, vb):
    bl = MOBA_BLOCK
    rows = N_HEADS * bl
    tile = lambda b, t: (b, t, 0)
    whole = lambda b, t: (b, 0, 0)
    return pl.pallas_call(
        _moba_kernel,
        grid=(BATCH, N_MOBA_BLOCKS),
        in_specs=[pl.BlockSpec((None, SEQ, Q_COLS), whole),
                  pl.BlockSpec((None, SEQ, KV_COLS), whole),
                  pl.BlockSpec((None, SEQ, KV_COLS), whole)],
        out_specs=pl.BlockSpec((None, bl, Q_COLS), tile),
        out_shape=jax.ShapeDtypeStruct((BATCH, SEQ, Q_COLS), BF16),
        scratch_shapes=[pltpu.VMEM((SEQ, 128), F32),
                        pltpu.VMEM((rows, 128), BF16),
                        pltpu.VMEM((rows, 256), BF16),
                        pltpu.VMEM((N_MOBA_BLOCKS, rows, bl), F32),
                        pltpu.VMEM((rows, 128), F32),
                        pltpu.VMEM((rows, 256), F32)],
        compiler_params=pltpu.CompilerParams(
            dimension_semantics=("arbitrary", "arbitrary"), vmem_limit_bytes=V7X_VMEM_LIMIT_BYTES),
        name="moba",
    )(qb, kb, vb)


def _mix_ffn_kernel(x1_ref, ya_ref, yb_ref, w_mix_hbm, wa_hbm, wb_hbm, wo_hbm, g2_ref, b2_ref,
                    w_in_hbm, w_out_hbm, g3_ref, b3_ref, o_ref,
                    act_ref, wg_ref, wa_ref, wb_ref, wo_ref, w_in_ref, w_out_ref,
                    st_in, st_row, st_gate, sem_in, sem_row, sem_gate):
    @pl.when(pl.program_id(0) == 0)
    def _():
        rc = W_ROW_CHUNK
        _stream_cast(D_MODEL // rc, lambda c: w_mix_hbm.at[0, pl.ds(c * rc, rc), pl.ds(QKV_COLS, 2 * D_MODEL)],
                     _cast_rows(wg_ref, rc), st_gate, sem_gate)
        st_head = st_row.at[:, pl.ds(0, HEAD_DIM), :]
        for src_hbm, dst_ref in ((wa_hbm, wa_ref), (wb_hbm, wb_ref)):
            _stream_cast(N_HEADS,
                         lambda c, src_hbm=src_hbm: src_hbm.at[0, pl.ds(HEAD_OF_SLOT[c] * HEAD_DIM, HEAD_DIM), :],
                         _cast_rows(dst_ref, HEAD_DIM), st_head, sem_row)
        _stream_cast(D_MODEL // rc, lambda c: wo_hbm.at[0, pl.ds(c * rc, rc), :],
                     _cast_rows(wo_ref, rc), st_row, sem_row)
        _stream_cast(D_MODEL // W_IN_CHUNK, lambda c: w_in_hbm.at[0, pl.ds(c * W_IN_CHUNK, W_IN_CHUNK), :],
                     _cast_rows(w_in_ref, W_IN_CHUNK), st_in, sem_in)
        _stream_cast(D_FF // rc, lambda c: w_out_hbm.at[0, pl.ds(c * rc, rc), :],
                     _cast_rows(w_out_ref, rc), st_row, sem_row)

    x1 = x1_ref[...]
    x1b = x1.astype(BF16)
    ga = jnp.dot(x1b, wg_ref[:, :D_MODEL], preferred_element_type=F32)
    ya = jnp.dot(ya_ref[...], wa_ref[...], preferred_element_type=F32)
    y = jax.nn.sigmoid(ga) * ya
    gb = jnp.dot(x1b, wg_ref[:, D_MODEL:], preferred_element_type=F32)
    yb = jnp.dot(yb_ref[...], wb_ref[...], preferred_element_type=F32)
    y = y + jax.nn.sigmoid(gb) * yb
    z = jnp.dot(y.astype(BF16), wo_ref[...], preferred_element_type=F32)
    x2 = _layer_norm(ALPHA * x1 + z, g2_ref[...], b2_ref[...])
    f = _swiglu(x2.astype(BF16), w_in_ref, w_out_ref, act_ref)
    o_ref[...] = _layer_norm(ALPHA * x2 + 0.5 * f, g3_ref[...], b3_ref[...])


def _mix_ffn(x1, ya, yb, w_mix, wa, wb, wo, g2, b2, w_in, w_out, g3, b3):
    tm = TOKEN_TILE
    row = lambda n: pl.BlockSpec((tm, n), lambda i: (i, 0))
    hbm = pl.BlockSpec(memory_space=pl.ANY)
    ln = _const_spec(g2.shape)
    return pl.pallas_call(
        _mix_ffn_kernel,
        grid=(TOKENS // tm,),
        in_specs=[row(D_MODEL), row(Q_COLS), row(Q_COLS), hbm, hbm, hbm, hbm, ln, ln, hbm, hbm, ln, ln],
        out_specs=row(D_MODEL),
        out_shape=jax.ShapeDtypeStruct((TOKENS, D_MODEL), F32),
        scratch_shapes=[pltpu.VMEM((tm, D_FF), BF16),
                        pltpu.VMEM((D_MODEL, 2 * D_MODEL), BF16),
                        pltpu.VMEM((Q_COLS, D_MODEL), BF16),
                        pltpu.VMEM((Q_COLS, D_MODEL), BF16),
                        pltpu.VMEM((D_MODEL, D_MODEL), BF16),
                        pltpu.VMEM((D_MODEL, 2 * D_FF), BF16),
                        pltpu.VMEM((D_FF, D_MODEL), BF16),
                        pltpu.VMEM((2, W_IN_CHUNK, 2 * D_FF), F32),
                        pltpu.VMEM((2, W_ROW_CHUNK, D_MODEL), F32),
                        pltpu.VMEM((2, W_ROW_CHUNK, 2 * D_MODEL), F32),
                        pltpu.SemaphoreType.DMA((2,)), pltpu.SemaphoreType.DMA((2,)),
                        pltpu.SemaphoreType.DMA((2,))],
        compiler_params=pltpu.CompilerParams(
            dimension_semantics=("arbitrary",), vmem_limit_bytes=V7X_VMEM_LIMIT_BYTES),
        name="mix_ffn",
    )(x1, ya, yb, w_mix, wa, wb, wo, g2, b2, w_in, w_out, g3, b3)


def kernel(x, ffn1_w_in, ffn1_w_out, ln1_g, ln1_b, mix_w_in, swa_sinks, w_branch_a, w_branch_b,
           mix_w_o, ln2_g, ln2_b, ffn2_w_in, ffn2_w_out, ln3_g, ln3_b):
    assert x.shape == (BATCH, SEQ, D_MODEL) and ffn1_w_in.shape == (1, D_MODEL, 2 * D_FF)
    assert mix_w_in.shape == (1, D_MODEL, MIX_IN_COLS)
    x1, qa, ka, va, qb, kb, vb = _ffn_qkv(x.reshape(TOKENS, D_MODEL), ffn1_w_in, ffn1_w_out,
                                          ln1_g, ln1_b, mix_w_in)
    seq3 = lambda a: a.reshape(BATCH, SEQ, a.shape[-1])
    ya = _swa(swa_sinks[0], seq3(qa), seq3(ka), seq3(va))
    yb = _moba(seq3(qb), seq3(kb), seq3(vb))
    out = _mix_ffn(x1, ya.reshape(TOKENS, Q_COLS), yb.reshape(TOKENS, Q_COLS), mix_w_in,
                   w_branch_a, w_branch_b, mix_w_o, ln2_g, ln2_b, ffn2_w_in, ffn2_w_out, ln3_g, ln3_b)
    return out.reshape(BATCH, SEQ, D_MODEL)
```

```python
import jax
import jax.numpy as jnp
from jax import lax
from jax.experimental import pallas as pl
from jax.experimental.pallas import tpu as pltpu

F32 = jnp.float32
BF16 = jnp.bfloat16

D_MODEL = 1024
BATCH = 8
SEQ = 2048
TOKENS = BATCH * SEQ
HEAD_DIM = 64
N_HEADS = 8
KV_COLS = 128
Q_COLS = 512
QKV_COLS = 2 * (Q_COLS + 2 * KV_COLS)
MIX_IN_COLS = QKV_COLS + 2 * D_MODEL
SWA_WINDOW = 128
SWA_TILE = 512
MOBA_BLOCK = 256
MOBA_TOPK = 3
N_MOBA_BLOCKS = SEQ // MOBA_BLOCK
D_FF = 2816
FFN_CHUNK = 256
N_FFN_CHUNKS = D_FF // FFN_CHUNK
ALPHA = 2.0 ** 0.25
LN_EPS = 1e-5
NEG = -1e30
LOG2E = 1.4426950408889634
Q_SCALE = 0.125 * LOG2E
ALIBI_SLOPES = tuple(float(2.0 ** (-8.0 * i / 16.0)) * LOG2E for i in range(1, 17))

HEAD_OF_SLOT = (0, 4, 1, 5, 2, 6, 3, 7)

V7X_VMEM_LIMIT_BYTES = 56 * 1024 * 1024
TOKEN_TILE = 512

SEL_LANES = 64
ROW_LANE = 64
COL_LANE = 67
BLK_LANE = 70


def _const_spec(shape):
    zeros = (0,) * len(shape)
    return pl.BlockSpec(shape, lambda *_: zeros, pipeline_mode=pl.Buffered(1))


def _split3(x):
    hi = x.astype(BF16)
    r1 = x - hi.astype(F32)
    mid = r1.astype(BF16)
    lo = (r1 - mid.astype(F32)).astype(BF16)
    return hi, mid, lo


def _layer_norm(y, g, b):
    mu = jnp.mean(y, axis=-1, keepdims=True)
    yc = y - mu
    var = jnp.mean(yc * yc, axis=-1, keepdims=True)
    return yc * lax.rsqrt(var + LN_EPS) * g + b


def _swiglu(xb, w_in_ref, w_out_ref, g_ref):
    for c in range(N_FFN_CHUNKS):
        lo = c * FFN_CHUNK
        a = jnp.dot(xb, w_in_ref[:, lo:lo + FFN_CHUNK], preferred_element_type=F32)
        u = jnp.dot(xb, w_in_ref[:, D_FF + lo:D_FF + lo + FFN_CHUNK], preferred_element_type=F32)
        g_ref[:, lo:lo + FFN_CHUNK] = (a * jax.nn.sigmoid(a) * u).astype(BF16)
    return jnp.dot(g_ref[...], w_out_ref[...], preferred_element_type=F32)


STAGE_ROWS = 256
STAGE_COLS = 512
STAGE_SLOTS = 6
STAGE_SHAPE = (STAGE_SLOTS, STAGE_ROWS, STAGE_COLS)


def _window_jobs(src_hbm, dst_ref, n_rows, n_cols, src_col0=0, dst_col0=0, fix=None, dst_row=None):
    jobs = []
    for r in range(0, n_rows, STAGE_ROWS):
        for c in range(0, n_cols, STAGE_COLS):
            w = min(STAGE_COLS, n_cols - c)
            src = src_hbm.at[0, pl.ds(r, STAGE_ROWS), pl.ds(src_col0 + c, w)]

            def store(tile, r=r, c=c, w=w):
                out = (fix(tile) if fix else tile).astype(BF16)
                cols = slice(dst_col0 + c, dst_col0 + c + w)
                if dst_row is None:
                    dst_ref[r:r + STAGE_ROWS, cols] = out
                else:
                    for b in range(0, STAGE_ROWS, HEAD_DIM):
                        dst_ref[dst_row(r + b):dst_row(r + b) + HEAD_DIM, cols] = out[b:b + HEAD_DIM]

            jobs.append((src, w, store))
    return jobs


def _stream_cast(jobs, stage_ref, sem_ref):
    def copy(n):
        src, w, _ = jobs[n]
        slot = n % STAGE_SLOTS
        dst = stage_ref.at[slot] if w == STAGE_COLS else stage_ref.at[slot, :, pl.ds(0, w)]
        return pltpu.make_async_copy(src, dst, sem_ref.at[slot])

    ahead = STAGE_SLOTS - 1
    for n in range(min(ahead, len(jobs))):
        copy(n).start()
    for n, (_, w, store) in enumerate(jobs):
        if n + ahead < len(jobs):
            copy(n + ahead).start()
        copy(n).wait()
        store(stage_ref[n % STAGE_SLOTS, :, 0:w])


def _to_slot_order(q):
    lane = lax.broadcasted_iota(jnp.int32, (q.shape[0], 128), 1)
    t = [q[:, i * 128:(i + 1) * 128] for i in range(4)]
    swap = lambda a: pltpu.roll(a, 64, 1)
    return jnp.concatenate([jnp.where(lane < 64, t[0], swap(t[2])), jnp.where(lane < 64, swap(t[0]), t[2]),
                            jnp.where(lane < 64, t[1], swap(t[3])), jnp.where(lane < 64, swap(t[1]), t[3])],
                           axis=1)


def _ffn_qkv_kernel(x_ref, w_in_hbm, w_out_hbm, g_ref, b_ref, w_mix_hbm,
                    x1_ref, qa_ref, ka_ref, va_ref, qb_ref, kb_ref, vb_ref,
                    act_ref, w_in_ref, w_out_ref, wqkv_ref, stage_ref, sem_ref):
    @pl.when(pl.program_id(0) == 0)
    def _():
        jobs = _window_jobs(w_in_hbm, w_in_ref, D_MODEL, 2 * D_FF)
        jobs += _window_jobs(w_out_hbm, w_out_ref, D_FF, D_MODEL)
        for col0, n_cols, fix in ((0, Q_COLS, _to_slot_order), (Q_COLS, 2 * KV_COLS, None),
                                  (QKV_COLS // 2, Q_COLS, _to_slot_order),
                                  (QKV_COLS // 2 + Q_COLS, 2 * KV_COLS, None)):
            jobs += _window_jobs(w_mix_hbm, wqkv_ref, D_MODEL, n_cols, col0, col0, fix)
        _stream_cast(jobs, stage_ref, sem_ref)

    x = x_ref[...]
    f = _swiglu(x.astype(BF16), w_in_ref, w_out_ref, act_ref)
    x1 = _layer_norm(ALPHA * x + 0.5 * f, g_ref[...], b_ref[...])
    x1_ref[...] = x1
    h = jnp.dot(x1.astype(BF16), wqkv_ref[...], preferred_element_type=F32)
    qa_ref[...] = (h[:, 0:512] * Q_SCALE).astype(BF16)
    ka_ref[...] = h[:, 512:640].astype(BF16)
    va_ref[...] = h[:, 640:768].astype(BF16)
    qb_ref[...] = (h[:, 768:1280] * Q_SCALE).astype(BF16)
    kb_ref[...] = h[:, 1280:1408].astype(BF16)
    vb_ref[...] = h[:, 1408:1536].astype(BF16)


def _ffn_qkv(x, w_in, w_out, g, b, w_mix):
    tm = 2 * TOKEN_TILE
    row = lambda n: pl.BlockSpec((tm, n), lambda i: (i, 0))
    hbm = pl.BlockSpec(memory_space=pl.ANY)
    out_cols = (D_MODEL, Q_COLS, KV_COLS, KV_COLS, Q_COLS, KV_COLS, KV_COLS)
    out_dtypes = (F32,) + (BF16,) * 6
    return pl.pallas_call(
        _ffn_qkv_kernel,
        grid=(TOKENS // tm,),
        in_specs=[row(D_MODEL), hbm, hbm, _const_spec(g.shape), _const_spec(b.shape), hbm],
        out_specs=[row(n) for n in out_cols],
        out_shape=[jax.ShapeDtypeStruct((TOKENS, n), dt) for n, dt in zip(out_cols, out_dtypes)],
        scratch_shapes=[pltpu.VMEM((tm, D_FF), BF16),
                        pltpu.VMEM((D_MODEL, 2 * D_FF), BF16),
                        pltpu.VMEM((D_FF, D_MODEL), BF16),
                        pltpu.VMEM((D_MODEL, QKV_COLS), BF16),
                        pltpu.VMEM(STAGE_SHAPE, F32),
                        pltpu.SemaphoreType.DMA((STAGE_SLOTS,))],
        compiler_params=pltpu.CompilerParams(
            dimension_semantics=("arbitrary",), vmem_limit_bytes=V7X_VMEM_LIMIT_BYTES),
        name="ffn_qkv",
    )(x, w_in, w_out, g, b, w_mix)


SWA_ROW_SLOTS = (0, 2, 4, 6, 1, 3, 5, 7)


def _swa_kernel(sink_ref, q_ref, kp_ref, kc_ref, vp_ref, vc_ref, o_ref, qconst_ref, bias_ref):
    i = pl.program_id(1)
    w = SWA_WINDOW
    rows = N_HEADS * w
    lane = lax.broadcasted_iota(jnp.int32, (w, 128), 1)

    @pl.when((pl.program_id(0) == 0) & (i == 0))
    def _():
        rowf = lax.broadcasted_iota(jnp.int32, (w, 128), 0).astype(F32)
        for rb, slot in enumerate(SWA_ROW_SLOTS):
            slope = jnp.full((w, 128), ALIBI_SLOPES[HEAD_OF_SLOT[slot]], F32)
            qc = jnp.zeros((w, 128), F32)
            for first, terms in ((ROW_LANE, _split3(-slope * (rowf + w))), (COL_LANE, _split3(slope))):
                for n, term in enumerate(terms):
                    qc = jnp.where(lane == first + n, term.astype(F32), qc)
            qconst_ref[rb * w:(rb + 1) * w, :] = qc.astype(BF16)
        dist = (lax.broadcasted_iota(jnp.int32, (w, 2 * w), 0) + w
                - lax.broadcasted_iota(jnp.int32, (w, 2 * w), 1))
        bias_ref[...] = jnp.where((dist >= 0) & (dist < w), 0.0, NEG)

    k_all = jnp.concatenate([kp_ref[...], kc_ref[...]], axis=0)
    v_all = jnp.concatenate([vp_ref[...], vc_ref[...]], axis=0)
    key_col = lax.broadcasted_iota(jnp.int32, (2 * w, 128), 0).astype(F32)
    lane2 = lax.broadcasted_iota(jnp.int32, (2 * w, 128), 1)
    k_extra = jnp.where((lane2 >= ROW_LANE) & (lane2 < ROW_LANE + 3), 1.0,
                        jnp.where((lane2 >= COL_LANE) & (lane2 < COL_LANE + 3), key_col, 0.0)).astype(BF16)
    sink = jnp.concatenate([jnp.full((w, 128), sink_ref[HEAD_OF_SLOT[slot]] * LOG2E, F32)
                            for slot in SWA_ROW_SLOTS], axis=0)
    ones = jnp.ones((2 * w, 128), BF16)
    bias = bias_ref[...]
    kcol = lax.broadcasted_iota(jnp.int32, (w, 2 * w), 1)
    bias_first = jnp.where((kcol >= w) | (i > 0), bias, NEG)

    for r in range(SWA_TILE // w):
        q = q_ref[r * w:(r + 1) * w, :]
        q_aug = jnp.concatenate(
            [jnp.concatenate([q[:, (slot // 2) * 128:(slot // 2 + 1) * 128],
                              qconst_ref[rb * w:(rb + 1) * w, :]], axis=1)
             for rb, slot in enumerate(SWA_ROW_SLOTS)], axis=0)
        k = k_all[r * w:(r + 2) * w, :]
        zero = jnp.zeros_like(k)
        nt = (((1,), (1,)), ((), ()))
        s0 = lax.dot_general(q_aug[:rows // 2], jnp.concatenate([jnp.where(lane2 < 64, k, zero), k_extra], axis=1),
                             nt, preferred_element_type=F32)
        s1 = lax.dot_general(q_aug[rows // 2:], jnp.concatenate([jnp.where(lane2 >= 64, k, zero), k_extra], axis=1),
                             nt, preferred_element_type=F32)
        b = bias_first if r == 0 else bias
        s = (jnp.concatenate([s0, s1], axis=0).reshape(N_HEADS, w, 2 * w) + b[None]).reshape(rows, 2 * w)
        row_max = jnp.max(jnp.maximum(s[:, :128], s[:, 128:]), axis=-1, keepdims=True)
        m = jnp.maximum(jnp.broadcast_to(row_max, (rows, 128)), sink)
        p = jnp.exp2(s - jnp.concatenate([m, m], axis=1)).astype(BF16)
        v_aug = jnp.concatenate([v_all[r * w:(r + 2) * w, :], ones], axis=1)
        pv = jnp.dot(p, v_aug, preferred_element_type=F32)
        out = pv[:, :128] / (pv[:, 128:] + jnp.exp2(sink - m))
        tiles = []
        for pair in range(4):
            lo = out[pair * w:(pair + 1) * w, :]
            hi = out[(4 + pair) * w:(5 + pair) * w, :]
            tiles.append(jnp.where(lane < 64, lo, hi))
        o_ref[r * w:(r + 1) * w, :] = jnp.concatenate(tiles, axis=1).astype(BF16)


def _swa(sinks, qa, ka, va):
    w = SWA_WINDOW
    per_tile = SWA_TILE // w
    cur = lambda b, i: (b, i, 0)
    prev = lambda b, i: (b, jnp.maximum(i * per_tile - 1, 0), 0)
    return pl.pallas_call(
        _swa_kernel,
        grid=(BATCH, SEQ // SWA_TILE),
        in_specs=[pl.BlockSpec(memory_space=pltpu.SMEM),
                  pl.BlockSpec((None, SWA_TILE, Q_COLS), cur),
                  pl.BlockSpec((None, w, KV_COLS), prev),
                  pl.BlockSpec((None, SWA_TILE, KV_COLS), cur),
                  pl.BlockSpec((None, w, KV_COLS), prev),
                  pl.BlockSpec((None, SWA_TILE, KV_COLS), cur)],
        out_specs=pl.BlockSpec((None, SWA_TILE, Q_COLS), cur),
        out_shape=jax.ShapeDtypeStruct((BATCH, SEQ, Q_COLS), BF16),
        scratch_shapes=[pltpu.VMEM((N_HEADS * w, 128), BF16),
                        pltpu.VMEM((w, 2 * w), F32)],
        compiler_params=pltpu.CompilerParams(dimension_semantics=("arbitrary", "arbitrary")),
        name="swa",
    )(sinks, qa, ka, ka, va, va)


def _moba_kernel(q_ref, k_ref, v_ref, o_ref,
                 selb_ref, qconst_ref, qaug_ref, u_ref, mpart_ref, acc_ref):
    t = pl.program_id(1)
    bl = MOBA_BLOCK
    nbk = N_MOBA_BLOCKS
    rows = N_HEADS * bl
    lane = lax.broadcasted_iota(jnp.int32, (bl, 128), 1)
    lane_blk = lane % nbk
    lane_slot = lane // nbk
    sel_lane = lane < SEL_LANES
    in3 = lambda first: (lane >= first) & (lane < first + 3)

    def lanes3(first, terms):
        out = jnp.zeros((bl, 128), F32)
        for i, term in enumerate(terms):
            out = jnp.where(lane == first + i, term.astype(F32), out)
        return out

    @pl.when((pl.program_id(0) == 0) & (t == 0))
    def _():
        rowf = lax.broadcasted_iota(jnp.int32, (bl, 128), 0).astype(F32)
        for slot in range(N_HEADS):
            slope = jnp.full((bl, 128), ALIBI_SLOPES[N_HEADS + HEAD_OF_SLOT[slot]], F32)
            qc = (lanes3(ROW_LANE, _split3(-slope * rowf)) + lanes3(COL_LANE, _split3(slope))
                  + lanes3(BLK_LANE, _split3(-slope)))
            qconst_ref[slot * bl:(slot + 1) * bl, :] = qc.astype(BF16)

    @pl.when(t == 0)
    def _():
        rsel = lax.broadcasted_iota(jnp.int32, (128, SEQ), 0)
        tsel = lax.broadcasted_iota(jnp.int32, (128, SEQ), 1)
        avg = jnp.where((rsel % nbk) == (tsel // bl), 1.0 / bl, 0.0).astype(BF16)
        kmean_rows = jnp.dot(avg, k_ref[...], preferred_element_type=F32)
        kmean_t = kmean_rows.T
        r128 = lax.broadcasted_iota(jnp.int32, (128, 128), 0)
        c128 = lax.broadcasted_iota(jnp.int32, (128, 128), 1)
        km = jnp.concatenate(
            [jnp.where((c128 // nbk) == 2 * pair + (r128 >= 64).astype(jnp.int32), kmean_t, 0.0)
             for pair in range(4)], axis=0)
        q_all = q_ref[...]
        gate = sum(jnp.dot(q_all, term, preferred_element_type=F32) for term in _split3(km)[:2])
        gate_t = gate.T[:SEL_LANES].reshape(N_HEADS, nbk, SEQ)
        blk = lax.broadcasted_iota(jnp.int32, (N_HEADS, nbk, SEQ), 1)
        own = lax.broadcasted_iota(jnp.int32, (N_HEADS, nbk, SEQ), 2) // bl
        past = blk < own
        gm = jnp.where(past, gate_t, -jnp.inf)
        rank = jnp.zeros((N_HEADS, nbk, SEQ), jnp.int32)
        for d in range(1, nbk):
            partner = pltpu.roll(gm, nbk - d, 1)
            wrapped = blk + d >= nbk
            beats = (partner > gm) | ((partner == gm) & wrapped)
            rank = rank + beats.astype(jnp.int32)
        selected = past & (rank < MOBA_TOPK)
        selb_t = jnp.where(selected, 0.0, NEG).reshape(SEL_LANES, SEQ)
        selb_ref[...] = jnp.concatenate([selb_t, jnp.zeros_like(selb_t)], axis=0).T

    q = q_ref[pl.ds(pl.multiple_of(t * bl, bl), bl), :]
    selb = selb_ref[pl.ds(pl.multiple_of(t * bl, bl), bl), :]

    for slot in range(N_HEADS):
        pair, half = divmod(slot, 2)
        qp = q[:, pair * 128:(pair + 1) * 128]
        keep = (lane < 64) if half == 0 else (lane >= 64)
        qm = jnp.where(keep, qp, jnp.zeros_like(qp))
        extra = jnp.where(sel_lane & (lane_slot == slot), selb,
                          qconst_ref[slot * bl:(slot + 1) * bl, :].astype(F32))
        qaug_ref[slot * bl:(slot + 1) * bl, :] = jnp.concatenate([qm, extra.astype(BF16)], axis=1)

    key_col = lax.broadcasted_iota(jnp.int32, (bl, 128), 0).astype(F32)

    def scores(j, diag):
        kj = k_ref[pl.ds(pl.multiple_of(j * bl, bl), bl), :]
        if diag:
            extra = jnp.where(in3(ROW_LANE), 1.0, jnp.where(in3(COL_LANE), key_col, 0.0))
        else:
            offset = ((t - j) * bl).astype(F32)
            extra = jnp.where(sel_lane, jnp.where(lane_blk == j, 1.0, 0.0),
                              jnp.where(in3(ROW_LANE), 1.0,
                                        jnp.where(in3(COL_LANE), key_col,
                                                  jnp.where(in3(BLK_LANE), offset, 0.0))))
        kaug = jnp.concatenate([kj, extra.astype(BF16)], axis=1)
        return lax.dot_general(qaug_ref[...], kaug, (((1,), (1,)), ((), ())),
                               preferred_element_type=F32)

    def fold(s):
        return jnp.maximum(s[:, :128], s[:, 128:])

    qrow = lax.broadcasted_iota(jnp.int32, (rows, bl), 0) & (bl - 1)
    kcol = lax.broadcasted_iota(jnp.int32, (rows, bl), 1)

    def pass1(j, own):
        sj = scores(j, own)
        if own:
            sj = jnp.where(qrow >= kcol, sj, NEG)
        u_ref[j] = sj
        return fold(sj)

    def pass2(j):
        sj = u_ref[j]
        m = mpart_ref[...]
        pb = jnp.concatenate([jnp.exp2(sj[:, :128] - m), jnp.exp2(sj[:, 128:] - m)], axis=1).astype(BF16)
        vj = v_ref[pl.ds(pl.multiple_of(j * bl, bl), bl), :]
        return jnp.dot(pb, jnp.concatenate([vj, jnp.ones_like(vj)], axis=1), preferred_element_type=F32)

    odd_tile = (t & 1) == 1
    first_pair = t & 1
    n_pairs = t // 2

    @pl.when(jnp.logical_not(odd_tile))
    def _():
        mpart_ref[...] = pass1(t, True)

    @pl.when(odd_tile)
    def _():
        mpart_ref[...] = jnp.maximum(pass1(t, True), pass1(0, False))

    def pass1_pair(n, carry):
        j = first_pair + 2 * n
        mpart_ref[...] = jnp.maximum(mpart_ref[...], jnp.maximum(pass1(j, False), pass1(j + 1, False)))
        return carry

    lax.fori_loop(0, n_pairs, pass1_pair, 0)

    mpart_ref[...] = jnp.broadcast_to(jnp.max(mpart_ref[...], axis=-1, keepdims=True), (rows, 128))

    @pl.when(jnp.logical_not(odd_tile))
    def _():
        acc_ref[...] = pass2(t)

    @pl.when(odd_tile)
    def _():
        acc_ref[...] = pass2(t) + pass2(0)

    def pass2_pair(n, carry):
        j = first_pair + 2 * n
        acc_ref[...] += pass2(j) + pass2(j + 1)
        return carry

    lax.fori_loop(0, n_pairs, pass2_pair, 0)

    out = acc_ref[:, :128] / acc_ref[:, 128:]
    tiles = []
    for pair in range(4):
        lo = out[(2 * pair) * bl:(2 * pair + 1) * bl, :]
        hi = out[(2 * pair + 1) * bl:(2 * pair + 2) * bl, :]
        tiles.append(jnp.where(lane < 64, lo, hi))
    o_ref[...] = jnp.concatenate(tiles, axis=1).astype(BF16)


def _moba(qb, kb, vb):
    bl = MOBA_BLOCK
    rows = N_HEADS * bl
    tile = lambda b, t: (b, t, 0)
    whole = lambda b, t: (b, 0, 0)
    return pl.pallas_call(
        _moba_kernel,
        grid=(BATCH, N_MOBA_BLOCKS),
        in_specs=[pl.BlockSpec((None, SEQ, Q_COLS), whole),
                  pl.BlockSpec((None, SEQ, KV_COLS), whole),
                  pl.BlockSpec((None, SEQ, KV_COLS), whole)],
        out_specs=pl.BlockSpec((None, bl, Q_COLS), tile),
        out_shape=jax.ShapeDtypeStruct((BATCH, SEQ, Q_COLS), BF16),
        scratch_shapes=[pltpu.VMEM((SEQ, 128), F32),
                        pltpu.VMEM((rows, 128), BF16),
                        pltpu.VMEM((rows, 256), BF16),
                        pltpu.VMEM((N_MOBA_BLOCKS, rows, bl), F32),
                        pltpu.VMEM((rows, 128), F32),
                        pltpu.VMEM((rows, 256), F32)],
        compiler_params=pltpu.CompilerParams(
            dimension_semantics=("arbitrary", "arbitrary"), vmem_limit_bytes=V7X_VMEM_LIMIT_BYTES),
        name="moba",
    )(qb, kb, vb)


def _mix_ffn_kernel(x1_ref, ya_ref, yb_ref, w_mix_hbm, wa_hbm, wb_hbm, wo_hbm, g2_ref, b2_ref,
                    w_in_hbm, w_out_hbm, g3_ref, b3_ref, o_ref,
                    act_ref, wg_ref, wa_ref, wb_ref, wo_ref, w_in_ref, w_out_ref, stage_ref, sem_ref):
    @pl.when(pl.program_id(0) == 0)
    def _():
        jobs = _window_jobs(w_mix_hbm, wg_ref, D_MODEL, 2 * D_MODEL, QKV_COLS)
        slot_row = lambda r: HEAD_OF_SLOT.index(r // HEAD_DIM) * HEAD_DIM
        jobs += _window_jobs(wa_hbm, wa_ref, Q_COLS, D_MODEL, dst_row=slot_row)
        jobs += _window_jobs(wb_hbm, wb_ref, Q_COLS, D_MODEL, dst_row=slot_row)
        jobs += _window_jobs(wo_hbm, wo_ref, D_MODEL, D_MODEL)
        jobs += _window_jobs(w_in_hbm, w_in_ref, D_MODEL, 2 * D_FF)
        jobs += _window_jobs(w_out_hbm, w_out_ref, D_FF, D_MODEL)
        _stream_cast(jobs, stage_ref, sem_ref)

    x1 = x1_ref[...]
    x1b = x1.astype(BF16)
    ga = jnp.dot(x1b, wg_ref[:, :D_MODEL], preferred_element_type=F32)
    ya = jnp.dot(ya_ref[...], wa_ref[...], preferred_element_type=F32)
    y = jax.nn.sigmoid(ga) * ya
    gb = jnp.dot(x1b, wg_ref[:, D_MODEL:], preferred_element_type=F32)
    yb = jnp.dot(yb_ref[...], wb_ref[...], preferred_element_type=F32)
    y = y + jax.nn.sigmoid(gb) * yb
    z = jnp.dot(y.astype(BF16), wo_ref[...], preferred_element_type=F32)
    x2 = _layer_norm(ALPHA * x1 + z, g2_ref[...], b2_ref[...])
    f = _swiglu(x2.astype(BF16), w_in_ref, w_out_ref, act_ref)
    o_ref[...] = _layer_norm(ALPHA * x2 + 0.5 * f, g3_ref[...], b3_ref[...])


def _mix_ffn(x1, ya, yb, w_mix, wa, wb, wo, g2, b2, w_in, w_out, g3, b3):
    tm = TOKEN_TILE
    row = lambda n: pl.BlockSpec((tm, n), lambda i: (i, 0))
    hbm = pl.BlockSpec(memory_space=pl.ANY)
    ln = _const_spec(g2.shape)
    return pl.pallas_call(
        _mix_ffn_kernel,
        grid=(TOKENS // tm,),
        in_specs=[row(D_MODEL), row(Q_COLS), row(Q_COLS), hbm, hbm, hbm, hbm, ln, ln, hbm, hbm, ln, ln],
        out_specs=row(D_MODEL),
        out_shape=jax.ShapeDtypeStruct((TOKENS, D_MODEL), F32),
        scratch_shapes=[pltpu.VMEM((tm, D_FF), BF16),
                        pltpu.VMEM((D_MODEL, 2 * D_MODEL), BF16),
                        pltpu.VMEM((Q_COLS, D_MODEL), BF16),
                        pltpu.VMEM((Q_COLS, D_MODEL), BF16),
                        pltpu.VMEM((D_MODEL, D_MODEL), BF16),
                        pltpu.VMEM((D_MODEL, 2 * D_FF), BF16),
                        pltpu.VMEM((D_FF, D_MODEL), BF16),
                        pltpu.VMEM(STAGE_SHAPE, F32),
                        pltpu.SemaphoreType.DMA((STAGE_SLOTS,))],
        compiler_params=pltpu.CompilerParams(
            dimension_semantics=("arbitrary",), vmem_limit_bytes=V7X_VMEM_LIMIT_BYTES),
        name="mix_ffn",
    )(x1, ya, yb, w_mix, wa, wb, wo, g2, b2, w_in, w_out, g3, b3)


def kernel(x, ffn1_w_in, ffn1_w_out, ln1_g, ln1_b, mix_w_in, swa_sinks, w_branch_a, w_branch_b,
           mix_w_o, ln2_g, ln2_b, ffn2_w_in, ffn2_w_out, ln3_g, ln3_b):
    assert x.shape == (BATCH, SEQ, D_MODEL) and ffn1_w_in.shape == (1, D_MODEL, 2 * D_FF)
    assert mix_w_in.shape == (1, D_MODEL, MIX_IN_COLS)
    x1, qa, ka, va, qb, kb, vb = _ffn_qkv(x.reshape(TOKENS, D_MODEL), ffn1_w_in, ffn1_w_out,
                                          ln1_g, ln1_b, mix_w_in)
    seq3 = lambda a: a.reshape(BATCH, SEQ, a.shape[-1])
    ya = _swa(swa_sinks[0], seq3(qa), seq3(ka), seq3(va))
    yb = _moba(seq3(qb), seq3(kb), seq3(vb))
    out = _mix_ffn(x1, ya.reshape(TOKENS, Q_COLS), yb.reshape(TOKENS, Q_COLS), mix_w_in,
                   w_branch_a, w_branch_b, mix_w_o, ln2_g, ln2_b, ffn2_w_in, ffn2_w_out, ln3_g, ln3_b)
    return out.reshape(BATCH, SEQ, D_MODEL)
```

```python
import functools

import jax
import jax.numpy as jnp
from jax import lax
from jax.experimental import pallas as pl
from jax.experimental.pallas import tpu as pltpu

F32 = jnp.float32
BF16 = jnp.bfloat16

D_MODEL = 1024
BATCH = 8
SEQ = 2048
TOKENS = BATCH * SEQ
HEAD_DIM = 64
N_HEADS = 8
KV_COLS = 128
Q_COLS = 512
QKV_COLS = 2 * (Q_COLS + 2 * KV_COLS)
MIX_IN_COLS = QKV_COLS + 2 * D_MODEL
SWA_WINDOW = 128
SWA_TILE = 512
MOBA_BLOCK = 256
MOBA_TOPK = 3
N_MOBA_BLOCKS = SEQ // MOBA_BLOCK
D_FF = 2816
FFN_CHUNK = 256
N_FFN_CHUNKS = D_FF // FFN_CHUNK
ALPHA = 2.0 ** 0.25
LN_EPS = 1e-5
NEG = -1e30
LOG2E = 1.4426950408889634
Q_SCALE = 0.125 * LOG2E
ALIBI_SLOPES = tuple(float(2.0 ** (-8.0 * i / 16.0)) * LOG2E for i in range(1, 17))

HEAD_OF_SLOT = (0, 4, 1, 5, 2, 6, 3, 7)

V7X_VMEM_LIMIT_BYTES = 60 * 1024 * 1024
TOKEN_TILE = 512

SEL_LANES = 64
ROW_LANE = 64
COL_LANE = 67
BLK_LANE = 70


def _const_spec(shape):
    zeros = (0,) * len(shape)
    return pl.BlockSpec(shape, lambda *_: zeros, pipeline_mode=pl.Buffered(1))


def _split3(x):
    hi = x.astype(BF16)
    r1 = x - hi.astype(F32)
    mid = r1.astype(BF16)
    lo = (r1 - mid.astype(F32)).astype(BF16)
    return hi, mid, lo


def _layer_norm(y, g, b):
    mu = jnp.mean(y, axis=-1, keepdims=True)
    yc = y - mu
    var = jnp.mean(yc * yc, axis=-1, keepdims=True)
    return yc * lax.rsqrt(var + LN_EPS) * g + b


def _swiglu(xb, w_in_ref, w_out_ref, g_ref):
    for c in range(N_FFN_CHUNKS):
        lo = c * FFN_CHUNK
        a = jnp.dot(xb, w_in_ref[:, lo:lo + FFN_CHUNK], preferred_element_type=F32)
        u = jnp.dot(xb, w_in_ref[:, D_FF + lo:D_FF + lo + FFN_CHUNK], preferred_element_type=F32)
        g_ref[:, lo:lo + FFN_CHUNK] = (a * jax.nn.sigmoid(a) * u).astype(BF16)
    return jnp.dot(g_ref[...], w_out_ref[...], preferred_element_type=F32)


STAGE_ROWS = 256
STAGE_COLS = 512
STAGE_SLOTS = 6
STAGE_SHAPE = (STAGE_SLOTS, STAGE_ROWS, STAGE_COLS)


def _window_jobs(src_hbm, dst_ref, n_rows, n_cols, src_col0=0, dst_col0=0, fix=None, dst_row=None):
    jobs = []
    for r in range(0, n_rows, STAGE_ROWS):
        for c in range(0, n_cols, STAGE_COLS):
            w = min(STAGE_COLS, n_cols - c)
            src = src_hbm.at[0, pl.ds(r, STAGE_ROWS), pl.ds(src_col0 + c, w)]

            def store(tile, r=r, c=c, w=w):
                out = (fix(tile) if fix else tile).astype(BF16)
                cols = slice(dst_col0 + c, dst_col0 + c + w)
                if dst_row is None:
                    dst_ref[r:r + STAGE_ROWS, cols] = out
                else:
                    for b in range(0, STAGE_ROWS, HEAD_DIM):
                        dst_ref[dst_row(r + b):dst_row(r + b) + HEAD_DIM, cols] = out[b:b + HEAD_DIM]

            jobs.append((src, w, store))
    return jobs


def _stream_cast(jobs, stage_ref, sem_ref):
    def copy(n):
        src, w, _ = jobs[n]
        slot = n % STAGE_SLOTS
        dst = stage_ref.at[slot] if w == STAGE_COLS else stage_ref.at[slot, :, pl.ds(0, w)]
        return pltpu.make_async_copy(src, dst, sem_ref.at[slot])

    ahead = STAGE_SLOTS - 1
    for n in range(min(ahead, len(jobs))):
        copy(n).start()
    for n, (_, w, store) in enumerate(jobs):
        if n + ahead < len(jobs):
            copy(n + ahead).start()
        copy(n).wait()
        store(stage_ref[n % STAGE_SLOTS, :, 0:w])


def _to_slot_order(q):
    lane = lax.broadcasted_iota(jnp.int32, (q.shape[0], 128), 1)
    t = [q[:, i * 128:(i + 1) * 128] for i in range(4)]
    swap = lambda a: pltpu.roll(a, 64, 1)
    return jnp.concatenate([jnp.where(lane < 64, t[0], swap(t[2])), jnp.where(lane < 64, swap(t[0]), t[2]),
                            jnp.where(lane < 64, t[1], swap(t[3])), jnp.where(lane < 64, swap(t[1]), t[3])],
                           axis=1)


def _ffn_qkv_kernel(x_ref, w_in_hbm, w_out_hbm, g_ref, b_ref, w_mix_hbm,
                    x1_ref, qa_ref, ka_ref, va_ref, qb_ref, kb_ref, vb_ref,
                    act_ref, w_in_ref, w_out_ref, wqkv_ref, stage_ref, sem_ref):
    @pl.when(pl.program_id(0) == 0)
    def _():
        jobs = _window_jobs(w_in_hbm, w_in_ref, D_MODEL, 2 * D_FF)
        jobs += _window_jobs(w_out_hbm, w_out_ref, D_FF, D_MODEL)
        for col0, n_cols, fix in ((0, Q_COLS, _to_slot_order), (Q_COLS, 2 * KV_COLS, None),
                                  (QKV_COLS // 2, Q_COLS, _to_slot_order),
                                  (QKV_COLS // 2 + Q_COLS, 2 * KV_COLS, None)):
            jobs += _window_jobs(w_mix_hbm, wqkv_ref, D_MODEL, n_cols, col0, col0, fix)
        _stream_cast(jobs, stage_ref, sem_ref)

    x = x_ref[...]
    f = _swiglu(x.astype(BF16), w_in_ref, w_out_ref, act_ref)
    x1 = _layer_norm(ALPHA * x + 0.5 * f, g_ref[...], b_ref[...])
    x1_ref[...] = x1
    h = jnp.dot(x1.astype(BF16), wqkv_ref[...], preferred_element_type=F32)
    qa_ref[...] = (h[:, 0:512] * Q_SCALE).astype(BF16)
    ka_ref[...] = h[:, 512:640].astype(BF16)
    va_ref[...] = h[:, 640:768].astype(BF16)
    qb_ref[...] = (h[:, 768:1280] * Q_SCALE).astype(BF16)
    kb_ref[...] = h[:, 1280:1408].astype(BF16)
    vb_ref[...] = h[:, 1408:1536].astype(BF16)


def _ffn_qkv(x, w_in, w_out, g, b, w_mix):
    tm = 2 * TOKEN_TILE
    row = lambda n: pl.BlockSpec((tm, n), lambda i: (i, 0))
    hbm = pl.BlockSpec(memory_space=pl.ANY)
    out_cols = (D_MODEL, Q_COLS, KV_COLS, KV_COLS, Q_COLS, KV_COLS, KV_COLS)
    out_dtypes = (F32,) + (BF16,) * 6
    return pl.pallas_call(
        _ffn_qkv_kernel,
        grid=(TOKENS // tm,),
        in_specs=[row(D_MODEL), hbm, hbm, _const_spec(g.shape), _const_spec(b.shape), hbm],
        out_specs=[row(n) for n in out_cols],
        out_shape=[jax.ShapeDtypeStruct((TOKENS, n), dt) for n, dt in zip(out_cols, out_dtypes)],
        scratch_shapes=[pltpu.VMEM((tm, D_FF), BF16),
                        pltpu.VMEM((D_MODEL, 2 * D_FF), BF16),
                        pltpu.VMEM((D_FF, D_MODEL), BF16),
                        pltpu.VMEM((D_MODEL, QKV_COLS), BF16),
                        pltpu.VMEM(STAGE_SHAPE, F32),
                        pltpu.SemaphoreType.DMA((STAGE_SLOTS,))],
        compiler_params=pltpu.CompilerParams(
            dimension_semantics=("arbitrary",), vmem_limit_bytes=V7X_VMEM_LIMIT_BYTES),
        name="ffn_qkv",
    )(x, w_in, w_out, g, b, w_mix)


SWA_ROW_SLOTS = (0, 2, 4, 6, 1, 3, 5, 7)


def _swa_kernel(sink_ref, q_ref, kp_ref, kc_ref, vp_ref, vc_ref, o_ref, qconst_ref, bias_ref):
    i = pl.program_id(1)
    w = SWA_WINDOW
    rows = N_HEADS * w
    lane = lax.broadcasted_iota(jnp.int32, (w, 128), 1)

    @pl.when((pl.program_id(0) == 0) & (i == 0))
    def _():
        rowf = lax.broadcasted_iota(jnp.int32, (w, 128), 0).astype(F32)
        for rb, slot in enumerate(SWA_ROW_SLOTS):
            slope = jnp.full((w, 128), ALIBI_SLOPES[HEAD_OF_SLOT[slot]], F32)
            qc = jnp.zeros((w, 128), F32)
            for first, terms in ((ROW_LANE, _split3(-slope * (rowf + w))), (COL_LANE, _split3(slope))):
                for n, term in enumerate(terms):
                    qc = jnp.where(lane == first + n, term.astype(F32), qc)
            qconst_ref[rb * w:(rb + 1) * w, :] = qc.astype(BF16)
        dist = (lax.broadcasted_iota(jnp.int32, (w, 2 * w), 0) + w
                - lax.broadcasted_iota(jnp.int32, (w, 2 * w), 1))
        bias_ref[...] = jnp.where((dist >= 0) & (dist < w), 0.0, NEG)

    k_all = jnp.concatenate([kp_ref[...], kc_ref[...]], axis=0)
    v_all = jnp.concatenate([vp_ref[...], vc_ref[...]], axis=0)
    key_col = lax.broadcasted_iota(jnp.int32, (2 * w, 128), 0).astype(F32)
    lane2 = lax.broadcasted_iota(jnp.int32, (2 * w, 128), 1)
    k_extra = jnp.where((lane2 >= ROW_LANE) & (lane2 < ROW_LANE + 3), 1.0,
                        jnp.where((lane2 >= COL_LANE) & (lane2 < COL_LANE + 3), key_col, 0.0)).astype(BF16)
    sink = jnp.concatenate([jnp.full((w, 128), sink_ref[HEAD_OF_SLOT[slot]] * LOG2E, F32)
                            for slot in SWA_ROW_SLOTS], axis=0)
    ones = jnp.ones((2 * w, 128), BF16)
    bias = bias_ref[...]
    kcol = lax.broadcasted_iota(jnp.int32, (w, 2 * w), 1)
    bias_first = jnp.where((kcol >= w) | (i > 0), bias, NEG)

    for r in range(SWA_TILE // w):
        q = q_ref[r * w:(r + 1) * w, :]
        q_aug = jnp.concatenate(
            [jnp.concatenate([q[:, (slot // 2) * 128:(slot // 2 + 1) * 128],
                              qconst_ref[rb * w:(rb + 1) * w, :]], axis=1)
             for rb, slot in enumerate(SWA_ROW_SLOTS)], axis=0)
        k = k_all[r * w:(r + 2) * w, :]
        zero = jnp.zeros_like(k)
        nt = (((1,), (1,)), ((), ()))
        s0 = lax.dot_general(q_aug[:rows // 2], jnp.concatenate([jnp.where(lane2 < 64, k, zero), k_extra], axis=1),
                             nt, preferred_element_type=F32)
        s1 = lax.dot_general(q_aug[rows // 2:], jnp.concatenate([jnp.where(lane2 >= 64, k, zero), k_extra], axis=1),
                             nt, preferred_element_type=F32)
        b = bias_first if r == 0 else bias
        s = (jnp.concatenate([s0, s1], axis=0).reshape(N_HEADS, w, 2 * w) + b[None]).reshape(rows, 2 * w)
        row_max = jnp.max(jnp.maximum(s[:, :128], s[:, 128:]), axis=-1, keepdims=True)
        m = jnp.maximum(jnp.broadcast_to(row_max, (rows, 128)), sink)
        p = jnp.exp2(s - jnp.concatenate([m, m], axis=1)).astype(BF16)
        v_aug = jnp.concatenate([v_all[r * w:(r + 2) * w, :], ones], axis=1)
        pv = jnp.dot(p, v_aug, preferred_element_type=F32)
        out = pv[:, :128] / (pv[:, 128:] + jnp.exp2(sink - m))
        tiles = []
        for pair in range(4):
            lo = out[pair * w:(pair + 1) * w, :]
            hi = out[(4 + pair) * w:(5 + pair) * w, :]
            tiles.append(jnp.where(lane < 64, lo, hi))
        o_ref[r * w:(r + 1) * w, :] = jnp.concatenate(tiles, axis=1).astype(BF16)


def _swa(sinks, qa, ka, va):
    w = SWA_WINDOW
    per_tile = SWA_TILE // w
    cur = lambda b, i: (b, i, 0)
    prev = lambda b, i: (b, jnp.maximum(i * per_tile - 1, 0), 0)
    return pl.pallas_call(
        _swa_kernel,
        grid=(BATCH, SEQ // SWA_TILE),
        in_specs=[pl.BlockSpec(memory_space=pltpu.SMEM),
                  pl.BlockSpec((None, SWA_TILE, Q_COLS), cur),
                  pl.BlockSpec((None, w, KV_COLS), prev),
                  pl.BlockSpec((None, SWA_TILE, KV_COLS), cur),
                  pl.BlockSpec((None, w, KV_COLS), prev),
                  pl.BlockSpec((None, SWA_TILE, KV_COLS), cur)],
        out_specs=pl.BlockSpec((None, SWA_TILE, Q_COLS), cur),
        out_shape=jax.ShapeDtypeStruct((BATCH, SEQ, Q_COLS), BF16),
        scratch_shapes=[pltpu.VMEM((N_HEADS * w, 128), BF16),
                        pltpu.VMEM((w, 2 * w), F32)],
        compiler_params=pltpu.CompilerParams(dimension_semantics=("arbitrary", "arbitrary")),
        name="swa",
    )(sinks, qa, ka, ka, va, va)


def _moba_kernel(q_ref, k_ref, v_ref, o_ref,
                 selb_ref, qconst_ref, qaug_ref, u_ref, mpart_ref, acc_ref):
    step = pl.program_id(1)
    bl = MOBA_BLOCK
    nbk = N_MOBA_BLOCKS
    rows = N_HEADS * bl
    lane = lax.broadcasted_iota(jnp.int32, (bl, 128), 1)
    lane_blk = lane % nbk
    lane_slot = lane // nbk
    sel_lane = lane < SEL_LANES
    in3 = lambda first: (lane >= first) & (lane < first + 3)

    def lanes3(first, terms):
        out = jnp.zeros((bl, 128), F32)
        for i, term in enumerate(terms):
            out = jnp.where(lane == first + i, term.astype(F32), out)
        return out

    @pl.when((pl.program_id(0) == 0) & (step == 0))
    def _():
        rowf = lax.broadcasted_iota(jnp.int32, (bl, 128), 0).astype(F32)
        for slot in range(N_HEADS):
            slope = jnp.full((bl, 128), ALIBI_SLOPES[N_HEADS + HEAD_OF_SLOT[slot]], F32)
            qc = (lanes3(ROW_LANE, _split3(-slope * rowf)) + lanes3(COL_LANE, _split3(slope))
                  + lanes3(BLK_LANE, _split3(-slope)))
            qconst_ref[slot * bl:(slot + 1) * bl, :] = qc.astype(BF16)

    @pl.when(step == 0)
    def _():
        rsel = lax.broadcasted_iota(jnp.int32, (128, SEQ), 0)
        tsel = lax.broadcasted_iota(jnp.int32, (128, SEQ), 1)
        avg = jnp.where((rsel % nbk) == (tsel // bl), 1.0 / bl, 0.0).astype(BF16)
        kmean_rows = jnp.dot(avg, k_ref[...], preferred_element_type=F32)
        kmean_t = kmean_rows.T
        r128 = lax.broadcasted_iota(jnp.int32, (128, 128), 0)
        c128 = lax.broadcasted_iota(jnp.int32, (128, 128), 1)
        km = jnp.concatenate(
            [jnp.where((c128 // nbk) == 2 * pair + (r128 >= 64).astype(jnp.int32), kmean_t, 0.0)
             for pair in range(4)], axis=0)
        q_all = q_ref[...]
        gate = sum(jnp.dot(q_all, term, preferred_element_type=F32) for term in _split3(km)[:2])
        gate_t = gate.T[:SEL_LANES].reshape(N_HEADS, nbk, SEQ)
        blk = lax.broadcasted_iota(jnp.int32, (N_HEADS, nbk, SEQ), 1)
        own = lax.broadcasted_iota(jnp.int32, (N_HEADS, nbk, SEQ), 2) // bl
        past = blk < own
        gm = jnp.where(past, gate_t, -jnp.inf)
        rank = jnp.zeros((N_HEADS, nbk, SEQ), jnp.int32)
        for d in range(1, nbk):
            partner = pltpu.roll(gm, nbk - d, 1)
            wrapped = blk + d >= nbk
            beats = (partner > gm) | ((partner == gm) & wrapped)
            rank = rank + beats.astype(jnp.int32)
        selected = past & (rank < MOBA_TOPK)
        selb_t = jnp.where(selected, 0.0, NEG).reshape(SEL_LANES, SEQ)
        selb_ref[...] = jnp.concatenate([selb_t, jnp.zeros_like(selb_t)], axis=0).T

    def fold(s):
        return jnp.maximum(s[:, :128], s[:, 128:])

    def run_step(t_a):
        tiles_t = (t_a, t_a + 1)
        past = [(side, j) for j in range(t_a + 1) for side in range(2) if j < tiles_t[side]]

        for side, t in enumerate(tiles_t):
            q = q_ref[t * bl:(t + 1) * bl, :]
            selb = selb_ref[t * bl:(t + 1) * bl, :]
            for slot in range(N_HEADS):
                pair, half = divmod(slot, 2)
                qp = q[:, pair * 128:(pair + 1) * 128]
                keep = (lane < 64) if half == 0 else (lane >= 64)
                qm = jnp.where(keep, qp, jnp.zeros_like(qp))
                extra = jnp.where(sel_lane & (lane_slot == slot), selb,
                                  qconst_ref[slot * bl:(slot + 1) * bl, :].astype(F32))
                qaug_ref[side, slot * bl:(slot + 1) * bl, :] = jnp.concatenate(
                    [qm, extra.astype(BF16)], axis=1)

        key_col = lax.broadcasted_iota(jnp.int32, (bl, 128), 0).astype(F32)
        alibi_lanes = jnp.where(in3(ROW_LANE), 1.0, jnp.where(in3(COL_LANE), key_col, 0.0))

        def scores(side, j):
            t = tiles_t[side]
            if j == t:
                extra = alibi_lanes
            else:
                extra = jnp.where(sel_lane, jnp.where(lane_blk == j, 1.0, 0.0),
                                  jnp.where(in3(BLK_LANE), float((t - j) * bl), alibi_lanes))
            kaug = jnp.concatenate([k_ref[j * bl:(j + 1) * bl, :], extra.astype(BF16)], axis=1)
            return lax.dot_general(qaug_ref[side], kaug, (((1,), (1,)), ((), ())),
                                   preferred_element_type=F32)

        def probs_v(n, side, j):
            sj = u_ref[n]
            m = mpart_ref[side]
            pb = jnp.concatenate([jnp.exp2(sj[:, :128] - m), jnp.exp2(sj[:, 128:] - m)],
                                 axis=1).astype(BF16)
            vj = v_ref[j * bl:(j + 1) * bl, :]
            return jnp.dot(pb, jnp.concatenate([vj, jnp.ones_like(vj)], axis=1),
                           preferred_element_type=F32)

        qrow = lax.broadcasted_iota(jnp.int32, (rows, bl), 0) & (bl - 1)
        kcol = lax.broadcasted_iota(jnp.int32, (rows, bl), 1)
        for side, t in enumerate(tiles_t):
            s_own = jnp.where(qrow >= kcol, scores(side, t), NEG)
            u_ref[side] = s_own
            mpart_ref[side] = fold(s_own)
        for n, (side, j) in enumerate(past):
            sj = scores(side, j)
            u_ref[2 + n] = sj
            mpart_ref[side] = jnp.maximum(mpart_ref[side], fold(sj))

        for side in range(2):
            mpart_ref[side] = jnp.broadcast_to(jnp.max(mpart_ref[side], axis=-1, keepdims=True),
                                               (rows, 128))

        for side, t in enumerate(tiles_t):
            acc_ref[side] = probs_v(side, side, t)
        for n, (side, j) in enumerate(past):
            acc_ref[side] += probs_v(2 + n, side, j)

        for side, t in enumerate(tiles_t):
            out = acc_ref[side, :, :128] / acc_ref[side, :, 128:]
            heads = []
            for pair in range(4):
                lo = out[(2 * pair) * bl:(2 * pair + 1) * bl, :]
                hi = out[(2 * pair + 1) * bl:(2 * pair + 2) * bl, :]
                heads.append(jnp.where(lane < 64, lo, hi))
            o_ref[t * bl:(t + 1) * bl, :] = jnp.concatenate(heads, axis=1).astype(BF16)

    for pair_step in range(N_MOBA_BLOCKS // 2):
        pl.when(step == pair_step)(functools.partial(run_step, 2 * pair_step))


def _moba(qb, kb, vb):
    bl = MOBA_BLOCK
    rows = N_HEADS * bl
    step_blocks = 2 * N_MOBA_BLOCKS - 1
    whole = lambda b, t: (b, 0, 0)
    return pl.pallas_call(
        _moba_kernel,
        grid=(BATCH, N_MOBA_BLOCKS // 2),
        in_specs=[pl.BlockSpec((None, SEQ, Q_COLS), whole),
                  pl.BlockSpec((None, SEQ, KV_COLS), whole),
                  pl.BlockSpec((None, SEQ, KV_COLS), whole)],
        out_specs=pl.BlockSpec((None, SEQ, Q_COLS), whole),
        out_shape=jax.ShapeDtypeStruct((BATCH, SEQ, Q_COLS), BF16),
        scratch_shapes=[pltpu.VMEM((SEQ, 128), F32),
                        pltpu.VMEM((rows, 128), BF16),
                        pltpu.VMEM((2, rows, 256), BF16),
                        pltpu.VMEM((step_blocks, rows, bl), F32),
                        pltpu.VMEM((2, rows, 128), F32),
                        pltpu.VMEM((2, rows, 256), F32)],
        compiler_params=pltpu.CompilerParams(
            dimension_semantics=("arbitrary", "arbitrary"), vmem_limit_bytes=V7X_VMEM_LIMIT_BYTES),
        name="moba",
    )(qb, kb, vb)


def _mix_ffn_kernel(x1_ref, ya_ref, yb_ref, w_mix_hbm, wa_hbm, wb_hbm, wo_hbm, g2_ref, b2_ref,
                    w_in_hbm, w_out_hbm, g3_ref, b3_ref, o_ref,
                    act_ref, wg_ref, wa_ref, wb_ref, wo_ref, w_in_ref, w_out_ref, stage_ref, sem_ref):
    @pl.when(pl.program_id(0) == 0)
    def _():
        jobs = _window_jobs(w_mix_hbm, wg_ref, D_MODEL, 2 * D_MODEL, QKV_COLS)
        slot_row = lambda r: HEAD_OF_SLOT.index(r // HEAD_DIM) * HEAD_DIM
        jobs += _window_jobs(wa_hbm, wa_ref, Q_COLS, D_MODEL, dst_row=slot_row)
        jobs += _window_jobs(wb_hbm, wb_ref, Q_COLS, D_MODEL, dst_row=slot_row)
        jobs += _window_jobs(wo_hbm, wo_ref, D_MODEL, D_MODEL)
        jobs += _window_jobs(w_in_hbm, w_in_ref, D_MODEL, 2 * D_FF)
        jobs += _window_jobs(w_out_hbm, w_out_ref, D_FF, D_MODEL)
        _stream_cast(jobs, stage_ref, sem_ref)

    x1 = x1_ref[...]
    x1b = x1.astype(BF16)
    ga = jnp.dot(x1b, wg_ref[:, :D_MODEL], preferred_element_type=F32)
    ya = jnp.dot(ya_ref[...], wa_ref[...], preferred_element_type=F32)
    y = jax.nn.sigmoid(ga) * ya
    gb = jnp.dot(x1b, wg_ref[:, D_MODEL:], preferred_element_type=F32)
    yb = jnp.dot(yb_ref[...], wb_ref[...], preferred_element_type=F32)
    y = y + jax.nn.sigmoid(gb) * yb
    z = jnp.dot(y.astype(BF16), wo_ref[...], preferred_element_type=F32)
    x2 = _layer_norm(ALPHA * x1 + z, g2_ref[...], b2_ref[...])
    f = _swiglu(x2.astype(BF16), w_in_ref, w_out_ref, act_ref)
    o_ref[...] = _layer_norm(ALPHA * x2 + 0.5 * f, g3_ref[...], b3_ref[...])


def _mix_ffn(x1, ya, yb, w_mix, wa, wb, wo, g2, b2, w_in, w_out, g3, b3):
    tm = 2 * TOKEN_TILE
    row = lambda n: pl.BlockSpec((tm, n), lambda i: (i, 0))
    hbm = pl.BlockSpec(memory_space=pl.ANY)
    ln = _const_spec(g2.shape)
    return pl.pallas_call(
        _mix_ffn_kernel,
        grid=(TOKENS // tm,),
        in_specs=[row(D_MODEL), row(Q_COLS), row(Q_COLS), hbm, hbm, hbm, hbm, ln, ln, hbm, hbm, ln, ln],
        out_specs=row(D_MODEL),
        out_shape=jax.ShapeDtypeStruct((TOKENS, D_MODEL), F32),
        scratch_shapes=[pltpu.VMEM((tm, D_FF), BF16),
                        pltpu.VMEM((D_MODEL, 2 * D_MODEL), BF16),
                        pltpu.VMEM((Q_COLS, D_MODEL), BF16),
                        pltpu.VMEM((Q_COLS, D_MODEL), BF16),
                        pltpu.VMEM((D_MODEL, D_MODEL), BF16),
                        pltpu.VMEM((D_MODEL, 2 * D_FF), BF16),
                        pltpu.VMEM((D_FF, D_MODEL), BF16),
                        pltpu.VMEM(STAGE_SHAPE, F32),
                        pltpu.SemaphoreType.DMA((STAGE_SLOTS,))],
        compiler_params=pltpu.CompilerParams(
            dimension_semantics=("arbitrary",), vmem_limit_bytes=V7X_VMEM_LIMIT_BYTES),
        name="mix_ffn",
    )(x1, ya, yb, w_mix, wa, wb, wo, g2, b2, w_in, w_out, g3, b3)


def kernel(x, ffn1_w_in, ffn1_w_out, ln1_g, ln1_b, mix_w_in, swa_sinks, w_branch_a, w_branch_b,
           mix_w_o, ln2_g, ln2_b, ffn2_w_in, ffn2_w_out, ln3_g, ln3_b):
    assert x.shape == (BATCH, SEQ, D_MODEL) and ffn1_w_in.shape == (1, D_MODEL, 2 * D_FF)
    assert mix_w_in.shape == (1, D_MODEL, MIX_IN_COLS)
    x1, qa, ka, va, qb, kb, vb = _ffn_qkv(x.reshape(TOKENS, D_MODEL), ffn1_w_in, ffn1_w_out,
                                          ln1_g, ln1_b, mix_w_in)
    seq3 = lambda a: a.reshape(BATCH, SEQ, a.shape[-1])
    ya = _swa(swa_sinks[0], seq3(qa), seq3(ka), seq3(va))
    yb = _moba(seq3(qb), seq3(kb), seq3(vb))
    out = _mix_ffn(x1, ya.reshape(TOKENS, Q_COLS), yb.reshape(TOKENS, Q_COLS), mix_w_in,
                   w_branch_a, w_branch_b, mix_w_o, ln2_g, ln2_b, ffn2_w_in, ffn2_w_out, ln3_g, ln3_b)
    return out.reshape(BATCH, SEQ, D_MODEL)
```

```python
import functools

import jax
import jax.numpy as jnp
from jax import lax
from jax.experimental import pallas as pl
from jax.experimental.pallas import tpu as pltpu

F32 = jnp.float32
BF16 = jnp.bfloat16

D_MODEL = 1024
BATCH = 8
SEQ = 2048
TOKENS = BATCH * SEQ
HEAD_DIM = 64
N_HEADS = 8
KV_COLS = 128
Q_COLS = 512
QKV_COLS = 2 * (Q_COLS + 2 * KV_COLS)
MIX_IN_COLS = QKV_COLS + 2 * D_MODEL
SWA_WINDOW = 128
SWA_TILE = 512
MOBA_BLOCK = 256
MOBA_TOPK = 3
N_MOBA_BLOCKS = SEQ // MOBA_BLOCK
D_FF = 2816
FFN_CHUNK = 256
N_FFN_CHUNKS = D_FF // FFN_CHUNK
ALPHA = 2.0 ** 0.25
LN_EPS = 1e-5
NEG = -1e30
LOG2E = 1.4426950408889634
Q_SCALE = 0.125 * LOG2E
ALIBI_SLOPES = tuple(float(2.0 ** (-8.0 * i / 16.0)) * LOG2E for i in range(1, 17))

HEAD_OF_SLOT = (0, 4, 1, 5, 2, 6, 3, 7)

V7X_VMEM_LIMIT_BYTES = 60 * 1024 * 1024
TOKEN_TILE = 512

SEL_LANES = 64
ROW_LANE = 64
COL_LANE = 67
BLK_LANE = 70


def _const_spec(shape):
    zeros = (0,) * len(shape)
    return pl.BlockSpec(shape, lambda *_: zeros, pipeline_mode=pl.Buffered(1))


def _split3(x):
    hi = x.astype(BF16)
    r1 = x - hi.astype(F32)
    mid = r1.astype(BF16)
    lo = (r1 - mid.astype(F32)).astype(BF16)
    return hi, mid, lo


def _layer_norm(y, g, b):
    mu = jnp.mean(y, axis=-1, keepdims=True)
    yc = y - mu
    var = jnp.mean(yc * yc, axis=-1, keepdims=True)
    return yc * lax.rsqrt(var + LN_EPS) * g + b


def _swiglu(xb, w_in_ref, w_out_ref, g_ref, before_chunk=None):
    for c in range(N_FFN_CHUNKS):
        if before_chunk is not None:
            before_chunk(c)
        lo = c * FFN_CHUNK
        a = jnp.dot(xb, w_in_ref[:, lo:lo + FFN_CHUNK], preferred_element_type=F32)
        u = jnp.dot(xb, w_in_ref[:, D_FF + lo:D_FF + lo + FFN_CHUNK], preferred_element_type=F32)
        g_ref[:, lo:lo + FFN_CHUNK] = (a * jax.nn.sigmoid(a) * u).astype(BF16)
    return jnp.dot(g_ref[...], w_out_ref[...], preferred_element_type=F32)


STAGE_ROWS = 256
STAGE_COLS = 512
STAGE_SLOTS = 6
STAGE_SHAPE = (STAGE_SLOTS, STAGE_ROWS, STAGE_COLS)


def _window_jobs(src_hbm, dst_ref, n_rows, n_cols, src_col0=0, dst_col0=0, fix=None, dst_row=None):
    jobs = []
    for r in range(0, n_rows, STAGE_ROWS):
        for c in range(0, n_cols, STAGE_COLS):
            w = min(STAGE_COLS, n_cols - c)
            src = src_hbm.at[0, pl.ds(r, STAGE_ROWS), pl.ds(src_col0 + c, w)]

            def store(tile, r=r, c=c, w=w):
                out = (fix(tile) if fix else tile).astype(BF16)
                cols = slice(dst_col0 + c, dst_col0 + c + w)
                if dst_row is None:
                    dst_ref[r:r + STAGE_ROWS, cols] = out
                else:
                    for b in range(0, STAGE_ROWS, HEAD_DIM):
                        dst_ref[dst_row(r + b):dst_row(r + b) + HEAD_DIM, cols] = out[b:b + HEAD_DIM]

            jobs.append((src, w, store))
    return jobs


def _start_stream(jobs, stage_ref, sem_ref):
    def copy(n):
        src, w, _ = jobs[n]
        slot = n % STAGE_SLOTS
        dst = stage_ref.at[slot] if w == STAGE_COLS else stage_ref.at[slot, :, pl.ds(0, w)]
        return pltpu.make_async_copy(src, dst, sem_ref.at[slot])

    ahead = STAGE_SLOTS - 1
    for n in range(min(ahead, len(jobs))):
        copy(n).start()
    done = [0]

    def consume(count):
        for n in range(done[0], done[0] + count):
            if n + ahead < len(jobs):
                copy(n + ahead).start()
            copy(n).wait()
            jobs[n][2](stage_ref[n % STAGE_SLOTS, :, 0:jobs[n][1]])
        done[0] += count
        assert done[0] <= len(jobs)

    return consume


def _ffn_stream_jobs(w_in_hbm, w_in_ref, w_out_hbm, w_out_ref, extra_jobs):
    out_jobs = _window_jobs(w_out_hbm, w_out_ref, D_FF, D_MODEL)
    jobs, counts = [], []
    for c in range(N_FFN_CHUNKS):
        mine = []
        if c % 2 == 0:
            col0 = c * FFN_CHUNK
            n_cols = min(STAGE_COLS, D_FF - col0)
            mine += _window_jobs(w_in_hbm, w_in_ref, D_MODEL, n_cols, col0, col0)
            mine += _window_jobs(w_in_hbm, w_in_ref, D_MODEL, n_cols, D_FF + col0, D_FF + col0)
        for pool in (out_jobs, extra_jobs):
            share = -(-len(pool) // (N_FFN_CHUNKS - c))
            mine += [pool.pop(0) for _ in range(share)]
        jobs += mine
        counts.append(len(mine))
    assert not out_jobs and not extra_jobs
    return jobs, counts


def _to_slot_order(q):
    lane = lax.broadcasted_iota(jnp.int32, (q.shape[0], 128), 1)
    t = [q[:, i * 128:(i + 1) * 128] for i in range(4)]
    swap = lambda a: pltpu.roll(a, 64, 1)
    return jnp.concatenate([jnp.where(lane < 64, t[0], swap(t[2])), jnp.where(lane < 64, swap(t[0]), t[2]),
                            jnp.where(lane < 64, t[1], swap(t[3])), jnp.where(lane < 64, swap(t[1]), t[3])],
                           axis=1)


def _ffn_qkv_kernel(x_ref, w_in_hbm, w_out_hbm, g_ref, b_ref, w_mix_hbm,
                    x1_ref, qa_ref, ka_ref, va_ref, qb_ref, kb_ref, vb_ref,
                    act_ref, w_in_ref, w_out_ref, wqkv_ref, stage_ref, sem_ref):
    def tile(before_chunk=None):
        x = x_ref[...]
        f = _swiglu(x.astype(BF16), w_in_ref, w_out_ref, act_ref, before_chunk)
        x1 = _layer_norm(ALPHA * x + 0.5 * f, g_ref[...], b_ref[...])
        x1_ref[...] = x1
        h = jnp.dot(x1.astype(BF16), wqkv_ref[...], preferred_element_type=F32)
        qa_ref[...] = (h[:, 0:512] * Q_SCALE).astype(BF16)
        ka_ref[...] = h[:, 512:640].astype(BF16)
        va_ref[...] = h[:, 640:768].astype(BF16)
        qb_ref[...] = (h[:, 768:1280] * Q_SCALE).astype(BF16)
        kb_ref[...] = h[:, 1280:1408].astype(BF16)
        vb_ref[...] = h[:, 1408:1536].astype(BF16)

    @pl.when(pl.program_id(0) == 0)
    def _():
        qkv_jobs = []
        for col0, n_cols, fix in ((0, Q_COLS, _to_slot_order), (Q_COLS, 2 * KV_COLS, None),
                                  (QKV_COLS // 2, Q_COLS, _to_slot_order),
                                  (QKV_COLS // 2 + Q_COLS, 2 * KV_COLS, None)):
            qkv_jobs += _window_jobs(w_mix_hbm, wqkv_ref, D_MODEL, n_cols, col0, col0, fix)
        jobs, counts = _ffn_stream_jobs(w_in_hbm, w_in_ref, w_out_hbm, w_out_ref, qkv_jobs)
        consume = _start_stream(jobs, stage_ref, sem_ref)
        tile(lambda c: consume(counts[c]))

    @pl.when(pl.program_id(0) != 0)
    def _():
        tile()


def _ffn_qkv(x, w_in, w_out, g, b, w_mix):
    tm = 2 * TOKEN_TILE
    row = lambda n: pl.BlockSpec((tm, n), lambda i: (i, 0))
    hbm = pl.BlockSpec(memory_space=pl.ANY)
    out_cols = (D_MODEL, Q_COLS, KV_COLS, KV_COLS, Q_COLS, KV_COLS, KV_COLS)
    out_dtypes = (F32,) + (BF16,) * 6
    return pl.pallas_call(
        _ffn_qkv_kernel,
        grid=(TOKENS // tm,),
        in_specs=[row(D_MODEL), hbm, hbm, _const_spec(g.shape), _const_spec(b.shape), hbm],
        out_specs=[row(n) for n in out_cols],
        out_shape=[jax.ShapeDtypeStruct((TOKENS, n), dt) for n, dt in zip(out_cols, out_dtypes)],
        scratch_shapes=[pltpu.VMEM((tm, D_FF), BF16),
                        pltpu.VMEM((D_MODEL, 2 * D_FF), BF16),
                        pltpu.VMEM((D_FF, D_MODEL), BF16),
                        pltpu.VMEM((D_MODEL, QKV_COLS), BF16),
                        pltpu.VMEM(STAGE_SHAPE, F32),
                        pltpu.SemaphoreType.DMA((STAGE_SLOTS,))],
        compiler_params=pltpu.CompilerParams(
            dimension_semantics=("arbitrary",), vmem_limit_bytes=V7X_VMEM_LIMIT_BYTES),
        name="ffn_qkv",
    )(x, w_in, w_out, g, b, w_mix)


SWA_ROW_SLOTS = (0, 2, 4, 6, 1, 3, 5, 7)


def _swa_kernel(sink_ref, q_ref, kp_ref, kc_ref, vp_ref, vc_ref, o_ref, qconst_ref, bias_ref):
    i = pl.program_id(1)
    w = SWA_WINDOW
    rows = N_HEADS * w
    lane = lax.broadcasted_iota(jnp.int32, (w, 128), 1)

    @pl.when((pl.program_id(0) == 0) & (i == 0))
    def _():
        rowf = lax.broadcasted_iota(jnp.int32, (w, 128), 0).astype(F32)
        for rb, slot in enumerate(SWA_ROW_SLOTS):
            slope = jnp.full((w, 128), ALIBI_SLOPES[HEAD_OF_SLOT[slot]], F32)
            qc = jnp.zeros((w, 128), F32)
            for first, terms in ((ROW_LANE, _split3(-slope * (rowf + w))), (COL_LANE, _split3(slope))):
                for n, term in enumerate(terms):
                    qc = jnp.where(lane == first + n, term.astype(F32), qc)
            qconst_ref[rb * w:(rb + 1) * w, :] = qc.astype(BF16)
        dist = (lax.broadcasted_iota(jnp.int32, (w, 2 * w), 0) + w
                - lax.broadcasted_iota(jnp.int32, (w, 2 * w), 1))
        bias_ref[...] = jnp.where((dist >= 0) & (dist < w), 0.0, NEG)

    k_all = jnp.concatenate([kp_ref[...], kc_ref[...]], axis=0)
    v_all = jnp.concatenate([vp_ref[...], vc_ref[...]], axis=0)
    key_col = lax.broadcasted_iota(jnp.int32, (2 * w, 128), 0).astype(F32)
    lane2 = lax.broadcasted_iota(jnp.int32, (2 * w, 128), 1)
    k_extra = jnp.where((lane2 >= ROW_LANE) & (lane2 < ROW_LANE + 3), 1.0,
                        jnp.where((lane2 >= COL_LANE) & (lane2 < COL_LANE + 3), key_col, 0.0)).astype(BF16)
    sink = jnp.concatenate([jnp.full((w, 128), sink_ref[HEAD_OF_SLOT[slot]] * LOG2E, F32)
                            for slot in SWA_ROW_SLOTS], axis=0)
    ones = jnp.ones((2 * w, 128), BF16)
    bias = bias_ref[...]
    kcol = lax.broadcasted_iota(jnp.int32, (w, 2 * w), 1)
    bias_first = jnp.where((kcol >= w) | (i > 0), bias, NEG)

    for r in range(SWA_TILE // w):
        q = q_ref[r * w:(r + 1) * w, :]
        q_aug = jnp.concatenate(
            [jnp.concatenate([q[:, (slot // 2) * 128:(slot // 2 + 1) * 128],
                              qconst_ref[rb * w:(rb + 1) * w, :]], axis=1)
             for rb, slot in enumerate(SWA_ROW_SLOTS)], axis=0)
        k = k_all[r * w:(r + 2) * w, :]
        zero = jnp.zeros_like(k)
        nt = (((1,), (1,)), ((), ()))
        s0 = lax.dot_general(q_aug[:rows // 2], jnp.concatenate([jnp.where(lane2 < 64, k, zero), k_extra], axis=1),
                             nt, preferred_element_type=F32)
        s1 = lax.dot_general(q_aug[rows // 2:], jnp.concatenate([jnp.where(lane2 >= 64, k, zero), k_extra], axis=1),
                             nt, preferred_element_type=F32)
        b = bias_first if r == 0 else bias
        s = (jnp.concatenate([s0, s1], axis=0).reshape(N_HEADS, w, 2 * w) + b[None]).reshape(rows, 2 * w)
        row_max = jnp.max(jnp.maximum(s[:, :128], s[:, 128:]), axis=-1, keepdims=True)
        m = jnp.maximum(jnp.broadcast_to(row_max, (rows, 128)), sink)
        p = jnp.exp2(s - jnp.concatenate([m, m], axis=1)).astype(BF16)
        v_aug = jnp.concatenate([v_all[r * w:(r + 2) * w, :], ones], axis=1)
        pv = jnp.dot(p, v_aug, preferred_element_type=F32)
        out = pv[:, :128] / (pv[:, 128:] + jnp.exp2(sink - m))
        tiles = []
        for pair in range(4):
            lo = out[pair * w:(pair + 1) * w, :]
            hi = out[(4 + pair) * w:(5 + pair) * w, :]
            tiles.append(jnp.where(lane < 64, lo, hi))
        o_ref[r * w:(r + 1) * w, :] = jnp.concatenate(tiles, axis=1).astype(BF16)


def _swa(sinks, qa, ka, va):
    w = SWA_WINDOW
    per_tile = SWA_TILE // w
    cur = lambda b, i: (b, i, 0)
    prev = lambda b, i: (b, jnp.maximum(i * per_tile - 1, 0), 0)
    return pl.pallas_call(
        _swa_kernel,
        grid=(BATCH, SEQ // SWA_TILE),
        in_specs=[pl.BlockSpec(memory_space=pltpu.SMEM),
                  pl.BlockSpec((None, SWA_TILE, Q_COLS), cur),
                  pl.BlockSpec((None, w, KV_COLS), prev),
                  pl.BlockSpec((None, SWA_TILE, KV_COLS), cur),
                  pl.BlockSpec((None, w, KV_COLS), prev),
                  pl.BlockSpec((None, SWA_TILE, KV_COLS), cur)],
        out_specs=pl.BlockSpec((None, SWA_TILE, Q_COLS), cur),
        out_shape=jax.ShapeDtypeStruct((BATCH, SEQ, Q_COLS), BF16),
        scratch_shapes=[pltpu.VMEM((N_HEADS * w, 128), BF16),
                        pltpu.VMEM((w, 2 * w), F32)],
        compiler_params=pltpu.CompilerParams(dimension_semantics=("arbitrary", "arbitrary")),
        name="swa",
    )(sinks, qa, ka, ka, va, va)


def _moba_kernel(q_ref, k_ref, v_ref, o_ref,
                 selb_ref, qconst_ref, qaug_ref, u_ref, mpart_ref, acc_ref):
    step = pl.program_id(1)
    bl = MOBA_BLOCK
    nbk = N_MOBA_BLOCKS
    rows = N_HEADS * bl
    lane = lax.broadcasted_iota(jnp.int32, (bl, 128), 1)
    lane_blk = lane % nbk
    lane_slot = lane // nbk
    sel_lane = lane < SEL_LANES
    in3 = lambda first: (lane >= first) & (lane < first + 3)

    def lanes3(first, terms):
        out = jnp.zeros((bl, 128), F32)
        for i, term in enumerate(terms):
            out = jnp.where(lane == first + i, term.astype(F32), out)
        return out

    @pl.when((pl.program_id(0) == 0) & (step == 0))
    def _():
        rowf = lax.broadcasted_iota(jnp.int32, (bl, 128), 0).astype(F32)
        for slot in range(N_HEADS):
            slope = jnp.full((bl, 128), ALIBI_SLOPES[N_HEADS + HEAD_OF_SLOT[slot]], F32)
            qc = (lanes3(ROW_LANE, _split3(-slope * rowf)) + lanes3(COL_LANE, _split3(slope))
                  + lanes3(BLK_LANE, _split3(-slope)))
            qconst_ref[slot * bl:(slot + 1) * bl, :] = qc.astype(BF16)

    @pl.when(step == 0)
    def _():
        rsel = lax.broadcasted_iota(jnp.int32, (128, SEQ), 0)
        tsel = lax.broadcasted_iota(jnp.int32, (128, SEQ), 1)
        avg = jnp.where((rsel % nbk) == (tsel // bl), 1.0 / bl, 0.0).astype(BF16)
        kmean_rows = jnp.dot(avg, k_ref[...], preferred_element_type=F32)
        kmean_t = kmean_rows.T
        r128 = lax.broadcasted_iota(jnp.int32, (128, 128), 0)
        c128 = lax.broadcasted_iota(jnp.int32, (128, 128), 1)
        km = jnp.concatenate(
            [jnp.where((c128 // nbk) == 2 * pair + (r128 >= 64).astype(jnp.int32), kmean_t, 0.0)
             for pair in range(4)], axis=0)
        q_all = q_ref[...]
        gate = sum(jnp.dot(q_all, term, preferred_element_type=F32) for term in _split3(km)[:2])
        gate_t = gate.T[:SEL_LANES].reshape(N_HEADS, nbk, SEQ)
        blk = lax.broadcasted_iota(jnp.int32, (N_HEADS, nbk, SEQ), 1)
        own = lax.broadcasted_iota(jnp.int32, (N_HEADS, nbk, SEQ), 2) // bl
        past = blk < own
        gm = jnp.where(past, gate_t, -jnp.inf)
        rank = jnp.zeros((N_HEADS, nbk, SEQ), jnp.int32)
        for d in range(1, nbk):
            partner = pltpu.roll(gm, nbk - d, 1)
            wrapped = blk + d >= nbk
            beats = (partner > gm) | ((partner == gm) & wrapped)
            rank = rank + beats.astype(jnp.int32)
        selected = past & (rank < MOBA_TOPK)
        selb_t = jnp.where(selected, 0.0, NEG).reshape(SEL_LANES, SEQ)
        selb_ref[...] = jnp.concatenate([selb_t, jnp.zeros_like(selb_t)], axis=0).T

    def fold(s):
        return jnp.maximum(s[:, :128], s[:, 128:])

    def run_step(t_a):
        tiles_t = (t_a, t_a + 1)
        past = [(side, j) for j in range(t_a + 1) for side in range(2) if j < tiles_t[side]]

        for side, t in enumerate(tiles_t):
            q = q_ref[t * bl:(t + 1) * bl, :]
            selb = selb_ref[t * bl:(t + 1) * bl, :]
            for slot in range(N_HEADS):
                pair, half = divmod(slot, 2)
                qp = q[:, pair * 128:(pair + 1) * 128]
                keep = (lane < 64) if half == 0 else (lane >= 64)
                qm = jnp.where(keep, qp, jnp.zeros_like(qp))
                extra = jnp.where(sel_lane & (lane_slot == slot), selb,
                                  qconst_ref[slot * bl:(slot + 1) * bl, :].astype(F32))
                qaug_ref[side, slot * bl:(slot + 1) * bl, :] = jnp.concatenate(
                    [qm, extra.astype(BF16)], axis=1)

        key_col = lax.broadcasted_iota(jnp.int32, (bl, 128), 0).astype(F32)
        alibi_lanes = jnp.where(in3(ROW_LANE), 1.0, jnp.where(in3(COL_LANE), key_col, 0.0))

        def scores(side, j):
            t = tiles_t[side]
            if j == t:
                extra = alibi_lanes
            else:
                extra = jnp.where(sel_lane, jnp.where(lane_blk == j, 1.0, 0.0),
                                  jnp.where(in3(BLK_LANE), float((t - j) * bl), alibi_lanes))
            kaug = jnp.concatenate([k_ref[j * bl:(j + 1) * bl, :], extra.astype(BF16)], axis=1)
            return lax.dot_general(qaug_ref[side], kaug, (((1,), (1,)), ((), ())),
                                   preferred_element_type=F32)

        def probs_v(side, blocks):
            m = mpart_ref[side]
            pb = jnp.concatenate([jnp.exp2(u_ref[n, :, c:c + 128] - m) for n, _ in blocks for c in (0, 128)],
                                 axis=1).astype(BF16)
            vj = jnp.concatenate([v_ref[j * bl:(j + 1) * bl, :] for _, j in blocks], axis=0)
            return jnp.dot(pb, jnp.concatenate([vj, jnp.ones_like(vj)], axis=1),
                           preferred_element_type=F32)

        qrow = lax.broadcasted_iota(jnp.int32, (rows, bl), 0) & (bl - 1)
        kcol = lax.broadcasted_iota(jnp.int32, (rows, bl), 1)
        for side, t in enumerate(tiles_t):
            s_own = jnp.where(qrow >= kcol, scores(side, t), NEG)
            u_ref[side] = s_own
            mpart_ref[side] = fold(s_own)
        for n, (side, j) in enumerate(past):
            sj = scores(side, j)
            u_ref[2 + n] = sj
            mpart_ref[side] = jnp.maximum(mpart_ref[side], fold(sj))

        for side in range(2):
            mpart_ref[side] = jnp.broadcast_to(jnp.max(mpart_ref[side], axis=-1, keepdims=True),
                                               (rows, 128))

        stored = [[(side, tiles_t[side])] + [(2 + n, j) for n, (s, j) in enumerate(past) if s == side]
                  for side in range(2)]
        pairs = [[blocks[i:i + 2] for i in range(0, len(blocks), 2)] for blocks in stored]
        for g in range(max(len(p) for p in pairs)):
            for side in range(2):
                if g < len(pairs[side]):
                    if g == 0:
                        acc_ref[side] = probs_v(side, pairs[side][g])
                    else:
                        acc_ref[side] += probs_v(side, pairs[side][g])

        for side, t in enumerate(tiles_t):
            out = acc_ref[side, :, :128] / acc_ref[side, :, 128:]
            heads = []
            for pair in range(4):
                lo = out[(2 * pair) * bl:(2 * pair + 1) * bl, :]
                hi = out[(2 * pair + 1) * bl:(2 * pair + 2) * bl, :]
                heads.append(jnp.where(lane < 64, lo, hi))
            o_ref[t * bl:(t + 1) * bl, :] = jnp.concatenate(heads, axis=1).astype(BF16)

    for pair_step in range(N_MOBA_BLOCKS // 2):
        pl.when(step == pair_step)(functools.partial(run_step, 2 * pair_step))


def _moba(qb, kb, vb):
    bl = MOBA_BLOCK
    rows = N_HEADS * bl
    step_blocks = 2 * N_MOBA_BLOCKS - 1
    whole = lambda b, t: (b, 0, 0)
    return pl.pallas_call(
        _moba_kernel,
        grid=(BATCH, N_MOBA_BLOCKS // 2),
        in_specs=[pl.BlockSpec((None, SEQ, Q_COLS), whole),
                  pl.BlockSpec((None, SEQ, KV_COLS), whole),
                  pl.BlockSpec((None, SEQ, KV_COLS), whole)],
        out_specs=pl.BlockSpec((None, SEQ, Q_COLS), whole),
        out_shape=jax.ShapeDtypeStruct((BATCH, SEQ, Q_COLS), BF16),
        scratch_shapes=[pltpu.VMEM((SEQ, 128), F32),
                        pltpu.VMEM((rows, 128), BF16),
                        pltpu.VMEM((2, rows, 256), BF16),
                        pltpu.VMEM((step_blocks, rows, bl), F32),
                        pltpu.VMEM((2, rows, 128), F32),
                        pltpu.VMEM((2, rows, 256), F32)],
        compiler_params=pltpu.CompilerParams(
            dimension_semantics=("arbitrary", "arbitrary"), vmem_limit_bytes=V7X_VMEM_LIMIT_BYTES),
        name="moba",
    )(qb, kb, vb)


def _mix_ffn_kernel(x1_ref, ya_ref, yb_ref, w_mix_hbm, wa_hbm, wb_hbm, wo_hbm, g2_ref, b2_ref,
                    w_in_hbm, w_out_hbm, g3_ref, b3_ref, o_ref,
                    act_ref, wg_ref, wa_ref, wb_ref, wo_ref, w_in_ref, w_out_ref, stage_ref, sem_ref):
    @pl.when(pl.program_id(0) == 0)
    def _():
        jobs = _window_jobs(w_mix_hbm, wg_ref, D_MODEL, 2 * D_MODEL, QKV_COLS)
        slot_row = lambda r: HEAD_OF_SLOT.index(r // HEAD_DIM) * HEAD_DIM
        jobs += _window_jobs(wa_hbm, wa_ref, Q_COLS, D_MODEL, dst_row=slot_row)
        jobs += _window_jobs(wb_hbm, wb_ref, Q_COLS, D_MODEL, dst_row=slot_row)
        jobs += _window_jobs(wo_hbm, wo_ref, D_MODEL, D_MODEL)
        jobs += _window_jobs(w_in_hbm, w_in_ref, D_MODEL, 2 * D_FF)
        jobs += _window_jobs(w_out_hbm, w_out_ref, D_FF, D_MODEL)
        _start_stream(jobs, stage_ref, sem_ref)(len(jobs))

    x1 = x1_ref[...]
    x1b = x1.astype(BF16)
    ga = jnp.dot(x1b, wg_ref[:, :D_MODEL], preferred_element_type=F32)
    ya = jnp.dot(ya_ref[...], wa_ref[...], preferred_element_type=F32)
    y = jax.nn.sigmoid(ga) * ya
    gb = jnp.dot(x1b, wg_ref[:, D_MODEL:], preferred_element_type=F32)
    yb = jnp.dot(yb_ref[...], wb_ref[...], preferred_element_type=F32)
    y = y + jax.nn.sigmoid(gb) * yb
    z = jnp.dot(y.astype(BF16), wo_ref[...], preferred_element_type=F32)
    x2 = _layer_norm(ALPHA * x1 + z, g2_ref[...], b2_ref[...])
    f = _swiglu(x2.astype(BF16), w_in_ref, w_out_ref, act_ref)
    o_ref[...] = _layer_norm(ALPHA * x2 + 0.5 * f, g3_ref[...], b3_ref[...])


def _mix_ffn(x1, ya, yb, w_mix, wa, wb, wo, g2, b2, w_in, w_out, g3, b3):
    tm = 2 * TOKEN_TILE
    row = lambda n: pl.BlockSpec((tm, n), lambda i: (i, 0))
    hbm = pl.BlockSpec(memory_space=pl.ANY)
    ln = _const_spec(g2.shape)
    return pl.pallas_call(
        _mix_ffn_kernel,
        grid=(TOKENS // tm,),
        in_specs=[row(D_MODEL), row(Q_COLS), row(Q_COLS), hbm, hbm, hbm, hbm, ln, ln, hbm, hbm, ln, ln],
        out_specs=row(D_MODEL),
        out_shape=jax.ShapeDtypeStruct((TOKENS, D_MODEL), F32),
        scratch_shapes=[pltpu.VMEM((tm, D_FF), BF16),
                        pltpu.VMEM((D_MODEL, 2 * D_MODEL), BF16),
                        pltpu.VMEM((Q_COLS, D_MODEL), BF16),
                        pltpu.VMEM((Q_COLS, D_MODEL), BF16),
                        pltpu.VMEM((D_MODEL, D_MODEL), BF16),
                        pltpu.VMEM((D_MODEL, 2 * D_FF), BF16),
                        pltpu.VMEM((D_FF, D_MODEL), BF16),
                        pltpu.VMEM(STAGE_SHAPE, F32),
                        pltpu.SemaphoreType.DMA((STAGE_SLOTS,))],
        compiler_params=pltpu.CompilerParams(
            dimension_semantics=("arbitrary",), vmem_limit_bytes=V7X_VMEM_LIMIT_BYTES),
        name="mix_ffn",
    )(x1, ya, yb, w_mix, wa, wb, wo, g2, b2, w_in, w_out, g3, b3)


def kernel(x, ffn1_w_in, ffn1_w_out, ln1_g, ln1_b, mix_w_in, swa_sinks, w_branch_a, w_branch_b,
           mix_w_o, ln2_g, ln2_b, ffn2_w_in, ffn2_w_out, ln3_g, ln3_b):
    assert x.shape == (BATCH, SEQ, D_MODEL) and ffn1_w_in.shape == (1, D_MODEL, 2 * D_FF)
    assert mix_w_in.shape == (1, D_MODEL, MIX_IN_COLS)
    x1, qa, ka, va, qb, kb, vb = _ffn_qkv(x.reshape(TOKENS, D_MODEL), ffn1_w_in, ffn1_w_out,
                                          ln1_g, ln1_b, mix_w_in)
    seq3 = lambda a: a.reshape(BATCH, SEQ, a.shape[-1])
    ya = _swa(swa_sinks[0], seq3(qa), seq3(ka), seq3(va))
    yb = _moba(seq3(qb), seq3(kb), seq3(vb))
    out = _mix_ffn(x1, ya.reshape(TOKENS, Q_COLS), yb.reshape(TOKENS, Q_COLS), mix_w_in,
                   w_branch_a, w_branch_b, mix_w_o, ln2_g, ln2_b, ffn2_w_in, ffn2_w_out, ln3_g, ln3_b)
    return out.reshape(BATCH, SEQ, D_MODEL)
```

```python
import functools

import jax
import jax.numpy as jnp
from jax import lax
from jax.experimental import pallas as pl
from jax.experimental.pallas import tpu as pltpu

F32 = jnp.float32
BF16 = jnp.bfloat16

D_MODEL = 1024
BATCH = 8
SEQ = 2048
TOKENS = BATCH * SEQ
HEAD_DIM = 64
N_HEADS = 8
KV_COLS = 128
Q_COLS = 512
QKV_COLS = 2 * (Q_COLS + 2 * KV_COLS)
MIX_IN_COLS = QKV_COLS + 2 * D_MODEL
SWA_WINDOW = 128
SWA_TILE = 512
MOBA_BLOCK = 256
MOBA_TOPK = 3
N_MOBA_BLOCKS = SEQ // MOBA_BLOCK
PV_GROUP = 2
D_FF = 2816
FFN_CHUNK = 256
N_FFN_CHUNKS = D_FF // FFN_CHUNK
ALPHA = 2.0 ** 0.25
LN_EPS = 1e-5
NEG = -1e30
LOG2E = 1.4426950408889634
Q_SCALE = 0.125 * LOG2E
ALIBI_SLOPES = tuple(float(2.0 ** (-8.0 * i / 16.0)) * LOG2E for i in range(1, 17))

HEAD_OF_SLOT = (0, 4, 1, 5, 2, 6, 3, 7)

V7X_VMEM_LIMIT_BYTES = 60 * 1024 * 1024
TOKEN_TILE = 512

SEL_LANES = 64
ROW_LANE = 64
COL_LANE = 67
BLK_LANE = 70


def _const_spec(shape):
    zeros = (0,) * len(shape)
    return pl.BlockSpec(shape, lambda *_: zeros, pipeline_mode=pl.Buffered(1))


def _split3(x):
    hi = x.astype(BF16)
    r1 = x - hi.astype(F32)
    mid = r1.astype(BF16)
    lo = (r1 - mid.astype(F32)).astype(BF16)
    return hi, mid, lo


def _layer_norm(y, g, b):
    mu = jnp.mean(y, axis=-1, keepdims=True)
    yc = y - mu
    var = jnp.mean(yc * yc, axis=-1, keepdims=True)
    return yc * lax.rsqrt(var + LN_EPS) * g + b


def _swiglu(xb, w_in_ref, w_out_ref, g_ref):
    for c in range(N_FFN_CHUNKS):
        lo = c * FFN_CHUNK
        a = jnp.dot(xb, w_in_ref[:, lo:lo + FFN_CHUNK], preferred_element_type=F32)
        u = jnp.dot(xb, w_in_ref[:, D_FF + lo:D_FF + lo + FFN_CHUNK], preferred_element_type=F32)
        g_ref[:, lo:lo + FFN_CHUNK] = (a * jax.nn.sigmoid(a) * u).astype(BF16)
    return jnp.dot(g_ref[...], w_out_ref[...], preferred_element_type=F32)


STAGE_ROWS = 256
STAGE_COLS = 512
STAGE_SLOTS = 6
STAGE_SHAPE = (STAGE_SLOTS, STAGE_ROWS, STAGE_COLS)


def _window_jobs(src_hbm, dst_ref, n_rows, n_cols, src_col0=0, dst_col0=0, fix=None, dst_row=None):
    jobs = []
    for r in range(0, n_rows, STAGE_ROWS):
        for c in range(0, n_cols, STAGE_COLS):
            w = min(STAGE_COLS, n_cols - c)
            src = src_hbm.at[0, pl.ds(r, STAGE_ROWS), pl.ds(src_col0 + c, w)]

            def store(tile, r=r, c=c, w=w):
                out = (fix(tile) if fix else tile).astype(BF16)
                cols = slice(dst_col0 + c, dst_col0 + c + w)
                if dst_row is None:
                    dst_ref[r:r + STAGE_ROWS, cols] = out
                else:
                    for b in range(0, STAGE_ROWS, HEAD_DIM):
                        dst_ref[dst_row(r + b):dst_row(r + b) + HEAD_DIM, cols] = out[b:b + HEAD_DIM]

            jobs.append((src, w, store))
    return jobs


def _stream_cast(jobs, stage_ref, sem_ref):
    def copy(n):
        src, w, _ = jobs[n]
        slot = n % STAGE_SLOTS
        dst = stage_ref.at[slot] if w == STAGE_COLS else stage_ref.at[slot, :, pl.ds(0, w)]
        return pltpu.make_async_copy(src, dst, sem_ref.at[slot])

    ahead = STAGE_SLOTS - 1
    for n in range(min(ahead, len(jobs))):
        copy(n).start()
    for n, (_, w, store) in enumerate(jobs):
        if n + ahead < len(jobs):
            copy(n + ahead).start()
        copy(n).wait()
        store(stage_ref[n % STAGE_SLOTS, :, 0:w])


def _to_slot_order(q):
    lane = lax.broadcasted_iota(jnp.int32, (q.shape[0], 128), 1)
    t = [q[:, i * 128:(i + 1) * 128] for i in range(4)]
    swap = lambda a: pltpu.roll(a, 64, 1)
    return jnp.concatenate([jnp.where(lane < 64, t[0], swap(t[2])), jnp.where(lane < 64, swap(t[0]), t[2]),
                            jnp.where(lane < 64, t[1], swap(t[3])), jnp.where(lane < 64, swap(t[1]), t[3])],
                           axis=1)


def _ffn_qkv_kernel(x_ref, w_in_hbm, w_out_hbm, g_ref, b_ref, w_mix_hbm,
                    x1_ref, qa_ref, ka_ref, va_ref, qb_ref, kb_ref, vb_ref,
                    act_ref, w_in_ref, w_out_ref, wqkv_ref, stage_ref, sem_ref):
    @pl.when(pl.program_id(0) == 0)
    def _():
        jobs = _window_jobs(w_in_hbm, w_in_ref, D_MODEL, 2 * D_FF)
        jobs += _window_jobs(w_out_hbm, w_out_ref, D_FF, D_MODEL)
        for col0, n_cols, fix in ((0, Q_COLS, _to_slot_order), (Q_COLS, 2 * KV_COLS, None),
                                  (QKV_COLS // 2, Q_COLS, _to_slot_order),
                                  (QKV_COLS // 2 + Q_COLS, 2 * KV_COLS, None)):
            jobs += _window_jobs(w_mix_hbm, wqkv_ref, D_MODEL, n_cols, col0, col0, fix)
        _stream_cast(jobs, stage_ref, sem_ref)

    x = x_ref[...]
    f = _swiglu(x.astype(BF16), w_in_ref, w_out_ref, act_ref)
    x1 = _layer_norm(ALPHA * x + 0.5 * f, g_ref[...], b_ref[...])
    x1_ref[...] = x1
    h = jnp.dot(x1.astype(BF16), wqkv_ref[...], preferred_element_type=F32)
    qa_ref[...] = (h[:, 0:512] * Q_SCALE).astype(BF16)
    ka_ref[...] = h[:, 512:640].astype(BF16)
    va_ref[...] = h[:, 640:768].astype(BF16)
    qb_ref[...] = (h[:, 768:1280] * Q_SCALE).astype(BF16)
    kb_ref[...] = h[:, 1280:1408].astype(BF16)
    vb_ref[...] = h[:, 1408:1536].astype(BF16)


def _ffn_qkv(x, w_in, w_out, g, b, w_mix):
    tm = 2 * TOKEN_TILE
    row = lambda n: pl.BlockSpec((tm, n), lambda i: (i, 0))
    hbm = pl.BlockSpec(memory_space=pl.ANY)
    out_cols = (D_MODEL, Q_COLS, KV_COLS, KV_COLS, Q_COLS, KV_COLS, KV_COLS)
    out_dtypes = (F32,) + (BF16,) * 6
    return pl.pallas_call(
        _ffn_qkv_kernel,
        grid=(TOKENS // tm,),
        in_specs=[row(D_MODEL), hbm, hbm, _const_spec(g.shape), _const_spec(b.shape), hbm],
        out_specs=[row(n) for n in out_cols],
        out_shape=[jax.ShapeDtypeStruct((TOKENS, n), dt) for n, dt in zip(out_cols, out_dtypes)],
        scratch_shapes=[pltpu.VMEM((tm, D_FF), BF16),
                        pltpu.VMEM((D_MODEL, 2 * D_FF), BF16),
                        pltpu.VMEM((D_FF, D_MODEL), BF16),
                        pltpu.VMEM((D_MODEL, QKV_COLS), BF16),
                        pltpu.VMEM(STAGE_SHAPE, F32),
                        pltpu.SemaphoreType.DMA((STAGE_SLOTS,))],
        compiler_params=pltpu.CompilerParams(
            dimension_semantics=("arbitrary",), vmem_limit_bytes=V7X_VMEM_LIMIT_BYTES),
        name="ffn_qkv",
    )(x, w_in, w_out, g, b, w_mix)


SWA_ROW_SLOTS = (0, 2, 4, 6, 1, 3, 5, 7)


def _swa_kernel(sink_ref, q_ref, kp_ref, kc_ref, vp_ref, vc_ref, o_ref, qconst_ref, bias_ref):
    i = pl.program_id(1)
    w = SWA_WINDOW
    rows = N_HEADS * w
    lane = lax.broadcasted_iota(jnp.int32, (w, 128), 1)

    @pl.when((pl.program_id(0) == 0) & (i == 0))
    def _():
        rowf = lax.broadcasted_iota(jnp.int32, (w, 128), 0).astype(F32)
        for rb, slot in enumerate(SWA_ROW_SLOTS):
            slope = jnp.full((w, 128), ALIBI_SLOPES[HEAD_OF_SLOT[slot]], F32)
            qc = jnp.zeros((w, 128), F32)
            for first, terms in ((ROW_LANE, _split3(-slope * (rowf + w))), (COL_LANE, _split3(slope))):
                for n, term in enumerate(terms):
                    qc = jnp.where(lane == first + n, term.astype(F32), qc)
            qconst_ref[rb * w:(rb + 1) * w, :] = qc.astype(BF16)
        dist = (lax.broadcasted_iota(jnp.int32, (w, 2 * w), 0) + w
                - lax.broadcasted_iota(jnp.int32, (w, 2 * w), 1))
        bias_ref[...] = jnp.where((dist >= 0) & (dist < w), 0.0, NEG)

    k_all = jnp.concatenate([kp_ref[...], kc_ref[...]], axis=0)
    v_all = jnp.concatenate([vp_ref[...], vc_ref[...]], axis=0)
    key_col = lax.broadcasted_iota(jnp.int32, (2 * w, 128), 0).astype(F32)
    lane2 = lax.broadcasted_iota(jnp.int32, (2 * w, 128), 1)
    k_extra = jnp.where((lane2 >= ROW_LANE) & (lane2 < ROW_LANE + 3), 1.0,
                        jnp.where((lane2 >= COL_LANE) & (lane2 < COL_LANE + 3), key_col, 0.0)).astype(BF16)
    sink = jnp.concatenate([jnp.full((w, 128), sink_ref[HEAD_OF_SLOT[slot]] * LOG2E, F32)
                            for slot in SWA_ROW_SLOTS], axis=0)
    ones = jnp.ones((2 * w, 128), BF16)
    bias = bias_ref[...]
    kcol = lax.broadcasted_iota(jnp.int32, (w, 2 * w), 1)
    bias_first = jnp.where((kcol >= w) | (i > 0), bias, NEG)

    for r in range(SWA_TILE // w):
        q = q_ref[r * w:(r + 1) * w, :]
        q_aug = jnp.concatenate(
            [jnp.concatenate([q[:, (slot // 2) * 128:(slot // 2 + 1) * 128],
                              qconst_ref[rb * w:(rb + 1) * w, :]], axis=1)
             for rb, slot in enumerate(SWA_ROW_SLOTS)], axis=0)
        k = k_all[r * w:(r + 2) * w, :]
        zero = jnp.zeros_like(k)
        nt = (((1,), (1,)), ((), ()))
        s0 = lax.dot_general(q_aug[:rows // 2], jnp.concatenate([jnp.where(lane2 < 64, k, zero), k_extra], axis=1),
                             nt, preferred_element_type=F32)
        s1 = lax.dot_general(q_aug[rows // 2:], jnp.concatenate([jnp.where(lane2 >= 64, k, zero), k_extra], axis=1),
                             nt, preferred_element_type=F32)
        b = bias_first if r == 0 else bias
        s = (jnp.concatenate([s0, s1], axis=0).reshape(N_HEADS, w, 2 * w) + b[None]).reshape(rows, 2 * w)
        row_max = jnp.max(jnp.maximum(s[:, :128], s[:, 128:]), axis=-1, keepdims=True)
        m = jnp.maximum(jnp.broadcast_to(row_max, (rows, 128)), sink)
        p = jnp.exp2(s - jnp.concatenate([m, m], axis=1)).astype(BF16)
        v_aug = jnp.concatenate([v_all[r * w:(r + 2) * w, :], ones], axis=1)
        pv = jnp.dot(p, v_aug, preferred_element_type=F32)
        out = pv[:, :128] / (pv[:, 128:] + jnp.exp2(sink - m))
        tiles = []
        for pair in range(4):
            lo = out[pair * w:(pair + 1) * w, :]
            hi = out[(4 + pair) * w:(5 + pair) * w, :]
            tiles.append(jnp.where(lane < 64, lo, hi))
        o_ref[r * w:(r + 1) * w, :] = jnp.concatenate(tiles, axis=1).astype(BF16)


def _swa(sinks, qa, ka, va):
    w = SWA_WINDOW
    per_tile = SWA_TILE // w
    cur = lambda b, i: (b, i, 0)
    prev = lambda b, i: (b, jnp.maximum(i * per_tile - 1, 0), 0)
    return pl.pallas_call(
        _swa_kernel,
        grid=(BATCH, SEQ // SWA_TILE),
        in_specs=[pl.BlockSpec(memory_space=pltpu.SMEM),
                  pl.BlockSpec((None, SWA_TILE, Q_COLS), cur),
                  pl.BlockSpec((None, w, KV_COLS), prev),
                  pl.BlockSpec((None, SWA_TILE, KV_COLS), cur),
                  pl.BlockSpec((None, w, KV_COLS), prev),
                  pl.BlockSpec((None, SWA_TILE, KV_COLS), cur)],
        out_specs=pl.BlockSpec((None, SWA_TILE, Q_COLS), cur),
        out_shape=jax.ShapeDtypeStruct((BATCH, SEQ, Q_COLS), BF16),
        scratch_shapes=[pltpu.VMEM((N_HEADS * w, 128), BF16),
                        pltpu.VMEM((w, 2 * w), F32)],
        compiler_params=pltpu.CompilerParams(dimension_semantics=("arbitrary", "arbitrary")),
        name="swa",
    )(sinks, qa, ka, ka, va, va)


def _moba_kernel(q_ref, k_ref, v_ref, o_ref,
                 selb_ref, qconst_ref, qaug_ref, u_ref, mpart_ref, acc_ref):
    step = pl.program_id(1)
    bl = MOBA_BLOCK
    nbk = N_MOBA_BLOCKS
    rows = N_HEADS * bl
    lane = lax.broadcasted_iota(jnp.int32, (bl, 128), 1)
    lane_blk = lane % nbk
    lane_slot = lane // nbk
    sel_lane = lane < SEL_LANES
    in3 = lambda first: (lane >= first) & (lane < first + 3)

    def lanes3(first, terms):
        out = jnp.zeros((bl, 128), F32)
        for i, term in enumerate(terms):
            out = jnp.where(lane == first + i, term.astype(F32), out)
        return out

    @pl.when((pl.program_id(0) == 0) & (step == 0))
    def _():
        rowf = lax.broadcasted_iota(jnp.int32, (bl, 128), 0).astype(F32)
        for slot in range(N_HEADS):
            slope = jnp.full((bl, 128), ALIBI_SLOPES[N_HEADS + HEAD_OF_SLOT[slot]], F32)
            qc = (lanes3(ROW_LANE, _split3(-slope * rowf)) + lanes3(COL_LANE, _split3(slope))
                  + lanes3(BLK_LANE, _split3(-slope)))
            qconst_ref[slot * bl:(slot + 1) * bl, :] = qc.astype(BF16)

    @pl.when(step == 0)
    def _():
        rsel = lax.broadcasted_iota(jnp.int32, (128, SEQ), 0)
        tsel = lax.broadcasted_iota(jnp.int32, (128, SEQ), 1)
        avg = jnp.where((rsel % nbk) == (tsel // bl), 1.0 / bl, 0.0).astype(BF16)
        kmean_rows = jnp.dot(avg, k_ref[...], preferred_element_type=F32)
        kmean_t = kmean_rows.T
        r128 = lax.broadcasted_iota(jnp.int32, (128, 128), 0)
        c128 = lax.broadcasted_iota(jnp.int32, (128, 128), 1)
        km = jnp.concatenate(
            [jnp.where((c128 // nbk) == 2 * pair + (r128 >= 64).astype(jnp.int32), kmean_t, 0.0)
             for pair in range(4)], axis=0)
        q_all = q_ref[...]
        gate = sum(jnp.dot(q_all, term, preferred_element_type=F32) for term in _split3(km)[:2])
        gate_t = gate.T[:SEL_LANES].reshape(N_HEADS, nbk, SEQ)
        blk = lax.broadcasted_iota(jnp.int32, (N_HEADS, nbk, SEQ), 1)
        own = lax.broadcasted_iota(jnp.int32, (N_HEADS, nbk, SEQ), 2) // bl
        past = blk < own
        gm = jnp.where(past, gate_t, -jnp.inf)
        rank = jnp.zeros((N_HEADS, nbk, SEQ), jnp.int32)
        for d in range(1, nbk):
            partner = pltpu.roll(gm, nbk - d, 1)
            wrapped = blk + d >= nbk
            beats = (partner > gm) | ((partner == gm) & wrapped)
            rank = rank + beats.astype(jnp.int32)
        selected = past & (rank < MOBA_TOPK)
        selb_t = jnp.where(selected, 0.0, NEG).reshape(SEL_LANES, SEQ)
        selb_ref[...] = jnp.concatenate([selb_t, jnp.zeros_like(selb_t)], axis=0).T

    def fold(s):
        return jnp.maximum(s[:, :128], s[:, 128:])

    def run_step(t_a):
        tiles_t = (t_a, t_a + 1)
        past = [(side, j) for j in range(t_a + 1) for side in range(2) if j < tiles_t[side]]

        for side, t in enumerate(tiles_t):
            q = q_ref[t * bl:(t + 1) * bl, :]
            selb = selb_ref[t * bl:(t + 1) * bl, :]
            for slot in range(N_HEADS):
                pair, half = divmod(slot, 2)
                qp = q[:, pair * 128:(pair + 1) * 128]
                keep = (lane < 64) if half == 0 else (lane >= 64)
                qm = jnp.where(keep, qp, jnp.zeros_like(qp))
                extra = jnp.where(sel_lane & (lane_slot == slot), selb,
                                  qconst_ref[slot * bl:(slot + 1) * bl, :].astype(F32))
                qaug_ref[side, slot * bl:(slot + 1) * bl, :] = jnp.concatenate(
                    [qm, extra.astype(BF16)], axis=1)

        key_col = lax.broadcasted_iota(jnp.int32, (bl, 128), 0).astype(F32)
        alibi_lanes = jnp.where(in3(ROW_LANE), 1.0, jnp.where(in3(COL_LANE), key_col, 0.0))

        def scores(side, j):
            t = tiles_t[side]
            if j == t:
                extra = alibi_lanes
            else:
                extra = jnp.where(sel_lane, jnp.where(lane_blk == j, 1.0, 0.0),
                                  jnp.where(in3(BLK_LANE), float((t - j) * bl), alibi_lanes))
            kaug = jnp.concatenate([k_ref[j * bl:(j + 1) * bl, :], extra.astype(BF16)], axis=1)
            return lax.dot_general(qaug_ref[side], kaug, (((1,), (1,)), ((), ())),
                                   preferred_element_type=F32)

        def probs_v(side, blocks):
            m = mpart_ref[side]
            pb = jnp.concatenate([jnp.exp2(u_ref[n, :, c:c + 128] - m) for n, _ in blocks for c in (0, 128)],
                                 axis=1).astype(BF16)
            vj = jnp.concatenate([v_ref[j * bl:(j + 1) * bl, :] for _, j in blocks], axis=0)
            return jnp.dot(pb, jnp.concatenate([vj, jnp.ones_like(vj)], axis=1),
                           preferred_element_type=F32)

        qrow = lax.broadcasted_iota(jnp.int32, (rows, bl), 0) & (bl - 1)
        kcol = lax.broadcasted_iota(jnp.int32, (rows, bl), 1)
        for side, t in enumerate(tiles_t):
            s_own = jnp.where(qrow >= kcol, scores(side, t), NEG)
            u_ref[side] = s_own
            mpart_ref[side] = fold(s_own)
        for n, (side, j) in enumerate(past):
            sj = scores(side, j)
            u_ref[2 + n] = sj
            mpart_ref[side] = jnp.maximum(mpart_ref[side], fold(sj))

        for side in range(2):
            mpart_ref[side] = jnp.broadcast_to(jnp.max(mpart_ref[side], axis=-1, keepdims=True),
                                               (rows, 128))

        stored = [[(side, tiles_t[side])] + [(2 + n, j) for n, (s, j) in enumerate(past) if s == side]
                  for side in range(2)]
        pairs = [[blocks[i:i + PV_GROUP] for i in range(0, len(blocks), PV_GROUP)] for blocks in stored]
        for g in range(max(len(p) for p in pairs)):
            for side in range(2):
                if g < len(pairs[side]):
                    if g == 0:
                        acc_ref[side] = probs_v(side, pairs[side][g])
                    else:
                        acc_ref[side] += probs_v(side, pairs[side][g])

        for side, t in enumerate(tiles_t):
            out = acc_ref[side, :, :128] / acc_ref[side, :, 128:]
            heads = []
            for pair in range(4):
                lo = out[(2 * pair) * bl:(2 * pair + 1) * bl, :]
                hi = out[(2 * pair + 1) * bl:(2 * pair + 2) * bl, :]
                heads.append(jnp.where(lane < 64, lo, hi))
            o_ref[t * bl:(t + 1) * bl, :] = jnp.concatenate(heads, axis=1).astype(BF16)

    for pair_step in range(N_MOBA_BLOCKS // 2):
        pl.when(step == pair_step)(functools.partial(run_step, 2 * pair_step))


def _moba(qb, kb, vb):
    bl = MOBA_BLOCK
    rows = N_HEADS * bl
    step_blocks = 2 * N_MOBA_BLOCKS - 1
    whole = lambda b, t: (b, 0, 0)
    return pl.pallas_call(
        _moba_kernel,
        grid=(BATCH, N_MOBA_BLOCKS // 2),
        in_specs=[pl.BlockSpec((None, SEQ, Q_COLS), whole),
                  pl.BlockSpec((None, SEQ, KV_COLS), whole),
                  pl.BlockSpec((None, SEQ, KV_COLS), whole)],
        out_specs=pl.BlockSpec((None, SEQ, Q_COLS), whole),
        out_shape=jax.ShapeDtypeStruct((BATCH, SEQ, Q_COLS), BF16),
        scratch_shapes=[pltpu.VMEM((SEQ, 128), F32),
                        pltpu.VMEM((rows, 128), BF16),
                        pltpu.VMEM((2, rows, 256), BF16),
                        pltpu.VMEM((step_blocks, rows, bl), F32),
                        pltpu.VMEM((2, rows, 128), F32),
                        pltpu.VMEM((2, rows, 256), F32)],
        compiler_params=pltpu.CompilerParams(
            dimension_semantics=("arbitrary", "arbitrary"), vmem_limit_bytes=V7X_VMEM_LIMIT_BYTES),
        name="moba",
    )(qb, kb, vb)


def _mix_ffn_kernel(x1_ref, ya_ref, yb_ref, w_mix_hbm, wa_hbm, wb_hbm, wo_hbm, g2_ref, b2_ref,
                    w_in_hbm, w_out_hbm, g3_ref, b3_ref, o_ref,
                    act_ref, wg_ref, wa_ref, wb_ref, wo_ref, w_in_ref, w_out_ref, stage_ref, sem_ref):
    @pl.when(pl.program_id(0) == 0)
    def _():
        jobs = _window_jobs(w_mix_hbm, wg_ref, D_MODEL, 2 * D_MODEL, QKV_COLS)
        slot_row = lambda r: HEAD_OF_SLOT.index(r // HEAD_DIM) * HEAD_DIM
        jobs += _window_jobs(wa_hbm, wa_ref, Q_COLS, D_MODEL, dst_row=slot_row)
        jobs += _window_jobs(wb_hbm, wb_ref, Q_COLS, D_MODEL, dst_row=slot_row)
        jobs += _window_jobs(wo_hbm, wo_ref, D_MODEL, D_MODEL)
        jobs += _window_jobs(w_in_hbm, w_in_ref, D_MODEL, 2 * D_FF)
        jobs += _window_jobs(w_out_hbm, w_out_ref, D_FF, D_MODEL)
        _stream_cast(jobs, stage_ref, sem_ref)

    x1 = x1_ref[...]
    x1b = x1.astype(BF16)
    ga = jnp.dot(x1b, wg_ref[:, :D_MODEL], preferred_element_type=F32)
    ya = jnp.dot(ya_ref[...], wa_ref[...], preferred_element_type=F32)
    y = jax.nn.sigmoid(ga) * ya
    gb = jnp.dot(x1b, wg_ref[:, D_MODEL:], preferred_element_type=F32)
    yb = jnp.dot(yb_ref[...], wb_ref[...], preferred_element_type=F32)
    y = y + jax.nn.sigmoid(gb) * yb
    z = jnp.dot(y.astype(BF16), wo_ref[...], preferred_element_type=F32)
    x2 = _layer_norm(ALPHA * x1 + z, g2_ref[...], b2_ref[...])
    f = _swiglu(x2.astype(BF16), w_in_ref, w_out_ref, act_ref)
    o_ref[...] = _layer_norm(ALPHA * x2 + 0.5 * f, g3_ref[...], b3_ref[...])


def _mix_ffn(x1, ya, yb, w_mix, wa, wb, wo, g2, b2, w_in, w_out, g3, b3):
    tm = 2 * TOKEN_TILE
    row = lambda n: pl.BlockSpec((tm, n), lambda i: (i, 0))
    hbm = pl.BlockSpec(memory_space=pl.ANY)
    ln = _const_spec(g2.shape)
    return pl.pallas_call(
        _mix_ffn_kernel,
        grid=(TOKENS // tm,),
        in_specs=[row(D_MODEL), row(Q_COLS), row(Q_COLS), hbm, hbm, hbm, hbm, ln, ln, hbm, hbm, ln, ln],
        out_specs=row(D_MODEL),
        out_shape=jax.ShapeDtypeStruct((TOKENS, D_MODEL), F32),
        scratch_shapes=[pltpu.VMEM((tm, D_FF), BF16),
                        pltpu.VMEM((D_MODEL, 2 * D_MODEL), BF16),
                        pltpu.VMEM((Q_COLS, D_MODEL), BF16),
                        pltpu.VMEM((Q_COLS, D_MODEL), BF16),
                        pltpu.VMEM((D_MODEL, D_MODEL), BF16),
                        pltpu.VMEM((D_MODEL, 2 * D_FF), BF16),
                        pltpu.VMEM((D_FF, D_MODEL), BF16),
                        pltpu.VMEM(STAGE_SHAPE, F32),
                        pltpu.SemaphoreType.DMA((STAGE_SLOTS,))],
        compiler_params=pltpu.CompilerParams(
            dimension_semantics=("arbitrary",), vmem_limit_bytes=V7X_VMEM_LIMIT_BYTES),
        name="mix_ffn",
    )(x1, ya, yb, w_mix, wa, wb, wo, g2, b2, w_in, w_out, g3, b3)


def kernel(x, ffn1_w_in, ffn1_w_out, ln1_g, ln1_b, mix_w_in, swa_sinks, w_branch_a, w_branch_b,
           mix_w_o, ln2_g, ln2_b, ffn2_w_in, ffn2_w_out, ln3_g, ln3_b):
    assert x.shape == (BATCH, SEQ, D_MODEL) and ffn1_w_in.shape == (1, D_MODEL, 2 * D_FF)
    assert mix_w_in.shape == (1, D_MODEL, MIX_IN_COLS)
    x1, qa, ka, va, qb, kb, vb = _ffn_qkv(x.reshape(TOKENS, D_MODEL), ffn1_w_in, ffn1_w_out,
                                          ln1_g, ln1_b, mix_w_in)
    seq3 = lambda a: a.reshape(BATCH, SEQ, a.shape[-1])
    ya = _swa(swa_sinks[0], seq3(qa), seq3(ka), seq3(va))
    yb = _moba(seq3(qb), seq3(kb), seq3(vb))
    out = _mix_ffn(x1, ya.reshape(TOKENS, Q_COLS), yb.reshape(TOKENS, Q_COLS), mix_w_in,
                   w_branch_a, w_branch_b, mix_w_o, ln2_g, ln2_b, ffn2_w_in, ffn2_w_out, ln3_g, ln3_b)
    return out.reshape(BATCH, SEQ, D_MODEL)
```

```python
import functools

import jax
import jax.numpy as jnp
from jax import lax
from jax.experimental import pallas as pl
from jax.experimental.pallas import tpu as pltpu

F32 = jnp.float32
BF16 = jnp.bfloat16

D_MODEL = 1024
BATCH = 8
SEQ = 2048
TOKENS = BATCH * SEQ
HEAD_DIM = 64
N_HEADS = 8
KV_COLS = 128
Q_COLS = 512
QKV_COLS = 2 * (Q_COLS + 2 * KV_COLS)
MIX_IN_COLS = QKV_COLS + 2 * D_MODEL
SWA_WINDOW = 128
SWA_TILE = 512
MOBA_BLOCK = 256
MOBA_TOPK = 3
N_MOBA_BLOCKS = SEQ // MOBA_BLOCK
PV_GROUP = 2
D_FF = 2816
FFN_CHUNK = 256
N_FFN_CHUNKS = D_FF // FFN_CHUNK
ALPHA = 2.0 ** 0.25
LN_EPS = 1e-5
NEG = -1e30
LOG2E = 1.4426950408889634
Q_SCALE = 0.125 * LOG2E
ALIBI_SLOPES = tuple(float(2.0 ** (-8.0 * i / 16.0)) * LOG2E for i in range(1, 17))

HEAD_OF_SLOT = (0, 4, 1, 5, 2, 6, 3, 7)

V7X_VMEM_LIMIT_BYTES = 60 * 1024 * 1024
TOKEN_TILE = 512

SEL_LANES = 64
ROW_LANE = 64
COL_LANE = 67
BLK_LANE = 70


def _const_spec(shape):
    zeros = (0,) * len(shape)
    return pl.BlockSpec(shape, lambda *_: zeros, pipeline_mode=pl.Buffered(1))


def _split3(x):
    hi = x.astype(BF16)
    r1 = x - hi.astype(F32)
    mid = r1.astype(BF16)
    lo = (r1 - mid.astype(F32)).astype(BF16)
    return hi, mid, lo


def _layer_norm(y, g, b):
    mu = jnp.mean(y, axis=-1, keepdims=True)
    yc = y - mu
    var = jnp.mean(yc * yc, axis=-1, keepdims=True)
    return yc * lax.rsqrt(var + LN_EPS) * g + b


def _swiglu(xb, w_in_ref, w_out_ref, g_ref):
    for c in range(N_FFN_CHUNKS):
        lo = c * FFN_CHUNK
        a = jnp.dot(xb, w_in_ref[:, lo:lo + FFN_CHUNK], preferred_element_type=F32)
        u = jnp.dot(xb, w_in_ref[:, D_FF + lo:D_FF + lo + FFN_CHUNK], preferred_element_type=F32)
        g_ref[:, lo:lo + FFN_CHUNK] = (a * jax.nn.sigmoid(a) * u).astype(BF16)
    return jnp.dot(g_ref[...], w_out_ref[...], preferred_element_type=F32)


STAGE_ROWS = 256
STAGE_COLS = 512
STAGE_SLOTS = 6
STAGE_SHAPE = (STAGE_SLOTS, STAGE_ROWS, STAGE_COLS)


def _window_jobs(src_hbm, dst_ref, n_rows, n_cols, src_col0=0, dst_col0=0, fix=None, dst_row=None):
    jobs = []
    for r in range(0, n_rows, STAGE_ROWS):
        for c in range(0, n_cols, STAGE_COLS):
            w = min(STAGE_COLS, n_cols - c)
            src = src_hbm.at[0, pl.ds(r, STAGE_ROWS), pl.ds(src_col0 + c, w)]

            def store(tile, r=r, c=c, w=w):
                out = (fix(tile) if fix else tile).astype(BF16)
                cols = slice(dst_col0 + c, dst_col0 + c + w)
                if dst_row is None:
                    dst_ref[r:r + STAGE_ROWS, cols] = out
                else:
                    for b in range(0, STAGE_ROWS, HEAD_DIM):
                        dst_ref[dst_row(r + b):dst_row(r + b) + HEAD_DIM, cols] = out[b:b + HEAD_DIM]

            jobs.append((src, w, store))
    return jobs


def _stream_cast(jobs, stage_ref, sem_ref):
    def copy(n):
        src, w, _ = jobs[n]
        slot = n % STAGE_SLOTS
        dst = stage_ref.at[slot] if w == STAGE_COLS else stage_ref.at[slot, :, pl.ds(0, w)]
        return pltpu.make_async_copy(src, dst, sem_ref.at[slot])

    ahead = STAGE_SLOTS - 1
    for n in range(min(ahead, len(jobs))):
        copy(n).start()
    for n, (_, w, store) in enumerate(jobs):
        if n + ahead < len(jobs):
            copy(n + ahead).start()
        copy(n).wait()
        store(stage_ref[n % STAGE_SLOTS, :, 0:w])


def _to_slot_order(q):
    lane = lax.broadcasted_iota(jnp.int32, (q.shape[0], 128), 1)
    t = [q[:, i * 128:(i + 1) * 128] for i in range(4)]
    swap = lambda a: pltpu.roll(a, 64, 1)
    return jnp.concatenate([jnp.where(lane < 64, t[0], swap(t[2])), jnp.where(lane < 64, swap(t[0]), t[2]),
                            jnp.where(lane < 64, t[1], swap(t[3])), jnp.where(lane < 64, swap(t[1]), t[3])],
                           axis=1)


def _ffn_qkv_kernel(x_ref, w_in_hbm, w_out_hbm, g_ref, b_ref, w_mix_hbm,
                    x1_ref, qa_ref, ka_ref, va_ref, qb_ref, kb_ref, vb_ref,
                    act_ref, w_in_ref, w_out_ref, wqkv_ref, stage_ref, sem_ref):
    @pl.when(pl.program_id(0) == 0)
    def _():
        jobs = _window_jobs(w_in_hbm, w_in_ref, D_MODEL, 2 * D_FF)
        jobs += _window_jobs(w_out_hbm, w_out_ref, D_FF, D_MODEL)
        for col0, n_cols, fix in ((0, Q_COLS, _to_slot_order), (Q_COLS, 2 * KV_COLS, None),
                                  (QKV_COLS // 2, Q_COLS, _to_slot_order),
                                  (QKV_COLS // 2 + Q_COLS, 2 * KV_COLS, None)):
            jobs += _window_jobs(w_mix_hbm, wqkv_ref, D_MODEL, n_cols, col0, col0, fix)
        _stream_cast(jobs, stage_ref, sem_ref)

    x = x_ref[...]
    f = _swiglu(x.astype(BF16), w_in_ref, w_out_ref, act_ref)
    x1 = _layer_norm(ALPHA * x + 0.5 * f, g_ref[...], b_ref[...])
    x1_ref[...] = x1
    h = jnp.dot(x1.astype(BF16), wqkv_ref[...], preferred_element_type=F32)
    qa_ref[...] = (h[:, 0:512] * Q_SCALE).astype(BF16)
    ka_ref[...] = h[:, 512:640].astype(BF16)
    va_ref[...] = h[:, 640:768].astype(BF16)
    qb_ref[...] = (h[:, 768:1280] * Q_SCALE).astype(BF16)
    kb_ref[...] = h[:, 1280:1408].astype(BF16)
    vb_ref[...] = h[:, 1408:1536].astype(BF16)


def _ffn_qkv(x, w_in, w_out, g, b, w_mix):
    tm = 2 * TOKEN_TILE
    row = lambda n: pl.BlockSpec((tm, n), lambda i: (i, 0))
    hbm = pl.BlockSpec(memory_space=pl.ANY)
    out_cols = (D_MODEL, Q_COLS, KV_COLS, KV_COLS, Q_COLS, KV_COLS, KV_COLS)
    out_dtypes = (F32,) + (BF16,) * 6
    return pl.pallas_call(
        _ffn_qkv_kernel,
        grid=(TOKENS // tm,),
        in_specs=[row(D_MODEL), hbm, hbm, _const_spec(g.shape), _const_spec(b.shape), hbm],
        out_specs=[row(n) for n in out_cols],
        out_shape=[jax.ShapeDtypeStruct((TOKENS, n), dt) for n, dt in zip(out_cols, out_dtypes)],
        scratch_shapes=[pltpu.VMEM((tm, D_FF), BF16),
                        pltpu.VMEM((D_MODEL, 2 * D_FF), BF16),
                        pltpu.VMEM((D_FF, D_MODEL), BF16),
                        pltpu.VMEM((D_MODEL, QKV_COLS), BF16),
                        pltpu.VMEM(STAGE_SHAPE, F32),
                        pltpu.SemaphoreType.DMA((STAGE_SLOTS,))],
        compiler_params=pltpu.CompilerParams(
            dimension_semantics=("arbitrary",), vmem_limit_bytes=V7X_VMEM_LIMIT_BYTES),
        name="ffn_qkv",
    )(x, w_in, w_out, g, b, w_mix)


SWA_ROW_SLOTS = (0, 2, 4, 6, 1, 3, 5, 7)


def _swa_consts(qconst_ref, bias_ref):
    w = SWA_WINDOW
    lane = lax.broadcasted_iota(jnp.int32, (w, 128), 1)
    rowf = lax.broadcasted_iota(jnp.int32, (w, 128), 0).astype(F32)
    for rb, slot in enumerate(SWA_ROW_SLOTS):
        slope = jnp.full((w, 128), ALIBI_SLOPES[HEAD_OF_SLOT[slot]], F32)
        qc = jnp.zeros((w, 128), F32)
        for first, terms in ((ROW_LANE, _split3(-slope * (rowf + w))), (COL_LANE, _split3(slope))):
            for n, term in enumerate(terms):
                qc = jnp.where(lane == first + n, term.astype(F32), qc)
        qconst_ref[rb * w:(rb + 1) * w, :] = qc.astype(BF16)
    dist = (lax.broadcasted_iota(jnp.int32, (w, 2 * w), 0) + w
            - lax.broadcasted_iota(jnp.int32, (w, 2 * w), 1))
    bias_ref[...] = jnp.where((dist >= 0) & (dist < w), 0.0, NEG)


def _swa_tile(first_tile, sink_ref, q_ref, kp_ref, kc_ref, vp_ref, vc_ref, o_ref, qconst_ref, bias_ref):
    w = SWA_WINDOW
    rows = N_HEADS * w
    lane = lax.broadcasted_iota(jnp.int32, (w, 128), 1)
    k_all = jnp.concatenate([kp_ref[...], kc_ref[...]], axis=0)
    v_all = jnp.concatenate([vp_ref[...], vc_ref[...]], axis=0)
    key_col = lax.broadcasted_iota(jnp.int32, (2 * w, 128), 0).astype(F32)
    lane2 = lax.broadcasted_iota(jnp.int32, (2 * w, 128), 1)
    k_extra = jnp.where((lane2 >= ROW_LANE) & (lane2 < ROW_LANE + 3), 1.0,
                        jnp.where((lane2 >= COL_LANE) & (lane2 < COL_LANE + 3), key_col, 0.0)).astype(BF16)
    sink = jnp.concatenate([jnp.full((w, 128), sink_ref[HEAD_OF_SLOT[slot]] * LOG2E, F32)
                            for slot in SWA_ROW_SLOTS], axis=0)
    ones = jnp.ones((2 * w, 128), BF16)
    bias = bias_ref[...]
    kcol = lax.broadcasted_iota(jnp.int32, (w, 2 * w), 1)
    bias_first = jnp.where(kcol >= w, bias, NEG) if first_tile else bias

    for r in range(SWA_TILE // w):
        q = q_ref[r * w:(r + 1) * w, :]
        q_aug = jnp.concatenate(
            [jnp.concatenate([q[:, (slot // 2) * 128:(slot // 2 + 1) * 128],
                              qconst_ref[rb * w:(rb + 1) * w, :]], axis=1)
             for rb, slot in enumerate(SWA_ROW_SLOTS)], axis=0)
        k = k_all[r * w:(r + 2) * w, :]
        zero = jnp.zeros_like(k)
        nt = (((1,), (1,)), ((), ()))
        s0 = lax.dot_general(q_aug[:rows // 2], jnp.concatenate([jnp.where(lane2 < 64, k, zero), k_extra], axis=1),
                             nt, preferred_element_type=F32)
        s1 = lax.dot_general(q_aug[rows // 2:], jnp.concatenate([jnp.where(lane2 >= 64, k, zero), k_extra], axis=1),
                             nt, preferred_element_type=F32)
        b = bias_first if r == 0 else bias
        s = (jnp.concatenate([s0, s1], axis=0).reshape(N_HEADS, w, 2 * w) + b[None]).reshape(rows, 2 * w)
        row_max = jnp.max(jnp.maximum(s[:, :128], s[:, 128:]), axis=-1, keepdims=True)
        m = jnp.maximum(jnp.broadcast_to(row_max, (rows, 128)), sink)
        p = jnp.exp2(s - jnp.concatenate([m, m], axis=1)).astype(BF16)
        v_aug = jnp.concatenate([v_all[r * w:(r + 2) * w, :], ones], axis=1)
        pv = jnp.dot(p, v_aug, preferred_element_type=F32)
        out = pv[:, :128] / (pv[:, 128:] + jnp.exp2(sink - m))
        tiles = []
        for pair in range(4):
            lo = out[pair * w:(pair + 1) * w, :]
            hi = out[(4 + pair) * w:(5 + pair) * w, :]
            tiles.append(jnp.where(lane < 64, lo, hi))
        o_ref[r * w:(r + 1) * w, :] = jnp.concatenate(tiles, axis=1).astype(BF16)


def _attn_kernel(sink_ref, qa_ref, kap_ref, kac_ref, vap_ref, vac_ref, q_ref, k_ref, v_ref,
                 oa_ref, o_ref,
                 selb_ref, qconst_ref, qaug_ref, u_ref, mpart_ref, acc_ref, swa_qconst_ref, swa_bias_ref):
    step = pl.program_id(1)
    bl = MOBA_BLOCK
    nbk = N_MOBA_BLOCKS
    rows = N_HEADS * bl
    lane = lax.broadcasted_iota(jnp.int32, (bl, 128), 1)
    lane_blk = lane % nbk
    lane_slot = lane // nbk
    sel_lane = lane < SEL_LANES
    in3 = lambda first: (lane >= first) & (lane < first + 3)

    def lanes3(first, terms):
        out = jnp.zeros((bl, 128), F32)
        for i, term in enumerate(terms):
            out = jnp.where(lane == first + i, term.astype(F32), out)
        return out

    @pl.when((pl.program_id(0) == 0) & (step == 0))
    def _():
        _swa_consts(swa_qconst_ref, swa_bias_ref)
        rowf = lax.broadcasted_iota(jnp.int32, (bl, 128), 0).astype(F32)
        for slot in range(N_HEADS):
            slope = jnp.full((bl, 128), ALIBI_SLOPES[N_HEADS + HEAD_OF_SLOT[slot]], F32)
            qc = (lanes3(ROW_LANE, _split3(-slope * rowf)) + lanes3(COL_LANE, _split3(slope))
                  + lanes3(BLK_LANE, _split3(-slope)))
            qconst_ref[slot * bl:(slot + 1) * bl, :] = qc.astype(BF16)

    @pl.when(step == 0)
    def _():
        rsel = lax.broadcasted_iota(jnp.int32, (128, SEQ), 0)
        tsel = lax.broadcasted_iota(jnp.int32, (128, SEQ), 1)
        avg = jnp.where((rsel % nbk) == (tsel // bl), 1.0 / bl, 0.0).astype(BF16)
        kmean_rows = jnp.dot(avg, k_ref[...], preferred_element_type=F32)
        kmean_t = kmean_rows.T
        r128 = lax.broadcasted_iota(jnp.int32, (128, 128), 0)
        c128 = lax.broadcasted_iota(jnp.int32, (128, 128), 1)
        km = jnp.concatenate(
            [jnp.where((c128 // nbk) == 2 * pair + (r128 >= 64).astype(jnp.int32), kmean_t, 0.0)
             for pair in range(4)], axis=0)
        q_all = q_ref[...]
        gate = sum(jnp.dot(q_all, term, preferred_element_type=F32) for term in _split3(km)[:2])
        gate_t = gate.T[:SEL_LANES].reshape(N_HEADS, nbk, SEQ)
        blk = lax.broadcasted_iota(jnp.int32, (N_HEADS, nbk, SEQ), 1)
        own = lax.broadcasted_iota(jnp.int32, (N_HEADS, nbk, SEQ), 2) // bl
        past = blk < own
        gm = jnp.where(past, gate_t, -jnp.inf)
        rank = jnp.zeros((N_HEADS, nbk, SEQ), jnp.int32)
        for d in range(1, nbk):
            partner = pltpu.roll(gm, nbk - d, 1)
            wrapped = blk + d >= nbk
            beats = (partner > gm) | ((partner == gm) & wrapped)
            rank = rank + beats.astype(jnp.int32)
        selected = past & (rank < MOBA_TOPK)
        selb_t = jnp.where(selected, 0.0, NEG).reshape(SEL_LANES, SEQ)
        selb_ref[...] = jnp.concatenate([selb_t, jnp.zeros_like(selb_t)], axis=0).T

    def fold(s):
        return jnp.maximum(s[:, :128], s[:, 128:])

    def run_step(t_a):
        tiles_t = (t_a, t_a + 1)
        past = [(side, j) for j in range(t_a + 1) for side in range(2) if j < tiles_t[side]]

        _swa_tile(t_a == 0, sink_ref, qa_ref, kap_ref, kac_ref, vap_ref, vac_ref, oa_ref,
                  swa_qconst_ref, swa_bias_ref)

        for side, t in enumerate(tiles_t):
            q = q_ref[t * bl:(t + 1) * bl, :]
            selb = selb_ref[t * bl:(t + 1) * bl, :]
            for slot in range(N_HEADS):
                pair, half = divmod(slot, 2)
                qp = q[:, pair * 128:(pair + 1) * 128]
                keep = (lane < 64) if half == 0 else (lane >= 64)
                qm = jnp.where(keep, qp, jnp.zeros_like(qp))
                extra = jnp.where(sel_lane & (lane_slot == slot), selb,
                                  qconst_ref[slot * bl:(slot + 1) * bl, :].astype(F32))
                qaug_ref[side, slot * bl:(slot + 1) * bl, :] = jnp.concatenate(
                    [qm, extra.astype(BF16)], axis=1)

        key_col = lax.broadcasted_iota(jnp.int32, (bl, 128), 0).astype(F32)
        alibi_lanes = jnp.where(in3(ROW_LANE), 1.0, jnp.where(in3(COL_LANE), key_col, 0.0))

        def scores(side, j):
            t = tiles_t[side]
            if j == t:
                extra = alibi_lanes
            else:
                extra = jnp.where(sel_lane, jnp.where(lane_blk == j, 1.0, 0.0),
                                  jnp.where(in3(BLK_LANE), float((t - j) * bl), alibi_lanes))
            kaug = jnp.concatenate([k_ref[j * bl:(j + 1) * bl, :], extra.astype(BF16)], axis=1)
            return lax.dot_general(qaug_ref[side], kaug, (((1,), (1,)), ((), ())),
                                   preferred_element_type=F32)

        def probs_v(side, blocks):
            m = mpart_ref[side]
            pb = jnp.concatenate([jnp.exp2(u_ref[n, :, c:c + 128] - m) for n, _ in blocks for c in (0, 128)],
                                 axis=1).astype(BF16)
            vj = jnp.concatenate([v_ref[j * bl:(j + 1) * bl, :] for _, j in blocks], axis=0)
            return jnp.dot(pb, jnp.concatenate([vj, jnp.ones_like(vj)], axis=1),
                           preferred_element_type=F32)

        qrow = lax.broadcasted_iota(jnp.int32, (rows, bl), 0) & (bl - 1)
        kcol = lax.broadcasted_iota(jnp.int32, (rows, bl), 1)
        for side, t in enumerate(tiles_t):
            s_own = jnp.where(qrow >= kcol, scores(side, t), NEG)
            u_ref[side] = s_own
            mpart_ref[side] = fold(s_own)
        for n, (side, j) in enumerate(past):
            sj = scores(side, j)
            u_ref[2 + n] = sj
            mpart_ref[side] = jnp.maximum(mpart_ref[side], fold(sj))

        for side in range(2):
            mpart_ref[side] = jnp.broadcast_to(jnp.max(mpart_ref[side], axis=-1, keepdims=True),
                                               (rows, 128))

        stored = [[(side, tiles_t[side])] + [(2 + n, j) for n, (s, j) in enumerate(past) if s == side]
                  for side in range(2)]
        pairs = [[blocks[i:i + PV_GROUP] for i in range(0, len(blocks), PV_GROUP)] for blocks in stored]
        for g in range(max(len(p) for p in pairs)):
            for side in range(2):
                if g < len(pairs[side]):
                    if g == 0:
                        acc_ref[side] = probs_v(side, pairs[side][g])
                    else:
                        acc_ref[side] += probs_v(side, pairs[side][g])

        for side, t in enumerate(tiles_t):
            out = acc_ref[side, :, :128] / acc_ref[side, :, 128:]
            heads = []
            for pair in range(4):
                lo = out[(2 * pair) * bl:(2 * pair + 1) * bl, :]
                hi = out[(2 * pair + 1) * bl:(2 * pair + 2) * bl, :]
                heads.append(jnp.where(lane < 64, lo, hi))
            o_ref[t * bl:(t + 1) * bl, :] = jnp.concatenate(heads, axis=1).astype(BF16)

    for pair_step in range(N_MOBA_BLOCKS // 2):
        pl.when(step == pair_step)(functools.partial(run_step, 2 * pair_step))


def _attention(sinks, qa, ka, va, qb, kb, vb):
    bl = MOBA_BLOCK
    w = SWA_WINDOW
    assert SWA_TILE == 2 * bl
    rows = N_HEADS * bl
    step_blocks = 2 * N_MOBA_BLOCKS - 1
    whole = lambda b, s: (b, 0, 0)
    cur = lambda b, s: (b, s, 0)
    prev = lambda b, s: (b, jnp.maximum(s * (SWA_TILE // w) - 1, 0), 0)
    out = jax.ShapeDtypeStruct((BATCH, SEQ, Q_COLS), BF16)
    return pl.pallas_call(
        _attn_kernel,
        grid=(BATCH, SEQ // SWA_TILE),
        in_specs=[pl.BlockSpec(memory_space=pltpu.SMEM),
                  pl.BlockSpec((None, SWA_TILE, Q_COLS), cur),
                  pl.BlockSpec((None, w, KV_COLS), prev),
                  pl.BlockSpec((None, SWA_TILE, KV_COLS), cur),
                  pl.BlockSpec((None, w, KV_COLS), prev),
                  pl.BlockSpec((None, SWA_TILE, KV_COLS), cur),
                  pl.BlockSpec((None, SEQ, Q_COLS), whole),
                  pl.BlockSpec((None, SEQ, KV_COLS), whole),
                  pl.BlockSpec((None, SEQ, KV_COLS), whole)],
        out_specs=[pl.BlockSpec((None, SWA_TILE, Q_COLS), cur),
                   pl.BlockSpec((None, SEQ, Q_COLS), whole)],
        out_shape=[out, out],
        scratch_shapes=[pltpu.VMEM((SEQ, 128), F32),
                        pltpu.VMEM((rows, 128), BF16),
                        pltpu.VMEM((2, rows, 256), BF16),
                        pltpu.VMEM((step_blocks, rows, bl), F32),
                        pltpu.VMEM((2, rows, 128), F32),
                        pltpu.VMEM((2, rows, 256), F32),
                        pltpu.VMEM((N_HEADS * w, 128), BF16),
                        pltpu.VMEM((w, 2 * w), F32)],
        compiler_params=pltpu.CompilerParams(
            dimension_semantics=("arbitrary", "arbitrary"), vmem_limit_bytes=V7X_VMEM_LIMIT_BYTES),
        name="attention",
    )(sinks, qa, ka, ka, va, va, qb, kb, vb)


def _mix_ffn_kernel(x1_ref, ya_ref, yb_ref, w_mix_hbm, wa_hbm, wb_hbm, wo_hbm, g2_ref, b2_ref,
                    w_in_hbm, w_out_hbm, g3_ref, b3_ref, o_ref,
                    act_ref, wg_ref, wa_ref, wb_ref, wo_ref, w_in_ref, w_out_ref, stage_ref, sem_ref):
    @pl.when(pl.program_id(0) == 0)
    def _():
        jobs = _window_jobs(w_mix_hbm, wg_ref, D_MODEL, 2 * D_MODEL, QKV_COLS)
        slot_row = lambda r: HEAD_OF_SLOT.index(r // HEAD_DIM) * HEAD_DIM
        jobs += _window_jobs(wa_hbm, wa_ref, Q_COLS, D_MODEL, dst_row=slot_row)
        jobs += _window_jobs(wb_hbm, wb_ref, Q_COLS, D_MODEL, dst_row=slot_row)
        jobs += _window_jobs(wo_hbm, wo_ref, D_MODEL, D_MODEL)
        jobs += _window_jobs(w_in_hbm, w_in_ref, D_MODEL, 2 * D_FF)
        jobs += _window_jobs(w_out_hbm, w_out_ref, D_FF, D_MODEL)
        _stream_cast(jobs, stage_ref, sem_ref)

    x1 = x1_ref[...]
    x1b = x1.astype(BF16)
    ga = jnp.dot(x1b, wg_ref[:, :D_MODEL], preferred_element_type=F32)
    ya = jnp.dot(ya_ref[...], wa_ref[...], preferred_element_type=F32)
    y = jax.nn.sigmoid(ga) * ya
    gb = jnp.dot(x1b, wg_ref[:, D_MODEL:], preferred_element_type=F32)
    yb = jnp.dot(yb_ref[...], wb_ref[...], preferred_element_type=F32)
    y = y + jax.nn.sigmoid(gb) * yb
    z = jnp.dot(y.astype(BF16), wo_ref[...], preferred_element_type=F32)
    x2 = _layer_norm(ALPHA * x1 + z, g2_ref[...], b2_ref[...])
    f = _swiglu(x2.astype(BF16), w_in_ref, w_out_ref, act_ref)
    o_ref[...] = _layer_norm(ALPHA * x2 + 0.5 * f, g3_ref[...], b3_ref[...])


def _mix_ffn(x1, ya, yb, w_mix, wa, wb, wo, g2, b2, w_in, w_out, g3, b3):
    tm = 2 * TOKEN_TILE
    row = lambda n: pl.BlockSpec((tm, n), lambda i: (i, 0))
    hbm = pl.BlockSpec(memory_space=pl.ANY)
    ln = _const_spec(g2.shape)
    return pl.pallas_call(
        _mix_ffn_kernel,
        grid=(TOKENS // tm,),
        in_specs=[row(D_MODEL), row(Q_COLS), row(Q_COLS), hbm, hbm, hbm, hbm, ln, ln, hbm, hbm, ln, ln],
        out_specs=row(D_MODEL),
        out_shape=jax.ShapeDtypeStruct((TOKENS, D_MODEL), F32),
        scratch_shapes=[pltpu.VMEM((tm, D_FF), BF16),
                        pltpu.VMEM((D_MODEL, 2 * D_MODEL), BF16),
                        pltpu.VMEM((Q_COLS, D_MODEL), BF16),
                        pltpu.VMEM((Q_COLS, D_MODEL), BF16),
                        pltpu.VMEM((D_MODEL, D_MODEL), BF16),
                        pltpu.VMEM((D_MODEL, 2 * D_FF), BF16),
                        pltpu.VMEM((D_FF, D_MODEL), BF16),
                        pltpu.VMEM(STAGE_SHAPE, F32),
                        pltpu.SemaphoreType.DMA((STAGE_SLOTS,))],
        compiler_params=pltpu.CompilerParams(
            dimension_semantics=("arbitrary",), vmem_limit_bytes=V7X_VMEM_LIMIT_BYTES),
        name="mix_ffn",
    )(x1, ya, yb, w_mix, wa, wb, wo, g2, b2, w_in, w_out, g3, b3)


def kernel(x, ffn1_w_in, ffn1_w_out, ln1_g, ln1_b, mix_w_in, swa_sinks, w_branch_a, w_branch_b,
           mix_w_o, ln2_g, ln2_b, ffn2_w_in, ffn2_w_out, ln3_g, ln3_b):
    assert x.shape == (BATCH, SEQ, D_MODEL) and ffn1_w_in.shape == (1, D_MODEL, 2 * D_FF)
    assert mix_w_in.shape == (1, D_MODEL, MIX_IN_COLS)
    x1, qa, ka, va, qb, kb, vb = _ffn_qkv(x.reshape(TOKENS, D_MODEL), ffn1_w_in, ffn1_w_out,
                                          ln1_g, ln1_b, mix_w_in)
    seq3 = lambda a: a.reshape(BATCH, SEQ, a.shape[-1])
    ya, yb = _attention(swa_sinks[0], seq3(qa), seq3(ka), seq3(va), seq3(qb), seq3(kb), seq3(vb))
    out = _mix_ffn(x1, ya.reshape(TOKENS, Q_COLS), yb.reshape(TOKENS, Q_COLS), mix_w_in,
                   w_branch_a, w_branch_b, mix_w_o, ln2_g, ln2_b, ffn2_w_in, ffn2_w_out, ln3_g, ln3_b)
    return out.reshape(BATCH, SEQ, D_MODEL)
```

```python
import functools

import jax
import jax.numpy as jnp
from jax import lax
from jax.experimental import pallas as pl
from jax.experimental.pallas import tpu as pltpu

F32 = jnp.float32
BF16 = jnp.bfloat16

D_MODEL = 1024
BATCH = 8
SEQ = 2048
TOKENS = BATCH * SEQ
HEAD_DIM = 64
N_HEADS = 8
KV_COLS = 128
Q_COLS = 512
QKV_COLS = 2 * (Q_COLS + 2 * KV_COLS)
MIX_IN_COLS = QKV_COLS + 2 * D_MODEL
SWA_WINDOW = 128
SWA_TILE = 512
MOBA_BLOCK = 256
MOBA_TOPK = 3
N_MOBA_BLOCKS = SEQ // MOBA_BLOCK
PV_GROUP = 2
D_FF = 2816
FFN_CHUNK = 256
N_FFN_CHUNKS = D_FF // FFN_CHUNK
ALPHA = 2.0 ** 0.25
LN_EPS = 1e-5
NEG = -1e30
LOG2E = 1.4426950408889634
Q_SCALE = 0.125 * LOG2E
ALIBI_SLOPES = tuple(float(2.0 ** (-8.0 * i / 16.0)) * LOG2E for i in range(1, 17))

HEAD_OF_SLOT = (0, 4, 1, 5, 2, 6, 3, 7)

V7X_VMEM_LIMIT_BYTES = 60 * 1024 * 1024
TOKEN_TILE = 1024

SEL_LANES = 64
ROW_LANE = 64
COL_LANE = 67
BLK_LANE = 70


def _const_spec(shape):
    zeros = (0,) * len(shape)
    return pl.BlockSpec(shape, lambda *_: zeros, pipeline_mode=pl.Buffered(1))


def _split3(x):
    hi = x.astype(BF16)
    r1 = x - hi.astype(F32)
    mid = r1.astype(BF16)
    lo = (r1 - mid.astype(F32)).astype(BF16)
    return hi, mid, lo


def _layer_norm(y, g, b):
    mu = jnp.mean(y, axis=-1, keepdims=True)
    yc = y - mu
    var = jnp.mean(yc * yc, axis=-1, keepdims=True)
    return yc * lax.rsqrt(var + LN_EPS) * g + b


def _swiglu(xb, w_in_ref, w_out_ref, g_ref):
    for c in range(N_FFN_CHUNKS):
        lo = c * FFN_CHUNK
        a = jnp.dot(xb, w_in_ref[:, lo:lo + FFN_CHUNK], preferred_element_type=F32)
        u = jnp.dot(xb, w_in_ref[:, D_FF + lo:D_FF + lo + FFN_CHUNK], preferred_element_type=F32)
        g_ref[:, lo:lo + FFN_CHUNK] = (a * jax.nn.sigmoid(a) * u).astype(BF16)
    return jnp.dot(g_ref[...], w_out_ref[...], preferred_element_type=F32)


STAGE_ROWS = 256
STAGE_COLS = 512
STAGE_SLOTS = 6
STAGE_SHAPE = (STAGE_SLOTS, STAGE_ROWS, STAGE_COLS)


def _window_jobs(src_hbm, dst_ref, n_rows, n_cols, src_col0=0, dst_col0=0, fix=None, dst_row=None):
    jobs = []
    for r in range(0, n_rows, STAGE_ROWS):
        for c in range(0, n_cols, STAGE_COLS):
            w = min(STAGE_COLS, n_cols - c)
            src = src_hbm.at[0, pl.ds(r, STAGE_ROWS), pl.ds(src_col0 + c, w)]

            def store(tile, r=r, c=c, w=w):
                out = (fix(tile) if fix else tile).astype(BF16)
                cols = slice(dst_col0 + c, dst_col0 + c + w)
                if dst_row is None:
                    dst_ref[r:r + STAGE_ROWS, cols] = out
                else:
                    for b in range(0, STAGE_ROWS, HEAD_DIM):
                        dst_ref[dst_row(r + b):dst_row(r + b) + HEAD_DIM, cols] = out[b:b + HEAD_DIM]

            jobs.append((src, w, store))
    return jobs


def _stream_cast(jobs, stage_ref, sem_ref):
    def copy(n):
        src, w, _ = jobs[n]
        slot = n % STAGE_SLOTS
        dst = stage_ref.at[slot] if w == STAGE_COLS else stage_ref.at[slot, :, pl.ds(0, w)]
        return pltpu.make_async_copy(src, dst, sem_ref.at[slot])

    ahead = STAGE_SLOTS - 1
    for n in range(min(ahead, len(jobs))):
        copy(n).start()
    for n, (_, w, store) in enumerate(jobs):
        if n + ahead < len(jobs):
            copy(n + ahead).start()
        copy(n).wait()
        store(stage_ref[n % STAGE_SLOTS, :, 0:w])


def _to_slot_order(q):
    lane = lax.broadcasted_iota(jnp.int32, (q.shape[0], 128), 1)
    t = [q[:, i * 128:(i + 1) * 128] for i in range(4)]
    swap = lambda a: pltpu.roll(a, 64, 1)
    return jnp.concatenate([jnp.where(lane < 64, t[0], swap(t[2])), jnp.where(lane < 64, swap(t[0]), t[2]),
                            jnp.where(lane < 64, t[1], swap(t[3])), jnp.where(lane < 64, swap(t[1]), t[3])],
                           axis=1)


def _ffn_qkv_kernel(x_ref, w_in_hbm, w_out_hbm, g_ref, b_ref, w_mix_hbm,
                    x1_ref, qa_ref, ka_ref, va_ref, qb_ref, kb_ref, vb_ref,
                    act_ref, w_in_ref, w_out_ref, wqkv_ref, stage_ref, sem_ref):
    @pl.when(pl.program_id(0) == 0)
    def _():
        jobs = _window_jobs(w_in_hbm, w_in_ref, D_MODEL, 2 * D_FF)
        jobs += _window_jobs(w_out_hbm, w_out_ref, D_FF, D_MODEL)
        for col0, n_cols, fix in ((0, Q_COLS, _to_slot_order), (Q_COLS, 2 * KV_COLS, None),
                                  (QKV_COLS // 2, Q_COLS, _to_slot_order),
                                  (QKV_COLS // 2 + Q_COLS, 2 * KV_COLS, None)):
            jobs += _window_jobs(w_mix_hbm, wqkv_ref, D_MODEL, n_cols, col0, col0, fix)
        _stream_cast(jobs, stage_ref, sem_ref)

    x = x_ref[...]
    f = _swiglu(x.astype(BF16), w_in_ref, w_out_ref, act_ref)
    x1 = _layer_norm(ALPHA * x + 0.5 * f, g_ref[...], b_ref[...])
    x1_ref[...] = x1
    h = jnp.dot(x1.astype(BF16), wqkv_ref[...], preferred_element_type=F32)
    qa_ref[...] = (h[:, 0:512] * Q_SCALE).astype(BF16)
    ka_ref[...] = h[:, 512:640].astype(BF16)
    va_ref[...] = h[:, 640:768].astype(BF16)
    qb_ref[...] = (h[:, 768:1280] * Q_SCALE).astype(BF16)
    kb_ref[...] = h[:, 1280:1408].astype(BF16)
    vb_ref[...] = h[:, 1408:1536].astype(BF16)


def _ffn_qkv(x, w_in, w_out, g, b, w_mix):
    tm = TOKEN_TILE
    row = lambda n: pl.BlockSpec((tm, n), lambda i: (i, 0))
    hbm = pl.BlockSpec(memory_space=pl.ANY)
    out_cols = (D_MODEL, Q_COLS, KV_COLS, KV_COLS, Q_COLS, KV_COLS, KV_COLS)
    out_dtypes = (F32,) + (BF16,) * 6
    return pl.pallas_call(
        _ffn_qkv_kernel,
        grid=(TOKENS // tm,),
        in_specs=[row(D_MODEL), hbm, hbm, _const_spec(g.shape), _const_spec(b.shape), hbm],
        out_specs=[row(n) for n in out_cols],
        out_shape=[jax.ShapeDtypeStruct((TOKENS, n), dt) for n, dt in zip(out_cols, out_dtypes)],
        scratch_shapes=[pltpu.VMEM((tm, D_FF), BF16),
                        pltpu.VMEM((D_MODEL, 2 * D_FF), BF16),
                        pltpu.VMEM((D_FF, D_MODEL), BF16),
                        pltpu.VMEM((D_MODEL, QKV_COLS), BF16),
                        pltpu.VMEM(STAGE_SHAPE, F32),
                        pltpu.SemaphoreType.DMA((STAGE_SLOTS,))],
        compiler_params=pltpu.CompilerParams(
            dimension_semantics=("arbitrary",), vmem_limit_bytes=V7X_VMEM_LIMIT_BYTES),
        name="ffn_qkv",
    )(x, w_in, w_out, g, b, w_mix)


SWA_ROW_SLOTS = (0, 2, 4, 6, 1, 3, 5, 7)


def _swa_consts(qconst_ref, bias_ref):
    w = SWA_WINDOW
    lane = lax.broadcasted_iota(jnp.int32, (w, 128), 1)
    rowf = lax.broadcasted_iota(jnp.int32, (w, 128), 0).astype(F32)
    for rb, slot in enumerate(SWA_ROW_SLOTS):
        slope = jnp.full((w, 128), ALIBI_SLOPES[HEAD_OF_SLOT[slot]], F32)
        qc = jnp.zeros((w, 128), F32)
        for first, terms in ((ROW_LANE, _split3(-slope * (rowf + w))), (COL_LANE, _split3(slope))):
            for n, term in enumerate(terms):
                qc = jnp.where(lane == first + n, term.astype(F32), qc)
        qconst_ref[rb * w:(rb + 1) * w, :] = qc.astype(BF16)
    dist = (lax.broadcasted_iota(jnp.int32, (w, 2 * w), 0) + w
            - lax.broadcasted_iota(jnp.int32, (w, 2 * w), 1))
    bias_ref[...] = jnp.where((dist >= 0) & (dist < w), 0.0, NEG)


def _swa_tile(first_tile, sink_ref, q_ref, kp_ref, kc_ref, vp_ref, vc_ref, o_ref, qconst_ref, bias_ref):
    w = SWA_WINDOW
    rows = N_HEADS * w
    lane = lax.broadcasted_iota(jnp.int32, (w, 128), 1)
    k_all = jnp.concatenate([kp_ref[...], kc_ref[...]], axis=0)
    v_all = jnp.concatenate([vp_ref[...], vc_ref[...]], axis=0)
    key_col = lax.broadcasted_iota(jnp.int32, (2 * w, 128), 0).astype(F32)
    lane2 = lax.broadcasted_iota(jnp.int32, (2 * w, 128), 1)
    k_extra = jnp.where((lane2 >= ROW_LANE) & (lane2 < ROW_LANE + 3), 1.0,
                        jnp.where((lane2 >= COL_LANE) & (lane2 < COL_LANE + 3), key_col, 0.0)).astype(BF16)
    sink = jnp.concatenate([jnp.full((w, 128), sink_ref[HEAD_OF_SLOT[slot]] * LOG2E, F32)
                            for slot in SWA_ROW_SLOTS], axis=0)
    ones = jnp.ones((2 * w, 128), BF16)
    bias = bias_ref[...]
    kcol = lax.broadcasted_iota(jnp.int32, (w, 2 * w), 1)
    bias_first = jnp.where(kcol >= w, bias, NEG) if first_tile else bias

    for r in range(SWA_TILE // w):
        q = q_ref[r * w:(r + 1) * w, :]
        q_aug = jnp.concatenate(
            [jnp.concatenate([q[:, (slot // 2) * 128:(slot // 2 + 1) * 128],
                              qconst_ref[rb * w:(rb + 1) * w, :]], axis=1)
             for rb, slot in enumerate(SWA_ROW_SLOTS)], axis=0)
        k = k_all[r * w:(r + 2) * w, :]
        zero = jnp.zeros_like(k)
        nt = (((1,), (1,)), ((), ()))
        s0 = lax.dot_general(q_aug[:rows // 2], jnp.concatenate([jnp.where(lane2 < 64, k, zero), k_extra], axis=1),
                             nt, preferred_element_type=F32)
        s1 = lax.dot_general(q_aug[rows // 2:], jnp.concatenate([jnp.where(lane2 >= 64, k, zero), k_extra], axis=1),
                             nt, preferred_element_type=F32)
        b = bias_first if r == 0 else bias
        s = (jnp.concatenate([s0, s1], axis=0).reshape(N_HEADS, w, 2 * w) + b[None]).reshape(rows, 2 * w)
        row_max = jnp.max(jnp.maximum(s[:, :128], s[:, 128:]), axis=-1, keepdims=True)
        m = jnp.maximum(jnp.broadcast_to(row_max, (rows, 128)), sink)
        p = jnp.exp2(s - jnp.concatenate([m, m], axis=1)).astype(BF16)
        v_aug = jnp.concatenate([v_all[r * w:(r + 2) * w, :], ones], axis=1)
        pv = jnp.dot(p, v_aug, preferred_element_type=F32)
        out = pv[:, :128] / (pv[:, 128:] + jnp.exp2(sink - m))
        tiles = []
        for pair in range(4):
            lo = out[pair * w:(pair + 1) * w, :]
            hi = out[(4 + pair) * w:(5 + pair) * w, :]
            tiles.append(jnp.where(lane < 64, lo, hi))
        o_ref[r * w:(r + 1) * w, :] = jnp.concatenate(tiles, axis=1).astype(BF16)


def _attn_kernel(sink_ref, qa_ref, kap_ref, kac_ref, vap_ref, vac_ref, q_ref, k_ref, v_ref,
                 oa_ref, o_ref,
                 selb_ref, qconst_ref, qaug_ref, u_ref, mpart_ref, acc_ref, swa_qconst_ref, swa_bias_ref):
    step = pl.program_id(1)
    bl = MOBA_BLOCK
    nbk = N_MOBA_BLOCKS
    rows = N_HEADS * bl
    lane = lax.broadcasted_iota(jnp.int32, (bl, 128), 1)
    lane_blk = lane % nbk
    lane_slot = lane // nbk
    sel_lane = lane < SEL_LANES
    in3 = lambda first: (lane >= first) & (lane < first + 3)

    def lanes3(first, terms):
        out = jnp.zeros((bl, 128), F32)
        for i, term in enumerate(terms):
            out = jnp.where(lane == first + i, term.astype(F32), out)
        return out

    @pl.when((pl.program_id(0) == 0) & (step == 0))
    def _():
        _swa_consts(swa_qconst_ref, swa_bias_ref)
        rowf = lax.broadcasted_iota(jnp.int32, (bl, 128), 0).astype(F32)
        for slot in range(N_HEADS):
            slope = jnp.full((bl, 128), ALIBI_SLOPES[N_HEADS + HEAD_OF_SLOT[slot]], F32)
            qc = (lanes3(ROW_LANE, _split3(-slope * rowf)) + lanes3(COL_LANE, _split3(slope))
                  + lanes3(BLK_LANE, _split3(-slope)))
            qconst_ref[slot * bl:(slot + 1) * bl, :] = qc.astype(BF16)

    @pl.when(step == 0)
    def _():
        rsel = lax.broadcasted_iota(jnp.int32, (128, SEQ), 0)
        tsel = lax.broadcasted_iota(jnp.int32, (128, SEQ), 1)
        avg = jnp.where((rsel % nbk) == (tsel // bl), 1.0 / bl, 0.0).astype(BF16)
        kmean_rows = jnp.dot(avg, k_ref[...], preferred_element_type=F32)
        kmean_t = kmean_rows.T
        r128 = lax.broadcasted_iota(jnp.int32, (128, 128), 0)
        c128 = lax.broadcasted_iota(jnp.int32, (128, 128), 1)
        km = jnp.concatenate(
            [jnp.where((c128 // nbk) == 2 * pair + (r128 >= 64).astype(jnp.int32), kmean_t, 0.0)
             for pair in range(4)], axis=0)
        q_all = q_ref[...]
        gate = sum(jnp.dot(q_all, term, preferred_element_type=F32) for term in _split3(km)[:2])
        gate_t = gate.T[:SEL_LANES].reshape(N_HEADS, nbk, SEQ)
        blk = lax.broadcasted_iota(jnp.int32, (N_HEADS, nbk, SEQ), 1)
        own = lax.broadcasted_iota(jnp.int32, (N_HEADS, nbk, SEQ), 2) // bl
        past = blk < own
        gm = jnp.where(past, gate_t, -jnp.inf)
        rank = jnp.zeros((N_HEADS, nbk, SEQ), jnp.int32)
        for d in range(1, nbk):
            partner = pltpu.roll(gm, nbk - d, 1)
            wrapped = blk + d >= nbk
            beats = (partner > gm) | ((partner == gm) & wrapped)
            rank = rank + beats.astype(jnp.int32)
        selected = past & (rank < MOBA_TOPK)
        selb_t = jnp.where(selected, 0.0, NEG).reshape(SEL_LANES, SEQ)
        selb_ref[...] = jnp.concatenate([selb_t, jnp.zeros_like(selb_t)], axis=0).T

    def fold(s):
        return jnp.maximum(s[:, :128], s[:, 128:])

    def run_step(t_a):
        tiles_t = (t_a, t_a + 1)
        past = [(side, j) for j in range(t_a + 1) for side in range(2) if j < tiles_t[side]]

        _swa_tile(t_a == 0, sink_ref, qa_ref, kap_ref, kac_ref, vap_ref, vac_ref, oa_ref,
                  swa_qconst_ref, swa_bias_ref)

        for side, t in enumerate(tiles_t):
            q = q_ref[t * bl:(t + 1) * bl, :]
            selb = selb_ref[t * bl:(t + 1) * bl, :]
            for slot in range(N_HEADS):
                pair, half = divmod(slot, 2)
                qp = q[:, pair * 128:(pair + 1) * 128]
                keep = (lane < 64) if half == 0 else (lane >= 64)
                qm = jnp.where(keep, qp, jnp.zeros_like(qp))
                extra = jnp.where(sel_lane & (lane_slot == slot), selb,
                                  qconst_ref[slot * bl:(slot + 1) * bl, :].astype(F32))
                qaug_ref[side, slot * bl:(slot + 1) * bl, :] = jnp.concatenate(
                    [qm, extra.astype(BF16)], axis=1)

        key_col = lax.broadcasted_iota(jnp.int32, (bl, 128), 0).astype(F32)
        alibi_lanes = jnp.where(in3(ROW_LANE), 1.0, jnp.where(in3(COL_LANE), key_col, 0.0))

        def scores(side, j):
            t = tiles_t[side]
            if j == t:
                extra = alibi_lanes
            else:
                extra = jnp.where(sel_lane, jnp.where(lane_blk == j, 1.0, 0.0),
                                  jnp.where(in3(BLK_LANE), float((t - j) * bl), alibi_lanes))
            kaug = jnp.concatenate([k_ref[j * bl:(j + 1) * bl, :], extra.astype(BF16)], axis=1)
            return lax.dot_general(qaug_ref[side], kaug, (((1,), (1,)), ((), ())),
                                   preferred_element_type=F32)

        def probs_v(side, blocks):
            m = mpart_ref[side]
            pb = jnp.concatenate([jnp.exp2(u_ref[n, :, c:c + 128] - m) for n, _ in blocks for c in (0, 128)],
                                 axis=1).astype(BF16)
            vj = jnp.concatenate([v_ref[j * bl:(j + 1) * bl, :] for _, j in blocks], axis=0)
            return jnp.dot(pb, jnp.concatenate([vj, jnp.ones_like(vj)], axis=1),
                           preferred_element_type=F32)

        qrow = lax.broadcasted_iota(jnp.int32, (rows, bl), 0) & (bl - 1)
        kcol = lax.broadcasted_iota(jnp.int32, (rows, bl), 1)
        for side, t in enumerate(tiles_t):
            s_own = jnp.where(qrow >= kcol, scores(side, t), NEG)
            u_ref[side] = s_own
            mpart_ref[side] = fold(s_own)
        for n, (side, j) in enumerate(past):
            sj = scores(side, j)
            u_ref[2 + n] = sj
            mpart_ref[side] = jnp.maximum(mpart_ref[side], fold(sj))

        for side in range(2):
            mpart_ref[side] = jnp.broadcast_to(jnp.max(mpart_ref[side], axis=-1, keepdims=True),
                                               (rows, 128))

        stored = [[(side, tiles_t[side])] + [(2 + n, j) for n, (s, j) in enumerate(past) if s == side]
                  for side in range(2)]
        pairs = [[blocks[i:i + PV_GROUP] for i in range(0, len(blocks), PV_GROUP)] for blocks in stored]
        for g in range(max(len(p) for p in pairs)):
            for side in range(2):
                if g < len(pairs[side]):
                    if g == 0:
                        acc_ref[side] = probs_v(side, pairs[side][g])
                    else:
                        acc_ref[side] += probs_v(side, pairs[side][g])

        for side, t in enumerate(tiles_t):
            out = acc_ref[side, :, :128] / acc_ref[side, :, 128:]
            heads = []
            for pair in range(4):
                lo = out[(2 * pair) * bl:(2 * pair + 1) * bl, :]
                hi = out[(2 * pair + 1) * bl:(2 * pair + 2) * bl, :]
                heads.append(jnp.where(lane < 64, lo, hi))
            o_ref[t * bl:(t + 1) * bl, :] = jnp.concatenate(heads, axis=1).astype(BF16)

    for pair_step in range(N_MOBA_BLOCKS // 2):
        pl.when(step == pair_step)(functools.partial(run_step, 2 * pair_step))


def _attention(sinks, qa, ka, va, qb, kb, vb):
    bl = MOBA_BLOCK
    w = SWA_WINDOW
    assert SWA_TILE == 2 * bl
    rows = N_HEADS * bl
    step_blocks = 2 * N_MOBA_BLOCKS - 1
    whole = lambda b, s: (b, 0, 0)
    cur = lambda b, s: (b, s, 0)
    prev = lambda b, s: (b, jnp.maximum(s * (SWA_TILE // w) - 1, 0), 0)
    out = jax.ShapeDtypeStruct((BATCH, SEQ, Q_COLS), BF16)
    return pl.pallas_call(
        _attn_kernel,
        grid=(BATCH, SEQ // SWA_TILE),
        in_specs=[pl.BlockSpec(memory_space=pltpu.SMEM),
                  pl.BlockSpec((None, SWA_TILE, Q_COLS), cur),
                  pl.BlockSpec((None, w, KV_COLS), prev),
                  pl.BlockSpec((None, SWA_TILE, KV_COLS), cur),
                  pl.BlockSpec((None, w, KV_COLS), prev),
                  pl.BlockSpec((None, SWA_TILE, KV_COLS), cur),
                  pl.BlockSpec((None, SEQ, Q_COLS), whole),
                  pl.BlockSpec((None, SEQ, KV_COLS), whole),
                  pl.BlockSpec((None, SEQ, KV_COLS), whole)],
        out_specs=[pl.BlockSpec((None, SWA_TILE, Q_COLS), cur),
                   pl.BlockSpec((None, SEQ, Q_COLS), whole)],
        out_shape=[out, out],
        scratch_shapes=[pltpu.VMEM((SEQ, 128), F32),
                        pltpu.VMEM((rows, 128), BF16),
                        pltpu.VMEM((2, rows, 256), BF16),
                        pltpu.VMEM((step_blocks, rows, bl), F32),
                        pltpu.VMEM((2, rows, 128), F32),
                        pltpu.VMEM((2, rows, 256), F32),
                        pltpu.VMEM((N_HEADS * w, 128), BF16),
                        pltpu.VMEM((w, 2 * w), F32)],
        compiler_params=pltpu.CompilerParams(
            dimension_semantics=("arbitrary", "arbitrary"), vmem_limit_bytes=V7X_VMEM_LIMIT_BYTES),
        name="attention",
    )(sinks, qa, ka, ka, va, va, qb, kb, vb)


def _mix_ffn_kernel(x1_ref, ya_ref, yb_ref, w_mix_hbm, wa_hbm, wb_hbm, wo_hbm, g2_ref, b2_ref,
                    w_in_hbm, w_out_hbm, g3_ref, b3_ref, o_ref,
                    act_ref, wg_ref, wa_ref, wb_ref, wo_ref, w_in_ref, w_out_ref, stage_ref, sem_ref):
    @pl.when(pl.program_id(0) == 0)
    def _():
        jobs = _window_jobs(w_mix_hbm, wg_ref, D_MODEL, 2 * D_MODEL, QKV_COLS)
        slot_row = lambda r: HEAD_OF_SLOT.index(r // HEAD_DIM) * HEAD_DIM
        jobs += _window_jobs(wa_hbm, wa_ref, Q_COLS, D_MODEL, dst_row=slot_row)
        jobs += _window_jobs(wb_hbm, wb_ref, Q_COLS, D_MODEL, dst_row=slot_row)
        jobs += _window_jobs(wo_hbm, wo_ref, D_MODEL, D_MODEL)
        jobs += _window_jobs(w_in_hbm, w_in_ref, D_MODEL, 2 * D_FF)
        jobs += _window_jobs(w_out_hbm, w_out_ref, D_FF, D_MODEL)
        _stream_cast(jobs, stage_ref, sem_ref)

    x1 = x1_ref[...]
    x1b = x1.astype(BF16)
    ga = jnp.dot(x1b, wg_ref[:, :D_MODEL], preferred_element_type=F32)
    ya = jnp.dot(ya_ref[...], wa_ref[...], preferred_element_type=F32)
    y = jax.nn.sigmoid(ga) * ya
    gb = jnp.dot(x1b, wg_ref[:, D_MODEL:], preferred_element_type=F32)
    yb = jnp.dot(yb_ref[...], wb_ref[...], preferred_element_type=F32)
    y = y + jax.nn.sigmoid(gb) * yb
    z = jnp.dot(y.astype(BF16), wo_ref[...], preferred_element_type=F32)
    x2 = _layer_norm(ALPHA * x1 + z, g2_ref[...], b2_ref[...])
    f = _swiglu(x2.astype(BF16), w_in_ref, w_out_ref, act_ref)
    o_ref[...] = _layer_norm(ALPHA * x2 + 0.5 * f, g3_ref[...], b3_ref[...])


def _mix_ffn(x1, ya, yb, w_mix, wa, wb, wo, g2, b2, w_in, w_out, g3, b3):
    tm = TOKEN_TILE
    row = lambda n: pl.BlockSpec((tm, n), lambda i: (i, 0))
    hbm = pl.BlockSpec(memory_space=pl.ANY)
    ln = _const_spec(g2.shape)
    return pl.pallas_call(
        _mix_ffn_kernel,
        grid=(TOKENS // tm,),
        in_specs=[row(D_MODEL), row(Q_COLS), row(Q_COLS), hbm, hbm, hbm, hbm, ln, ln, hbm, hbm, ln, ln],
        out_specs=row(D_MODEL),
        out_shape=jax.ShapeDtypeStruct((TOKENS, D_MODEL), F32),
        scratch_shapes=[pltpu.VMEM((tm, D_FF), BF16),
                        pltpu.VMEM((D_MODEL, 2 * D_MODEL), BF16),
                        pltpu.VMEM((Q_COLS, D_MODEL), BF16),
                        pltpu.VMEM((Q_COLS, D_MODEL), BF16),
                        pltpu.VMEM((D_MODEL, D_MODEL), BF16),
                        pltpu.VMEM((D_MODEL, 2 * D_FF), BF16),
                        pltpu.VMEM((D_FF, D_MODEL), BF16),
                        pltpu.VMEM(STAGE_SHAPE, F32),
                        pltpu.SemaphoreType.DMA((STAGE_SLOTS,))],
        compiler_params=pltpu.CompilerParams(
            dimension_semantics=("arbitrary",), vmem_limit_bytes=V7X_VMEM_LIMIT_BYTES),
        name="mix_ffn",
    )(x1, ya, yb, w_mix, wa, wb, wo, g2, b2, w_in, w_out, g3, b3)


def kernel(x, ffn1_w_in, ffn1_w_out, ln1_g, ln1_b, mix_w_in, swa_sinks, w_branch_a, w_branch_b,
           mix_w_o, ln2_g, ln2_b, ffn2_w_in, ffn2_w_out, ln3_g, ln3_b):
    assert x.shape == (BATCH, SEQ, D_MODEL) and ffn1_w_in.shape == (1, D_MODEL, 2 * D_FF)
    assert mix_w_in.shape == (1, D_MODEL, MIX_IN_COLS)
    x1, qa, ka, va, qb, kb, vb = _ffn_qkv(x.reshape(TOKENS, D_MODEL), ffn1_w_in, ffn1_w_out,
                                          ln1_g, ln1_b, mix_w_in)
    seq3 = lambda a: a.reshape(BATCH, SEQ, a.shape[-1])
    ya, yb = _attention(swa_sinks[0], seq3(qa), seq3(ka), seq3(va), seq3(qb), seq3(kb), seq3(vb))
    out = _mix_ffn(x1, ya.reshape(TOKENS, Q_COLS), yb.reshape(TOKENS, Q_COLS), mix_w_in,
                   w_branch_a, w_branch_b, mix_w_o, ln2_g, ln2_b, ffn2_w_in, ffn2_w_out, ln3_g, ln3_b)
    return out.reshape(BATCH, SEQ, D_MODEL)
```

```python
import functools

import jax
import jax.numpy as jnp
from jax import lax
from jax.experimental import pallas as pl
from jax.experimental.pallas import tpu as pltpu

F32 = jnp.float32
BF16 = jnp.bfloat16

D_MODEL = 1024
BATCH = 8
SEQ = 2048
TOKENS = BATCH * SEQ
HEAD_DIM = 64
N_HEADS = 8
KV_COLS = 128
Q_COLS = 512
QKV_COLS = 2 * (Q_COLS + 2 * KV_COLS)
MIX_IN_COLS = QKV_COLS + 2 * D_MODEL
SWA_WINDOW = 128
SWA_TILE = 512
MOBA_BLOCK = 256
MOBA_TOPK = 3
N_MOBA_BLOCKS = SEQ // MOBA_BLOCK
PV_GROUP = 2
D_FF = 2816
FFN_CHUNK = 256
N_FFN_CHUNKS = D_FF // FFN_CHUNK
ALPHA = 2.0 ** 0.25
LN_EPS = 1e-5
NEG = -1e30
LOG2E = 1.4426950408889634
Q_SCALE = 0.125 * LOG2E
ALIBI_SLOPES = tuple(float(2.0 ** (-8.0 * i / 16.0)) * LOG2E for i in range(1, 17))

HEAD_OF_SLOT = (0, 4, 1, 5, 2, 6, 3, 7)

V7X_VMEM_LIMIT_BYTES = 60 * 1024 * 1024
TOKEN_TILE = 1024

SEL_LANES = 64
ROW_LANE = 64
COL_LANE = 67
BLK_LANE = 70


def _const_spec(shape):
    zeros = (0,) * len(shape)
    return pl.BlockSpec(shape, lambda *_: zeros, pipeline_mode=pl.Buffered(1))


def _split3(x):
    hi = x.astype(BF16)
    r1 = x - hi.astype(F32)
    mid = r1.astype(BF16)
    lo = (r1 - mid.astype(F32)).astype(BF16)
    return hi, mid, lo


def _layer_norm(y, g, b):
    mu = jnp.mean(y, axis=-1, keepdims=True)
    yc = y - mu
    var = jnp.mean(yc * yc, axis=-1, keepdims=True)
    return yc * lax.rsqrt(var + LN_EPS) * g + b


def _half(w):
    return 0.5 * w


def _swiglu(xb, w_in_ref, w_out_ref, g_ref):
    for c in range(N_FFN_CHUNKS):
        lo = c * FFN_CHUNK
        a = jnp.dot(xb, w_in_ref[:, lo:lo + FFN_CHUNK], preferred_element_type=F32)
        u = jnp.dot(xb, w_in_ref[:, D_FF + lo:D_FF + lo + FFN_CHUNK], preferred_element_type=F32)
        g_ref[:, lo:lo + FFN_CHUNK] = (a * jax.nn.sigmoid(a) * u).astype(BF16)
    return jnp.dot(g_ref[...], w_out_ref[...], preferred_element_type=F32)


STAGE_ROWS = 256
STAGE_COLS = 512
STAGE_SLOTS = 6
STAGE_SHAPE = (STAGE_SLOTS, STAGE_ROWS, STAGE_COLS)


def _window_jobs(src_hbm, dst_ref, n_rows, n_cols, src_col0=0, dst_col0=0, fix=None, dst_row=None):
    jobs = []
    for r in range(0, n_rows, STAGE_ROWS):
        for c in range(0, n_cols, STAGE_COLS):
            w = min(STAGE_COLS, n_cols - c)
            src = src_hbm.at[0, pl.ds(r, STAGE_ROWS), pl.ds(src_col0 + c, w)]

            def store(tile, r=r, c=c, w=w):
                out = (fix(tile) if fix else tile).astype(BF16)
                cols = slice(dst_col0 + c, dst_col0 + c + w)
                if dst_row is None:
                    dst_ref[r:r + STAGE_ROWS, cols] = out
                else:
                    for b in range(0, STAGE_ROWS, HEAD_DIM):
                        dst_ref[dst_row(r + b):dst_row(r + b) + HEAD_DIM, cols] = out[b:b + HEAD_DIM]

            jobs.append((src, w, store))
    return jobs


def _stream_cast(jobs, stage_ref, sem_ref):
    def copy(n):
        src, w, _ = jobs[n]
        slot = n % STAGE_SLOTS
        dst = stage_ref.at[slot] if w == STAGE_COLS else stage_ref.at[slot, :, pl.ds(0, w)]
        return pltpu.make_async_copy(src, dst, sem_ref.at[slot])

    ahead = STAGE_SLOTS - 1
    for n in range(min(ahead, len(jobs))):
        copy(n).start()
    for n, (_, w, store) in enumerate(jobs):
        if n + ahead < len(jobs):
            copy(n + ahead).start()
        copy(n).wait()
        store(stage_ref[n % STAGE_SLOTS, :, 0:w])


def _to_slot_order(q):
    lane = lax.broadcasted_iota(jnp.int32, (q.shape[0], 128), 1)
    t = [q[:, i * 128:(i + 1) * 128] for i in range(4)]
    swap = lambda a: pltpu.roll(a, 64, 1)
    return jnp.concatenate([jnp.where(lane < 64, t[0], swap(t[2])), jnp.where(lane < 64, swap(t[0]), t[2]),
                            jnp.where(lane < 64, t[1], swap(t[3])), jnp.where(lane < 64, swap(t[1]), t[3])],
                           axis=1)


def _ffn_qkv_kernel(x_ref, w_in_hbm, w_out_hbm, g_ref, b_ref, w_mix_hbm,
                    x1_ref, qa_ref, ka_ref, va_ref, qb_ref, kb_ref, vb_ref,
                    act_ref, w_in_ref, w_out_ref, wqkv_ref, stage_ref, sem_ref):
    @pl.when(pl.program_id(0) == 0)
    def _():
        jobs = _window_jobs(w_in_hbm, w_in_ref, D_MODEL, 2 * D_FF)
        jobs += _window_jobs(w_out_hbm, w_out_ref, D_FF, D_MODEL, fix=_half)
        for col0, n_cols, fix in ((0, Q_COLS, _to_slot_order), (Q_COLS, 2 * KV_COLS, None),
                                  (QKV_COLS // 2, Q_COLS, _to_slot_order),
                                  (QKV_COLS // 2 + Q_COLS, 2 * KV_COLS, None)):
            jobs += _window_jobs(w_mix_hbm, wqkv_ref, D_MODEL, n_cols, col0, col0, fix)
        _stream_cast(jobs, stage_ref, sem_ref)

    x = x_ref[...]
    f = _swiglu(x.astype(BF16), w_in_ref, w_out_ref, act_ref)
    x1 = _layer_norm(ALPHA * x + f, g_ref[...], b_ref[...])
    x1_ref[...] = x1
    h = jnp.dot(x1.astype(BF16), wqkv_ref[...], preferred_element_type=F32)
    qa_ref[...] = (h[:, 0:512] * Q_SCALE).astype(BF16)
    ka_ref[...] = h[:, 512:640].astype(BF16)
    va_ref[...] = h[:, 640:768].astype(BF16)
    qb_ref[...] = (h[:, 768:1280] * Q_SCALE).astype(BF16)
    kb_ref[...] = h[:, 1280:1408].astype(BF16)
    vb_ref[...] = h[:, 1408:1536].astype(BF16)


def _ffn_qkv(x, w_in, w_out, g, b, w_mix):
    tm = TOKEN_TILE
    row = lambda n: pl.BlockSpec((tm, n), lambda i: (i, 0))
    hbm = pl.BlockSpec(memory_space=pl.ANY)
    out_cols = (D_MODEL, Q_COLS, KV_COLS, KV_COLS, Q_COLS, KV_COLS, KV_COLS)
    out_dtypes = (F32,) + (BF16,) * 6
    return pl.pallas_call(
        _ffn_qkv_kernel,
        grid=(TOKENS // tm,),
        in_specs=[row(D_MODEL), hbm, hbm, _const_spec(g.shape), _const_spec(b.shape), hbm],
        out_specs=[row(n) for n in out_cols],
        out_shape=[jax.ShapeDtypeStruct((TOKENS, n), dt) for n, dt in zip(out_cols, out_dtypes)],
        scratch_shapes=[pltpu.VMEM((tm, D_FF), BF16),
                        pltpu.VMEM((D_MODEL, 2 * D_FF), BF16),
                        pltpu.VMEM((D_FF, D_MODEL), BF16),
                        pltpu.VMEM((D_MODEL, QKV_COLS), BF16),
                        pltpu.VMEM(STAGE_SHAPE, F32),
                        pltpu.SemaphoreType.DMA((STAGE_SLOTS,))],
        compiler_params=pltpu.CompilerParams(
            dimension_semantics=("arbitrary",), vmem_limit_bytes=V7X_VMEM_LIMIT_BYTES),
        name="ffn_qkv",
    )(x, w_in, w_out, g, b, w_mix)


SWA_ROW_SLOTS = (0, 2, 4, 6, 1, 3, 5, 7)


def _swa_consts(qconst_ref, bias_ref):
    w = SWA_WINDOW
    lane = lax.broadcasted_iota(jnp.int32, (w, 128), 1)
    rowf = lax.broadcasted_iota(jnp.int32, (w, 128), 0).astype(F32)
    for rb, slot in enumerate(SWA_ROW_SLOTS):
        slope = jnp.full((w, 128), ALIBI_SLOPES[HEAD_OF_SLOT[slot]], F32)
        qc = jnp.zeros((w, 128), F32)
        for first, terms in ((ROW_LANE, _split3(-slope * (rowf + w))), (COL_LANE, _split3(slope))):
            for n, term in enumerate(terms):
                qc = jnp.where(lane == first + n, term.astype(F32), qc)
        qconst_ref[rb * w:(rb + 1) * w, :] = qc.astype(BF16)
    dist = (lax.broadcasted_iota(jnp.int32, (w, 2 * w), 0) + w
            - lax.broadcasted_iota(jnp.int32, (w, 2 * w), 1))
    bias_ref[...] = jnp.where((dist >= 0) & (dist < w), 0.0, NEG)


def _swa_tile(first_tile, sink_ref, q_ref, kp_ref, kc_ref, vp_ref, vc_ref, o_ref, qconst_ref, bias_ref):
    w = SWA_WINDOW
    rows = N_HEADS * w
    lane = lax.broadcasted_iota(jnp.int32, (w, 128), 1)
    k_all = jnp.concatenate([kp_ref[...], kc_ref[...]], axis=0)
    v_all = jnp.concatenate([vp_ref[...], vc_ref[...]], axis=0)
    key_col = lax.broadcasted_iota(jnp.int32, (2 * w, 128), 0).astype(F32)
    lane2 = lax.broadcasted_iota(jnp.int32, (2 * w, 128), 1)
    k_extra = jnp.where((lane2 >= ROW_LANE) & (lane2 < ROW_LANE + 3), 1.0,
                        jnp.where((lane2 >= COL_LANE) & (lane2 < COL_LANE + 3), key_col, 0.0)).astype(BF16)
    sink = jnp.concatenate([jnp.full((w, 128), sink_ref[HEAD_OF_SLOT[slot]] * LOG2E, F32)
                            for slot in SWA_ROW_SLOTS], axis=0)
    ones = jnp.ones((2 * w, 128), BF16)
    bias = bias_ref[...]
    kcol = lax.broadcasted_iota(jnp.int32, (w, 2 * w), 1)
    bias_first = jnp.where(kcol >= w, bias, NEG) if first_tile else bias

    for r in range(SWA_TILE // w):
        q = q_ref[r * w:(r + 1) * w, :]
        q_aug = jnp.concatenate(
            [jnp.concatenate([q[:, (slot // 2) * 128:(slot // 2 + 1) * 128],
                              qconst_ref[rb * w:(rb + 1) * w, :]], axis=1)
             for rb, slot in enumerate(SWA_ROW_SLOTS)], axis=0)
        k = k_all[r * w:(r + 2) * w, :]
        zero = jnp.zeros_like(k)
        nt = (((1,), (1,)), ((), ()))
        s0 = lax.dot_general(q_aug[:rows // 2], jnp.concatenate([jnp.where(lane2 < 64, k, zero), k_extra], axis=1),
                             nt, preferred_element_type=F32)
        s1 = lax.dot_general(q_aug[rows // 2:], jnp.concatenate([jnp.where(lane2 >= 64, k, zero), k_extra], axis=1),
                             nt, preferred_element_type=F32)
        b = bias_first if r == 0 else bias
        s = (jnp.concatenate([s0, s1], axis=0).reshape(N_HEADS, w, 2 * w) + b[None]).reshape(rows, 2 * w)
        row_max = jnp.max(jnp.maximum(s[:, :128], s[:, 128:]), axis=-1, keepdims=True)
        m = jnp.maximum(jnp.broadcast_to(row_max, (rows, 128)), sink)
        p = jnp.exp2(s - jnp.concatenate([m, m], axis=1)).astype(BF16)
        v_aug = jnp.concatenate([v_all[r * w:(r + 2) * w, :], ones], axis=1)
        pv = jnp.dot(p, v_aug, preferred_element_type=F32)
        out = pv[:, :128] / (pv[:, 128:] + jnp.exp2(sink - m))
        tiles = []
        for pair in range(4):
            lo = out[pair * w:(pair + 1) * w, :]
            hi = out[(4 + pair) * w:(5 + pair) * w, :]
            tiles.append(jnp.where(lane < 64, lo, hi))
        o_ref[r * w:(r + 1) * w, :] = jnp.concatenate(tiles, axis=1).astype(BF16)


def _select_kernel(q_ref, k_ref, selb_ref):
    bl = MOBA_BLOCK
    nbk = N_MOBA_BLOCKS
    rsel = lax.broadcasted_iota(jnp.int32, (128, SEQ), 0)
    tsel = lax.broadcasted_iota(jnp.int32, (128, SEQ), 1)
    avg = jnp.where((rsel % nbk) == (tsel // bl), 1.0 / bl, 0.0).astype(BF16)
    kmean_rows = jnp.dot(avg, k_ref[...], preferred_element_type=F32)
    kmean_t = kmean_rows.T
    r128 = lax.broadcasted_iota(jnp.int32, (128, 128), 0)
    c128 = lax.broadcasted_iota(jnp.int32, (128, 128), 1)
    km = jnp.concatenate(
        [jnp.where((c128 // nbk) == 2 * pair + (r128 >= 64).astype(jnp.int32), kmean_t, 0.0)
         for pair in range(4)], axis=0)
    q_all = q_ref[...]
    gate = sum(jnp.dot(q_all, term, preferred_element_type=F32) for term in _split3(km)[:2])
    gate_t = gate.T[:SEL_LANES].reshape(N_HEADS, nbk, SEQ)
    blk = lax.broadcasted_iota(jnp.int32, (N_HEADS, nbk, SEQ), 1)
    own = lax.broadcasted_iota(jnp.int32, (N_HEADS, nbk, SEQ), 2) // bl
    past = blk < own
    gm = jnp.where(past, gate_t, -jnp.inf)
    rank = jnp.zeros((N_HEADS, nbk, SEQ), jnp.int32)
    for d in range(1, nbk):
        partner = pltpu.roll(gm, nbk - d, 1)
        wrapped = blk + d >= nbk
        beats = (partner > gm) | ((partner == gm) & wrapped)
        rank = rank + beats.astype(jnp.int32)
    selected = past & (rank < MOBA_TOPK)
    selb_t = jnp.where(selected, 0.0, NEG).reshape(SEL_LANES, SEQ)
    selb_ref[...] = jnp.concatenate([selb_t, jnp.zeros_like(selb_t)], axis=0).T.astype(BF16)


def _moba_select(qb, kb):
    whole = lambda b: (b, 0, 0)
    return pl.pallas_call(
        _select_kernel,
        grid=(BATCH,),
        in_specs=[pl.BlockSpec((None, SEQ, Q_COLS), whole), pl.BlockSpec((None, SEQ, KV_COLS), whole)],
        out_specs=pl.BlockSpec((None, SEQ, 128), whole),
        out_shape=jax.ShapeDtypeStruct((BATCH, SEQ, 128), BF16),
        compiler_params=pltpu.CompilerParams(dimension_semantics=("arbitrary",)),
        name="moba_select",
    )(qb, kb)


def _attn_kernel(sink_ref, qa_ref, kap_ref, kac_ref, vap_ref, vac_ref, q_ref, k_ref, v_ref, selb_ref,
                 oa_ref, o_ref,
                 qconst_ref, causal_ref, qaug_ref, u_ref, mpart_ref, acc_ref,
                 swa_qconst_ref, swa_bias_ref):
    step = pl.program_id(0)
    bl = MOBA_BLOCK
    nbk = N_MOBA_BLOCKS
    rows = N_HEADS * bl
    lane = lax.broadcasted_iota(jnp.int32, (bl, 128), 1)
    lane_blk = lane % nbk
    lane_slot = lane // nbk
    sel_lane = lane < SEL_LANES
    in3 = lambda first: (lane >= first) & (lane < first + 3)

    def lanes3(first, terms):
        out = jnp.zeros((bl, 128), F32)
        for i, term in enumerate(terms):
            out = jnp.where(lane == first + i, term.astype(F32), out)
        return out

    @pl.when((pl.program_id(1) == 0) & (step == 0))
    def _():
        _swa_consts(swa_qconst_ref, swa_bias_ref)
        causal_ref[...] = jnp.where(lax.broadcasted_iota(jnp.int32, (bl, bl), 0)
                                    >= lax.broadcasted_iota(jnp.int32, (bl, bl), 1), 0.0, NEG)
        rowf = lax.broadcasted_iota(jnp.int32, (bl, 128), 0).astype(F32)
        for slot in range(N_HEADS):
            slope = jnp.full((bl, 128), ALIBI_SLOPES[N_HEADS + HEAD_OF_SLOT[slot]], F32)
            qc = (lanes3(ROW_LANE, _split3(-slope * rowf)) + lanes3(COL_LANE, _split3(slope))
                  + lanes3(BLK_LANE, _split3(-slope)))
            qconst_ref[slot * bl:(slot + 1) * bl, :] = qc.astype(BF16)

    def fold(s):
        return jnp.maximum(s[:, :128], s[:, 128:])

    def run_step(t_a):
        tiles_t = (t_a, t_a + 1)
        past = [(side, j) for j in range(t_a + 1) for side in range(2) if j < tiles_t[side]]

        _swa_tile(t_a == 0, sink_ref, qa_ref, kap_ref, kac_ref, vap_ref, vac_ref, oa_ref,
                  swa_qconst_ref, swa_bias_ref)

        for side, t in enumerate(tiles_t):
            q = q_ref[t * bl:(t + 1) * bl, :]
            selb = selb_ref[side * bl:(side + 1) * bl, :].astype(F32)
            for slot in range(N_HEADS):
                pair, half = divmod(slot, 2)
                qp = q[:, pair * 128:(pair + 1) * 128]
                keep = (lane < 64) if half == 0 else (lane >= 64)
                qm = jnp.where(keep, qp, jnp.zeros_like(qp))
                extra = jnp.where(sel_lane & (lane_slot == slot), selb,
                                  qconst_ref[slot * bl:(slot + 1) * bl, :].astype(F32))
                qaug_ref[side, slot * bl:(slot + 1) * bl, :] = jnp.concatenate(
                    [qm, extra.astype(BF16)], axis=1)

        key_col = lax.broadcasted_iota(jnp.int32, (bl, 128), 0).astype(F32)
        alibi_lanes = jnp.where(in3(ROW_LANE), 1.0, jnp.where(in3(COL_LANE), key_col, 0.0))

        def scores(side, j):
            t = tiles_t[side]
            if j == t:
                extra = alibi_lanes
            else:
                extra = jnp.where(sel_lane, jnp.where(lane_blk == j, 1.0, 0.0),
                                  jnp.where(in3(BLK_LANE), float((t - j) * bl), alibi_lanes))
            kaug = jnp.concatenate([k_ref[j * bl:(j + 1) * bl, :], extra.astype(BF16)], axis=1)
            return lax.dot_general(qaug_ref[side], kaug, (((1,), (1,)), ((), ())),
                                   preferred_element_type=F32)

        def probs_v(side, blocks):
            m = mpart_ref[side]
            pb = jnp.concatenate([jnp.exp2(u_ref[n, :, c:c + 128] - m) for n, _ in blocks for c in (0, 128)],
                                 axis=1).astype(BF16)
            vj = jnp.concatenate([v_ref[j * bl:(j + 1) * bl, :] for _, j in blocks], axis=0)
            return jnp.dot(pb, jnp.concatenate([vj, jnp.ones_like(vj)], axis=1),
                           preferred_element_type=F32)

        causal = causal_ref[...]
        for side, t in enumerate(tiles_t):
            s_own = (scores(side, t).reshape(N_HEADS, bl, bl) + causal[None]).reshape(rows, bl)
            u_ref[side] = s_own
            mpart_ref[side] = fold(s_own)
        for n, (side, j) in enumerate(past):
            sj = scores(side, j)
            u_ref[2 + n] = sj
            mpart_ref[side] = jnp.maximum(mpart_ref[side], fold(sj))

        for side in range(2):
            mpart_ref[side] = jnp.broadcast_to(jnp.max(mpart_ref[side], axis=-1, keepdims=True),
                                               (rows, 128))

        stored = [[(side, tiles_t[side])] + [(2 + n, j) for n, (s, j) in enumerate(past) if s == side]
                  for side in range(2)]
        pairs = [[blocks[i:i + PV_GROUP] for i in range(0, len(blocks), PV_GROUP)] for blocks in stored]
        for g in range(max(len(p) for p in pairs)):
            for side in range(2):
                if g < len(pairs[side]):
                    if g == 0:
                        acc_ref[side] = probs_v(side, pairs[side][g])
                    else:
                        acc_ref[side] += probs_v(side, pairs[side][g])

        for side, t in enumerate(tiles_t):
            out = acc_ref[side, :, :128] / acc_ref[side, :, 128:]
            heads = []
            for pair in range(4):
                lo = out[(2 * pair) * bl:(2 * pair + 1) * bl, :]
                hi = out[(2 * pair + 1) * bl:(2 * pair + 2) * bl, :]
                heads.append(jnp.where(lane < 64, lo, hi))
            o_ref[side * bl:(side + 1) * bl, :] = jnp.concatenate(heads, axis=1).astype(BF16)

    for pair_step in range(N_MOBA_BLOCKS // 2):
        pl.when(step == pair_step)(functools.partial(run_step, 2 * pair_step))


def _attention(sinks, qa, ka, va, qb, kb, vb, selb):
    bl = MOBA_BLOCK
    w = SWA_WINDOW
    assert SWA_TILE == 2 * bl
    rows = N_HEADS * bl
    step_blocks = 2 * N_MOBA_BLOCKS - 1
    whole = lambda s, b: (b, 0, 0)
    cur = lambda s, b: (b, s, 0)
    prev = lambda s, b: (b, jnp.maximum(s * (SWA_TILE // w) - 1, 0), 0)
    out = jax.ShapeDtypeStruct((BATCH, SEQ, Q_COLS), BF16)
    return pl.pallas_call(
        _attn_kernel,
        grid=(SEQ // SWA_TILE, BATCH),
        in_specs=[pl.BlockSpec(memory_space=pltpu.SMEM),
                  pl.BlockSpec((None, SWA_TILE, Q_COLS), cur),
                  pl.BlockSpec((None, w, KV_COLS), prev),
                  pl.BlockSpec((None, SWA_TILE, KV_COLS), cur),
                  pl.BlockSpec((None, w, KV_COLS), prev),
                  pl.BlockSpec((None, SWA_TILE, KV_COLS), cur),
                  pl.BlockSpec((None, SEQ, Q_COLS), whole),
                  pl.BlockSpec((None, SEQ, KV_COLS), whole),
                  pl.BlockSpec((None, SEQ, KV_COLS), whole),
                  pl.BlockSpec((None, SWA_TILE, 128), cur)],
        out_specs=[pl.BlockSpec((None, SWA_TILE, Q_COLS), cur),
                   pl.BlockSpec((None, SWA_TILE, Q_COLS), cur)],
        out_shape=[out, out],
        scratch_shapes=[pltpu.VMEM((rows, 128), BF16),
                        pltpu.VMEM((bl, bl), F32),
                        pltpu.VMEM((2, rows, 256), BF16),
                        pltpu.VMEM((step_blocks, rows, bl), F32),
                        pltpu.VMEM((2, rows, 128), F32),
                        pltpu.VMEM((2, rows, 256), F32),
                        pltpu.VMEM((N_HEADS * w, 128), BF16),
                        pltpu.VMEM((w, 2 * w), F32)],
        compiler_params=pltpu.CompilerParams(
            dimension_semantics=("arbitrary", "arbitrary"), vmem_limit_bytes=V7X_VMEM_LIMIT_BYTES),
        name="attention",
    )(sinks, qa, ka, ka, va, va, qb, kb, vb, selb)


def _mix_ffn_kernel(x1_ref, ya_ref, yb_ref, w_mix_hbm, wa_hbm, wb_hbm, wo_hbm, g2_ref, b2_ref,
                    w_in_hbm, w_out_hbm, g3_ref, b3_ref, o_ref,
                    act_ref, wg_ref, wa_ref, wb_ref, wo_ref, w_in_ref, w_out_ref, stage_ref, sem_ref):
    @pl.when(pl.program_id(0) == 0)
    def _():
        jobs = _window_jobs(w_mix_hbm, wg_ref, D_MODEL, 2 * D_MODEL, QKV_COLS)
        slot_row = lambda r: HEAD_OF_SLOT.index(r // HEAD_DIM) * HEAD_DIM
        jobs += _window_jobs(wa_hbm, wa_ref, Q_COLS, D_MODEL, dst_row=slot_row)
        jobs += _window_jobs(wb_hbm, wb_ref, Q_COLS, D_MODEL, dst_row=slot_row)
        jobs += _window_jobs(wo_hbm, wo_ref, D_MODEL, D_MODEL)
        jobs += _window_jobs(w_in_hbm, w_in_ref, D_MODEL, 2 * D_FF)
        jobs += _window_jobs(w_out_hbm, w_out_ref, D_FF, D_MODEL, fix=_half)
        _stream_cast(jobs, stage_ref, sem_ref)

    x1 = x1_ref[...]
    x1b = x1.astype(BF16)
    ga = jnp.dot(x1b, wg_ref[:, :D_MODEL], preferred_element_type=F32)
    ya = jnp.dot(ya_ref[...], wa_ref[...], preferred_element_type=F32)
    y = jax.nn.sigmoid(ga) * ya
    gb = jnp.dot(x1b, wg_ref[:, D_MODEL:], preferred_element_type=F32)
    yb = jnp.dot(yb_ref[...], wb_ref[...], preferred_element_type=F32)
    y = y + jax.nn.sigmoid(gb) * yb
    z = jnp.dot(y.astype(BF16), wo_ref[...], preferred_element_type=F32)
    x2 = _layer_norm(ALPHA * x1 + z, g2_ref[...], b2_ref[...])
    f = _swiglu(x2.astype(BF16), w_in_ref, w_out_ref, act_ref)
    o_ref[...] = _layer_norm(ALPHA * x2 + f, g3_ref[...], b3_ref[...])


def _mix_ffn(x1, ya, yb, w_mix, wa, wb, wo, g2, b2, w_in, w_out, g3, b3):
    tm = TOKEN_TILE
    row = lambda n: pl.BlockSpec((tm, n), lambda i: (i, 0))
    hbm = pl.BlockSpec(memory_space=pl.ANY)
    ln = _const_spec(g2.shape)
    return pl.pallas_call(
        _mix_ffn_kernel,
        grid=(TOKENS // tm,),
        in_specs=[row(D_MODEL), row(Q_COLS), row(Q_COLS), hbm, hbm, hbm, hbm, ln, ln, hbm, hbm, ln, ln],
        out_specs=row(D_MODEL),
        out_shape=jax.ShapeDtypeStruct((TOKENS, D_MODEL), F32),
        scratch_shapes=[pltpu.VMEM((tm, D_FF), BF16),
                        pltpu.VMEM((D_MODEL, 2 * D_MODEL), BF16),
                        pltpu.VMEM((Q_COLS, D_MODEL), BF16),
                        pltpu.VMEM((Q_COLS, D_MODEL), BF16),
                        pltpu.VMEM((D_MODEL, D_MODEL), BF16),
                        pltpu.VMEM((D_MODEL, 2 * D_FF), BF16),
                        pltpu.VMEM((D_FF, D_MODEL), BF16),
                        pltpu.VMEM(STAGE_SHAPE, F32),
                        pltpu.SemaphoreType.DMA((STAGE_SLOTS,))],
        compiler_params=pltpu.CompilerParams(
            dimension_semantics=("arbitrary",), vmem_limit_bytes=V7X_VMEM_LIMIT_BYTES),
        name="mix_ffn",
    )(x1, ya, yb, w_mix, wa, wb, wo, g2, b2, w_in, w_out, g3, b3)


def kernel(x, ffn1_w_in, ffn1_w_out, ln1_g, ln1_b, mix_w_in, swa_sinks, w_branch_a, w_branch_b,
           mix_w_o, ln2_g, ln2_b, ffn2_w_in, ffn2_w_out, ln3_g, ln3_b):
    assert x.shape == (BATCH, SEQ, D_MODEL) and ffn1_w_in.shape == (1, D_MODEL, 2 * D_FF)
    assert mix_w_in.shape == (1, D_MODEL, MIX_IN_COLS)
    x1, qa, ka, va, qb, kb, vb = _ffn_qkv(x.reshape(TOKENS, D_MODEL), ffn1_w_in, ffn1_w_out,
                                          ln1_g, ln1_b, mix_w_in)
    seq3 = lambda a: a.reshape(BATCH, SEQ, a.shape[-1])
    selb = _moba_select(seq3(qb), seq3(kb))
    ya, yb = _attention(swa_sinks[0], seq3(qa), seq3(ka), seq3(va), seq3(qb), seq3(kb), seq3(vb), selb)
    out = _mix_ffn(x1, ya.reshape(TOKENS, Q_COLS), yb.reshape(TOKENS, Q_COLS), mix_w_in,
                   w_branch_a, w_branch_b, mix_w_o, ln2_g, ln2_b, ffn2_w_in, ffn2_w_out, ln3_g, ln3_b)
    return out.reshape(BATCH, SEQ, D_MODEL)
```

```python
import functools

import jax
import jax.numpy as jnp
from jax import lax
from jax.experimental import pallas as pl
from jax.experimental.pallas import tpu as pltpu

F32 = jnp.float32
BF16 = jnp.bfloat16

D_MODEL = 1024
BATCH = 8
SEQ = 2048
TOKENS = BATCH * SEQ
HEAD_DIM = 64
N_HEADS = 8
KV_COLS = 128
Q_COLS = 512
QKV_COLS = 2 * (Q_COLS + 2 * KV_COLS)
MIX_IN_COLS = QKV_COLS + 2 * D_MODEL
SWA_WINDOW = 128
SWA_TILE = 512
MOBA_BLOCK = 256
MOBA_TOPK = 3
N_MOBA_BLOCKS = SEQ // MOBA_BLOCK
PV_GROUP = 2
D_FF = 2816
FFN_CHUNK = 256
N_FFN_CHUNKS = D_FF // FFN_CHUNK
ALPHA = 2.0 ** 0.25
LN_EPS = 1e-5
NEG = -1e30
LOG2E = 1.4426950408889634
Q_SCALE = 0.125 * LOG2E
ALIBI_SLOPES = tuple(float(2.0 ** (-8.0 * i / 16.0)) * LOG2E for i in range(1, 17))

HEAD_OF_SLOT = (0, 4, 1, 5, 2, 6, 3, 7)

V7X_VMEM_LIMIT_BYTES = 60 * 1024 * 1024
TOKEN_TILE = 1024

SEL_LANES = 64
ROW_LANE = 64
COL_LANE = 67
BLK_LANE = 70


def _const_spec(shape):
    zeros = (0,) * len(shape)
    return pl.BlockSpec(shape, lambda *_: zeros, pipeline_mode=pl.Buffered(1))


def _split3(x):
    hi = x.astype(BF16)
    r1 = x - hi.astype(F32)
    mid = r1.astype(BF16)
    lo = (r1 - mid.astype(F32)).astype(BF16)
    return hi, mid, lo


def _layer_norm(y, g, b):
    mu = jnp.mean(y, axis=-1, keepdims=True)
    yc = y - mu
    var = jnp.mean(yc * yc, axis=-1, keepdims=True)
    return yc * lax.rsqrt(var + LN_EPS) * g + b


def _half(w):
    return 0.5 * w


def _swiglu(xb, w_in_ref, w_out_ref, g_ref):
    for c in range(N_FFN_CHUNKS):
        lo = c * FFN_CHUNK
        a = jnp.dot(xb, w_in_ref[:, lo:lo + FFN_CHUNK], preferred_element_type=F32)
        u = jnp.dot(xb, w_in_ref[:, D_FF + lo:D_FF + lo + FFN_CHUNK], preferred_element_type=F32)
        g_ref[:, lo:lo + FFN_CHUNK] = (a * jax.nn.sigmoid(a) * u).astype(BF16)
    return jnp.dot(g_ref[...], w_out_ref[...], preferred_element_type=F32)


STAGE_ROWS = 256
STAGE_COLS = 512
STAGE_SLOTS = 6
STAGE_SHAPE = (STAGE_SLOTS, STAGE_ROWS, STAGE_COLS)


def _window_jobs(src_hbm, dst_ref, n_rows, n_cols, src_col0=0, dst_col0=0, fix=None, dst_row=None):
    jobs = []
    for r in range(0, n_rows, STAGE_ROWS):
        for c in range(0, n_cols, STAGE_COLS):
            w = min(STAGE_COLS, n_cols - c)
            src = src_hbm.at[0, pl.ds(r, STAGE_ROWS), pl.ds(src_col0 + c, w)]

            def store(tile, r=r, c=c, w=w):
                out = (fix(tile) if fix else tile).astype(BF16)
                cols = slice(dst_col0 + c, dst_col0 + c + w)
                if dst_row is None:
                    dst_ref[r:r + STAGE_ROWS, cols] = out
                else:
                    for b in range(0, STAGE_ROWS, HEAD_DIM):
                        dst_ref[dst_row(r + b):dst_row(r + b) + HEAD_DIM, cols] = out[b:b + HEAD_DIM]

            jobs.append((src, w, store))
    return jobs


def _stream_cast(jobs, stage_ref, sem_ref):
    def copy(n):
        src, w, _ = jobs[n]
        slot = n % STAGE_SLOTS
        dst = stage_ref.at[slot] if w == STAGE_COLS else stage_ref.at[slot, :, pl.ds(0, w)]
        return pltpu.make_async_copy(src, dst, sem_ref.at[slot])

    ahead = STAGE_SLOTS - 1
    for n in range(min(ahead, len(jobs))):
        copy(n).start()
    for n, (_, w, store) in enumerate(jobs):
        if n + ahead < len(jobs):
            copy(n + ahead).start()
        copy(n).wait()
        store(stage_ref[n % STAGE_SLOTS, :, 0:w])


def _to_slot_order(q):
    lane = lax.broadcasted_iota(jnp.int32, (q.shape[0], 128), 1)
    t = [q[:, i * 128:(i + 1) * 128] for i in range(4)]
    swap = lambda a: pltpu.roll(a, 64, 1)
    return jnp.concatenate([jnp.where(lane < 64, t[0], swap(t[2])), jnp.where(lane < 64, swap(t[0]), t[2]),
                            jnp.where(lane < 64, t[1], swap(t[3])), jnp.where(lane < 64, swap(t[1]), t[3])],
                           axis=1)


def _ffn_qkv_kernel(x_ref, w_in_hbm, w_out_hbm, g_ref, b_ref, w_mix_hbm,
                    x1_ref, qa_ref, ka_ref, va_ref, qb_ref, kb_ref, vb_ref,
                    act_ref, w_in_ref, w_out_ref, wqkv_ref, stage_ref, sem_ref):
    @pl.when(pl.program_id(0) == 0)
    def _():
        jobs = _window_jobs(w_in_hbm, w_in_ref, D_MODEL, 2 * D_FF)
        jobs += _window_jobs(w_out_hbm, w_out_ref, D_FF, D_MODEL, fix=_half)
        for col0, n_cols, fix in ((0, Q_COLS, _to_slot_order), (Q_COLS, 2 * KV_COLS, None),
                                  (QKV_COLS // 2, Q_COLS, _to_slot_order),
                                  (QKV_COLS // 2 + Q_COLS, 2 * KV_COLS, None)):
            jobs += _window_jobs(w_mix_hbm, wqkv_ref, D_MODEL, n_cols, col0, col0, fix)
        _stream_cast(jobs, stage_ref, sem_ref)

    x = x_ref[...]
    f = _swiglu(x.astype(BF16), w_in_ref, w_out_ref, act_ref)
    x1 = _layer_norm(ALPHA * x + f, g_ref[...], b_ref[...])
    x1_ref[...] = x1
    h = jnp.dot(x1.astype(BF16), wqkv_ref[...], preferred_element_type=F32)
    qa_ref[...] = (h[:, 0:512] * Q_SCALE).astype(BF16)
    ka_ref[...] = h[:, 512:640].astype(BF16)
    va_ref[...] = h[:, 640:768].astype(BF16)
    qb_ref[...] = (h[:, 768:1280] * Q_SCALE).astype(BF16)
    kb_ref[...] = h[:, 1280:1408].astype(BF16)
    vb_ref[...] = h[:, 1408:1536].astype(BF16)


def _ffn_qkv(x, w_in, w_out, g, b, w_mix):
    tm = TOKEN_TILE
    row = lambda n: pl.BlockSpec((tm, n), lambda i: (i, 0))
    hbm = pl.BlockSpec(memory_space=pl.ANY)
    out_cols = (D_MODEL, Q_COLS, KV_COLS, KV_COLS, Q_COLS, KV_COLS, KV_COLS)
    out_dtypes = (F32,) + (BF16,) * 6
    return pl.pallas_call(
        _ffn_qkv_kernel,
        grid=(TOKENS // tm,),
        in_specs=[row(D_MODEL), hbm, hbm, _const_spec(g.shape), _const_spec(b.shape), hbm],
        out_specs=[row(n) for n in out_cols],
        out_shape=[jax.ShapeDtypeStruct((TOKENS, n), dt) for n, dt in zip(out_cols, out_dtypes)],
        scratch_shapes=[pltpu.VMEM((tm, D_FF), BF16),
                        pltpu.VMEM((D_MODEL, 2 * D_FF), BF16),
                        pltpu.VMEM((D_FF, D_MODEL), BF16),
                        pltpu.VMEM((D_MODEL, QKV_COLS), BF16),
                        pltpu.VMEM(STAGE_SHAPE, F32),
                        pltpu.SemaphoreType.DMA((STAGE_SLOTS,))],
        compiler_params=pltpu.CompilerParams(
            dimension_semantics=("arbitrary",), vmem_limit_bytes=V7X_VMEM_LIMIT_BYTES),
        name="ffn_qkv",
    )(x, w_in, w_out, g, b, w_mix)


SWA_ROW_SLOTS = (0, 2, 4, 6, 1, 3, 5, 7)


def _swa_consts(qconst_ref, bias_ref):
    w = SWA_WINDOW
    lane = lax.broadcasted_iota(jnp.int32, (w, 128), 1)
    rowf = lax.broadcasted_iota(jnp.int32, (w, 128), 0).astype(F32)
    for rb, slot in enumerate(SWA_ROW_SLOTS):
        slope = jnp.full((w, 128), ALIBI_SLOPES[HEAD_OF_SLOT[slot]], F32)
        qc = jnp.zeros((w, 128), F32)
        for first, terms in ((ROW_LANE, _split3(-slope * (rowf + w))), (COL_LANE, _split3(slope))):
            for n, term in enumerate(terms):
                qc = jnp.where(lane == first + n, term.astype(F32), qc)
        qconst_ref[rb * w:(rb + 1) * w, :] = qc.astype(BF16)
    dist = (lax.broadcasted_iota(jnp.int32, (w, 2 * w), 0) + w
            - lax.broadcasted_iota(jnp.int32, (w, 2 * w), 1))
    bias_ref[...] = jnp.where((dist >= 0) & (dist < w), 0.0, NEG)


def _swa_tile(first_tile, sink_ref, q_ref, kp_ref, kc_ref, vp_ref, vc_ref, o_ref, qconst_ref, bias_ref):
    w = SWA_WINDOW
    rows = N_HEADS * w
    lane = lax.broadcasted_iota(jnp.int32, (w, 128), 1)
    k_all = jnp.concatenate([kp_ref[...], kc_ref[...]], axis=0)
    v_all = jnp.concatenate([vp_ref[...], vc_ref[...]], axis=0)
    key_col = lax.broadcasted_iota(jnp.int32, (2 * w, 128), 0).astype(F32)
    lane2 = lax.broadcasted_iota(jnp.int32, (2 * w, 128), 1)
    k_extra = jnp.where((lane2 >= ROW_LANE) & (lane2 < ROW_LANE + 3), 1.0,
                        jnp.where((lane2 >= COL_LANE) & (lane2 < COL_LANE + 3), key_col, 0.0)).astype(BF16)
    sink = jnp.concatenate([jnp.full((w, 128), sink_ref[HEAD_OF_SLOT[slot]] * LOG2E, F32)
                            for slot in SWA_ROW_SLOTS], axis=0)
    ones = jnp.ones((2 * w, 128), BF16)
    bias = bias_ref[...]
    kcol = lax.broadcasted_iota(jnp.int32, (w, 2 * w), 1)
    bias_first = jnp.where(kcol >= w, bias, NEG) if first_tile else bias

    for r in range(SWA_TILE // w):
        q = q_ref[r * w:(r + 1) * w, :]
        q_aug = jnp.concatenate(
            [jnp.concatenate([q[:, (slot // 2) * 128:(slot // 2 + 1) * 128],
                              qconst_ref[rb * w:(rb + 1) * w, :]], axis=1)
             for rb, slot in enumerate(SWA_ROW_SLOTS)], axis=0)
        k = k_all[r * w:(r + 2) * w, :]
        zero = jnp.zeros_like(k)
        nt = (((1,), (1,)), ((), ()))
        s0 = lax.dot_general(q_aug[:rows // 2], jnp.concatenate([jnp.where(lane2 < 64, k, zero), k_extra], axis=1),
                             nt, preferred_element_type=F32)
        s1 = lax.dot_general(q_aug[rows // 2:], jnp.concatenate([jnp.where(lane2 >= 64, k, zero), k_extra], axis=1),
                             nt, preferred_element_type=F32)
        b = bias_first if r == 0 else bias
        s = (jnp.concatenate([s0, s1], axis=0).reshape(N_HEADS, w, 2 * w) + b[None]).reshape(rows, 2 * w)
        row_max = jnp.max(jnp.maximum(s[:, :128], s[:, 128:]), axis=-1, keepdims=True)
        m = jnp.maximum(jnp.broadcast_to(row_max, (rows, 128)), sink)
        p = jnp.exp2(s - jnp.concatenate([m, m], axis=1)).astype(BF16)
        v_aug = jnp.concatenate([v_all[r * w:(r + 2) * w, :], ones], axis=1)
        pv = jnp.dot(p, v_aug, preferred_element_type=F32)
        out = pv[:, :128] / (pv[:, 128:] + jnp.exp2(sink - m))
        tiles = []
        for pair in range(4):
            lo = out[pair * w:(pair + 1) * w, :]
            hi = out[(4 + pair) * w:(5 + pair) * w, :]
            tiles.append(jnp.where(lane < 64, lo, hi))
        o_ref[r * w:(r + 1) * w, :] = jnp.concatenate(tiles, axis=1).astype(BF16)


def _attn_kernel(sink_ref, qa_ref, kap_ref, kac_ref, vap_ref, vac_ref, q_ref, k_ref, v_ref,
                 oa_ref, o_ref,
                 selb_ref, qconst_ref, causal_ref, qaug_ref, u_ref, mpart_ref, acc_ref,
                 swa_qconst_ref, swa_bias_ref):
    step = pl.program_id(1)
    bl = MOBA_BLOCK
    nbk = N_MOBA_BLOCKS
    rows = N_HEADS * bl
    lane = lax.broadcasted_iota(jnp.int32, (bl, 128), 1)
    lane_blk = lane % nbk
    lane_slot = lane // nbk
    sel_lane = lane < SEL_LANES
    in3 = lambda first: (lane >= first) & (lane < first + 3)

    def lanes3(first, terms):
        out = jnp.zeros((bl, 128), F32)
        for i, term in enumerate(terms):
            out = jnp.where(lane == first + i, term.astype(F32), out)
        return out

    @pl.when((pl.program_id(0) == 0) & (step == 0))
    def _():
        _swa_consts(swa_qconst_ref, swa_bias_ref)
        causal_ref[...] = jnp.where(lax.broadcasted_iota(jnp.int32, (bl, bl), 0)
                                    >= lax.broadcasted_iota(jnp.int32, (bl, bl), 1), 0.0, NEG)
        rowf = lax.broadcasted_iota(jnp.int32, (bl, 128), 0).astype(F32)
        for slot in range(N_HEADS):
            slope = jnp.full((bl, 128), ALIBI_SLOPES[N_HEADS + HEAD_OF_SLOT[slot]], F32)
            qc = (lanes3(ROW_LANE, _split3(-slope * rowf)) + lanes3(COL_LANE, _split3(slope))
                  + lanes3(BLK_LANE, _split3(-slope)))
            qconst_ref[slot * bl:(slot + 1) * bl, :] = qc.astype(BF16)

    @pl.when(step == 0)
    def _():
        rsel = lax.broadcasted_iota(jnp.int32, (128, SEQ), 0)
        tsel = lax.broadcasted_iota(jnp.int32, (128, SEQ), 1)
        avg = jnp.where((rsel % nbk) == (tsel // bl), 1.0 / bl, 0.0).astype(BF16)
        kmean_rows = jnp.dot(avg, k_ref[...], preferred_element_type=F32)
        kmean_t = kmean_rows.T
        r128 = lax.broadcasted_iota(jnp.int32, (128, 128), 0)
        c128 = lax.broadcasted_iota(jnp.int32, (128, 128), 1)
        km = jnp.concatenate(
            [jnp.where((c128 // nbk) == 2 * pair + (r128 >= 64).astype(jnp.int32), kmean_t, 0.0)
             for pair in range(4)], axis=0)
        q_all = q_ref[...]
        gate = sum(jnp.dot(q_all, term, preferred_element_type=F32) for term in _split3(km)[:2])
        gate_t = gate.T[:SEL_LANES].reshape(N_HEADS, nbk, SEQ)
        blk = lax.broadcasted_iota(jnp.int32, (N_HEADS, nbk, SEQ), 1)
        own = lax.broadcasted_iota(jnp.int32, (N_HEADS, nbk, SEQ), 2) // bl
        past = blk < own
        gm = jnp.where(past, gate_t, -jnp.inf)
        rank = jnp.zeros((N_HEADS, nbk, SEQ), jnp.int32)
        for d in range(1, nbk):
            partner = pltpu.roll(gm, nbk - d, 1)
            wrapped = blk + d >= nbk
            beats = (partner > gm) | ((partner == gm) & wrapped)
            rank = rank + beats.astype(jnp.int32)
        selected = past & (rank < MOBA_TOPK)
        selb_t = jnp.where(selected, 0.0, NEG).reshape(SEL_LANES, SEQ)
        selb_ref[...] = jnp.concatenate([selb_t, jnp.zeros_like(selb_t)], axis=0).T

    def fold(s):
        return jnp.maximum(s[:, :128], s[:, 128:])

    def run_step(t_a):
        tiles_t = (t_a, t_a + 1)
        past = [(side, j) for j in range(t_a + 1) for side in range(2) if j < tiles_t[side]]

        _swa_tile(t_a == 0, sink_ref, qa_ref, kap_ref, kac_ref, vap_ref, vac_ref, oa_ref,
                  swa_qconst_ref, swa_bias_ref)

        for side, t in enumerate(tiles_t):
            q = q_ref[t * bl:(t + 1) * bl, :]
            selb = selb_ref[t * bl:(t + 1) * bl, :]
            for slot in range(N_HEADS):
                pair, half = divmod(slot, 2)
                qp = q[:, pair * 128:(pair + 1) * 128]
                keep = (lane < 64) if half == 0 else (lane >= 64)
                qm = jnp.where(keep, qp, jnp.zeros_like(qp))
                extra = jnp.where(sel_lane & (lane_slot == slot), selb,
                                  qconst_ref[slot * bl:(slot + 1) * bl, :].astype(F32))
                qaug_ref[side, slot * bl:(slot + 1) * bl, :] = jnp.concatenate(
                    [qm, extra.astype(BF16)], axis=1)

        key_col = lax.broadcasted_iota(jnp.int32, (bl, 128), 0).astype(F32)
        alibi_lanes = jnp.where(in3(ROW_LANE), 1.0, jnp.where(in3(COL_LANE), key_col, 0.0))

        def scores(side, j):
            t = tiles_t[side]
            if j == t:
                extra = alibi_lanes
            else:
                extra = jnp.where(sel_lane, jnp.where(lane_blk == j, 1.0, 0.0),
                                  jnp.where(in3(BLK_LANE), float((t - j) * bl), alibi_lanes))
            kaug = jnp.concatenate([k_ref[j * bl:(j + 1) * bl, :], extra.astype(BF16)], axis=1)
            return lax.dot_general(qaug_ref[side], kaug, (((1,), (1,)), ((), ())),
                                   preferred_element_type=F32)

        def probs_v(side, blocks):
            m = mpart_ref[side]
            pb = jnp.concatenate([jnp.exp2(u_ref[n, :, c:c + 128].astype(F32) - m)
                                  for n, _ in blocks for c in (0, 128)], axis=1).astype(BF16)
            vj = jnp.concatenate([v_ref[j * bl:(j + 1) * bl, :] for _, j in blocks], axis=0)
            return jnp.dot(pb, jnp.concatenate([vj, jnp.ones_like(vj)], axis=1),
                           preferred_element_type=F32)

        causal = causal_ref[...]
        for side, t in enumerate(tiles_t):
            s_own = (scores(side, t).reshape(N_HEADS, bl, bl) + causal[None]).reshape(rows, bl)
            u_ref[side] = s_own.astype(BF16)
            mpart_ref[side] = fold(s_own)
        for n, (side, j) in enumerate(past):
            sj = scores(side, j)
            u_ref[2 + n] = sj.astype(BF16)
            mpart_ref[side] = jnp.maximum(mpart_ref[side], fold(sj))

        for side in range(2):
            mpart_ref[side] = jnp.broadcast_to(jnp.max(mpart_ref[side], axis=-1, keepdims=True),
                                               (rows, 128))

        stored = [[(side, tiles_t[side])] + [(2 + n, j) for n, (s, j) in enumerate(past) if s == side]
                  for side in range(2)]
        pairs = [[blocks[i:i + PV_GROUP] for i in range(0, len(blocks), PV_GROUP)] for blocks in stored]
        for g in range(max(len(p) for p in pairs)):
            for side in range(2):
                if g < len(pairs[side]):
                    if g == 0:
                        acc_ref[side] = probs_v(side, pairs[side][g])
                    else:
                        acc_ref[side] += probs_v(side, pairs[side][g])

        for side, t in enumerate(tiles_t):
            out = acc_ref[side, :, :128] / acc_ref[side, :, 128:]
            heads = []
            for pair in range(4):
                lo = out[(2 * pair) * bl:(2 * pair + 1) * bl, :]
                hi = out[(2 * pair + 1) * bl:(2 * pair + 2) * bl, :]
                heads.append(jnp.where(lane < 64, lo, hi))
            o_ref[t * bl:(t + 1) * bl, :] = jnp.concatenate(heads, axis=1).astype(BF16)

    for pair_step in range(N_MOBA_BLOCKS // 2):
        pl.when(step == pair_step)(functools.partial(run_step, 2 * pair_step))


def _attention(sinks, qa, ka, va, qb, kb, vb):
    bl = MOBA_BLOCK
    w = SWA_WINDOW
    assert SWA_TILE == 2 * bl
    rows = N_HEADS * bl
    step_blocks = 2 * N_MOBA_BLOCKS - 1
    whole = lambda b, s: (b, 0, 0)
    cur = lambda b, s: (b, s, 0)
    prev = lambda b, s: (b, jnp.maximum(s * (SWA_TILE // w) - 1, 0), 0)
    out = jax.ShapeDtypeStruct((BATCH, SEQ, Q_COLS), BF16)
    return pl.pallas_call(
        _attn_kernel,
        grid=(BATCH, SEQ // SWA_TILE),
        in_specs=[pl.BlockSpec(memory_space=pltpu.SMEM),
                  pl.BlockSpec((None, SWA_TILE, Q_COLS), cur),
                  pl.BlockSpec((None, w, KV_COLS), prev),
                  pl.BlockSpec((None, SWA_TILE, KV_COLS), cur),
                  pl.BlockSpec((None, w, KV_COLS), prev),
                  pl.BlockSpec((None, SWA_TILE, KV_COLS), cur),
                  pl.BlockSpec((None, SEQ, Q_COLS), whole),
                  pl.BlockSpec((None, SEQ, KV_COLS), whole),
                  pl.BlockSpec((None, SEQ, KV_COLS), whole)],
        out_specs=[pl.BlockSpec((None, SWA_TILE, Q_COLS), cur),
                   pl.BlockSpec((None, SEQ, Q_COLS), whole)],
        out_shape=[out, out],
        scratch_shapes=[pltpu.VMEM((SEQ, 128), F32),
                        pltpu.VMEM((rows, 128), BF16),
                        pltpu.VMEM((bl, bl), F32),
                        pltpu.VMEM((2, rows, 256), BF16),
                        pltpu.VMEM((step_blocks, rows, bl), BF16),
                        pltpu.VMEM((2, rows, 128), F32),
                        pltpu.VMEM((2, rows, 256), F32),
                        pltpu.VMEM((N_HEADS * w, 128), BF16),
                        pltpu.VMEM((w, 2 * w), F32)],
        compiler_params=pltpu.CompilerParams(
            dimension_semantics=("arbitrary", "arbitrary"), vmem_limit_bytes=V7X_VMEM_LIMIT_BYTES),
        name="attention",
    )(sinks, qa, ka, ka, va, va, qb, kb, vb)


def _mix_ffn_kernel(x1_ref, ya_ref, yb_ref, w_mix_hbm, wa_hbm, wb_hbm, wo_hbm, g2_ref, b2_ref,
                    w_in_hbm, w_out_hbm, g3_ref, b3_ref, o_ref,
                    act_ref, wg_ref, wa_ref, wb_ref, wo_ref, w_in_ref, w_out_ref, stage_ref, sem_ref):
    @pl.when(pl.program_id(0) == 0)
    def _():
        jobs = _window_jobs(w_mix_hbm, wg_ref, D_MODEL, 2 * D_MODEL, QKV_COLS)
        slot_row = lambda r: HEAD_OF_SLOT.index(r // HEAD_DIM) * HEAD_DIM
        jobs += _window_jobs(wa_hbm, wa_ref, Q_COLS, D_MODEL, dst_row=slot_row)
        jobs += _window_jobs(wb_hbm, wb_ref, Q_COLS, D_MODEL, dst_row=slot_row)
        jobs += _window_jobs(wo_hbm, wo_ref, D_MODEL, D_MODEL)
        jobs += _window_jobs(w_in_hbm, w_in_ref, D_MODEL, 2 * D_FF)
        jobs += _window_jobs(w_out_hbm, w_out_ref, D_FF, D_MODEL, fix=_half)
        _stream_cast(jobs, stage_ref, sem_ref)

    x1 = x1_ref[...]
    x1b = x1.astype(BF16)
    ga = jnp.dot(x1b, wg_ref[:, :D_MODEL], preferred_element_type=F32)
    ya = jnp.dot(ya_ref[...], wa_ref[...], preferred_element_type=F32)
    y = jax.nn.sigmoid(ga) * ya
    gb = jnp.dot(x1b, wg_ref[:, D_MODEL:], preferred_element_type=F32)
    yb = jnp.dot(yb_ref[...], wb_ref[...], preferred_element_type=F32)
    y = y + jax.nn.sigmoid(gb) * yb
    z = jnp.dot(y.astype(BF16), wo_ref[...], preferred_element_type=F32)
    x2 = _layer_norm(ALPHA * x1 + z, g2_ref[...], b2_ref[...])
    f = _swiglu(x2.astype(BF16), w_in_ref, w_out_ref, act_ref)
    o_ref[...] = _layer_norm(ALPHA * x2 + f, g3_ref[...], b3_ref[...])


def _mix_ffn(x1, ya, yb, w_mix, wa, wb, wo, g2, b2, w_in, w_out, g3, b3):
    tm = TOKEN_TILE
    row = lambda n: pl.BlockSpec((tm, n), lambda i: (i, 0))
    hbm = pl.BlockSpec(memory_space=pl.ANY)
    ln = _const_spec(g2.shape)
    return pl.pallas_call(
        _mix_ffn_kernel,
        grid=(TOKENS // tm,),
        in_specs=[row(D_MODEL), row(Q_COLS), row(Q_COLS), hbm, hbm, hbm, hbm, ln, ln, hbm, hbm, ln, ln],
        out_specs=row(D_MODEL),
        out_shape=jax.ShapeDtypeStruct((TOKENS, D_MODEL), F32),
        scratch_shapes=[pltpu.VMEM((tm, D_FF), BF16),
                        pltpu.VMEM((D_MODEL, 2 * D_MODEL), BF16),
                        pltpu.VMEM((Q_COLS, D_MODEL), BF16),
                        pltpu.VMEM((Q_COLS, D_MODEL), BF16),
                        pltpu.VMEM((D_MODEL, D_MODEL), BF16),
                        pltpu.VMEM((D_MODEL, 2 * D_FF), BF16),
                        pltpu.VMEM((D_FF, D_MODEL), BF16),
                        pltpu.VMEM(STAGE_SHAPE, F32),
                        pltpu.SemaphoreType.DMA((STAGE_SLOTS,))],
        compiler_params=pltpu.CompilerParams(
            dimension_semantics=("arbitrary",), vmem_limit_bytes=V7X_VMEM_LIMIT_BYTES),
        name="mix_ffn",
    )(x1, ya, yb, w_mix, wa, wb, wo, g2, b2, w_in, w_out, g3, b3)


def kernel(x, ffn1_w_in, ffn1_w_out, ln1_g, ln1_b, mix_w_in, swa_sinks, w_branch_a, w_branch_b,
           mix_w_o, ln2_g, ln2_b, ffn2_w_in, ffn2_w_out, ln3_g, ln3_b):
    assert x.shape == (BATCH, SEQ, D_MODEL) and ffn1_w_in.shape == (1, D_MODEL, 2 * D_FF)
    assert mix_w_in.shape == (1, D_MODEL, MIX_IN_COLS)
    x1, qa, ka, va, qb, kb, vb = _ffn_qkv(x.reshape(TOKENS, D_MODEL), ffn1_w_in, ffn1_w_out,
                                          ln1_g, ln1_b, mix_w_in)
    seq3 = lambda a: a.reshape(BATCH, SEQ, a.shape[-1])
    ya, yb = _attention(swa_sinks[0], seq3(qa), seq3(ka), seq3(va), seq3(qb), seq3(kb), seq3(vb))
    out = _mix_ffn(x1, ya.reshape(TOKENS, Q_COLS), yb.reshape(TOKENS, Q_COLS), mix_w_in,
                   w_branch_a, w_branch_b, mix_w_o, ln2_g, ln2_b, ffn2_w_in, ffn2_w_out, ln3_g, ln3_b)
    return out.reshape(BATCH, SEQ, D_MODEL)
```

```python
import functools

import jax
import jax.numpy as jnp
from jax import lax
from jax.experimental import pallas as pl
from jax.experimental.pallas import tpu as pltpu

F32 = jnp.float32
BF16 = jnp.bfloat16

D_MODEL = 1024
BATCH = 8
SEQ = 2048
TOKENS = BATCH * SEQ
HEAD_DIM = 64
N_HEADS = 8
KV_COLS = 128
Q_COLS = 512
QKV_COLS = 2 * (Q_COLS + 2 * KV_COLS)
MIX_IN_COLS = QKV_COLS + 2 * D_MODEL
SWA_WINDOW = 128
SWA_TILE = 512
MOBA_BLOCK = 256
MOBA_TOPK = 3
N_MOBA_BLOCKS = SEQ // MOBA_BLOCK
PV_GROUP = 2
D_FF = 2816
FFN_CHUNK = 256
N_FFN_CHUNKS = D_FF // FFN_CHUNK
ALPHA = 2.0 ** 0.25
LN_EPS = 1e-5
NEG = -1e30
LOG2E = 1.4426950408889634
Q_SCALE = 0.125 * LOG2E
ALIBI_SLOPES = tuple(float(2.0 ** (-8.0 * i / 16.0)) * LOG2E for i in range(1, 17))

HEAD_OF_SLOT = (0, 4, 1, 5, 2, 6, 3, 7)

V7X_VMEM_LIMIT_BYTES = 60 * 1024 * 1024
TOKEN_TILE = 1024

SEL_LANES = 64
ROW_LANE = 64
COL_LANE = 67
BLK_LANE = 70


def _const_spec(shape):
    zeros = (0,) * len(shape)
    return pl.BlockSpec(shape, lambda *_: zeros, pipeline_mode=pl.Buffered(1))


def _split3(x):
    hi = x.astype(BF16)
    r1 = x - hi.astype(F32)
    mid = r1.astype(BF16)
    lo = (r1 - mid.astype(F32)).astype(BF16)
    return hi, mid, lo


def _layer_norm(y, g, b):
    mu = jnp.mean(y, axis=-1, keepdims=True)
    yc = y - mu
    var = jnp.mean(yc * yc, axis=-1, keepdims=True)
    return yc * lax.rsqrt(var + LN_EPS) * g + b


def _half(w):
    return 0.5 * w


def _swiglu(xb, w_in_ref, w_out_ref, g_ref):
    for c in range(N_FFN_CHUNKS):
        lo = c * FFN_CHUNK
        a = jnp.dot(xb, w_in_ref[:, lo:lo + FFN_CHUNK], preferred_element_type=F32)
        u = jnp.dot(xb, w_in_ref[:, D_FF + lo:D_FF + lo + FFN_CHUNK], preferred_element_type=F32)
        g_ref[:, lo:lo + FFN_CHUNK] = (a * jax.nn.sigmoid(a) * u).astype(BF16)
    return jnp.dot(g_ref[...], w_out_ref[...], preferred_element_type=F32)


STAGE_ROWS = 256
STAGE_COLS = 512
STAGE_SLOTS = 6
STAGE_SHAPE = (STAGE_SLOTS, STAGE_ROWS, STAGE_COLS)


def _window_jobs(src_hbm, dst_ref, n_rows, n_cols, src_col0=0, dst_col0=0, fix=None, dst_row=None):
    jobs = []
    for r in range(0, n_rows, STAGE_ROWS):
        for c in range(0, n_cols, STAGE_COLS):
            w = min(STAGE_COLS, n_cols - c)
            src = src_hbm.at[0, pl.ds(r, STAGE_ROWS), pl.ds(src_col0 + c, w)]

            def store(tile, r=r, c=c, w=w):
                out = (fix(tile) if fix else tile).astype(BF16)
                cols = slice(dst_col0 + c, dst_col0 + c + w)
                if dst_row is None:
                    dst_ref[r:r + STAGE_ROWS, cols] = out
                else:
                    for b in range(0, STAGE_ROWS, HEAD_DIM):
                        dst_ref[dst_row(r + b):dst_row(r + b) + HEAD_DIM, cols] = out[b:b + HEAD_DIM]

            jobs.append((src, w, store))
    return jobs


def _stream_cast(jobs, stage_ref, sem_ref):
    def copy(n):
        src, w, _ = jobs[n]
        slot = n % STAGE_SLOTS
        dst = stage_ref.at[slot] if w == STAGE_COLS else stage_ref.at[slot, :, pl.ds(0, w)]
        return pltpu.make_async_copy(src, dst, sem_ref.at[slot])

    ahead = STAGE_SLOTS - 1
    for n in range(min(ahead, len(jobs))):
        copy(n).start()
    for n, (_, w, store) in enumerate(jobs):
        if n + ahead < len(jobs):
            copy(n + ahead).start()
        copy(n).wait()
        store(stage_ref[n % STAGE_SLOTS, :, 0:w])


def _to_slot_order(q):
    lane = lax.broadcasted_iota(jnp.int32, (q.shape[0], 128), 1)
    t = [q[:, i * 128:(i + 1) * 128] for i in range(4)]
    swap = lambda a: pltpu.roll(a, 64, 1)
    return jnp.concatenate([jnp.where(lane < 64, t[0], swap(t[2])), jnp.where(lane < 64, swap(t[0]), t[2]),
                            jnp.where(lane < 64, t[1], swap(t[3])), jnp.where(lane < 64, swap(t[1]), t[3])],
                           axis=1)


def _ffn_qkv_kernel(x_ref, w_in_hbm, w_out_hbm, g_ref, b_ref, w_mix_hbm,
                    x1_ref, qa_ref, ka_ref, va_ref, qb_ref, kb_ref, vb_ref,
                    act_ref, w_in_ref, w_out_ref, wqkv_ref, stage_ref, sem_ref):
    @pl.when(pl.program_id(0) == 0)
    def _():
        jobs = _window_jobs(w_in_hbm, w_in_ref, D_MODEL, 2 * D_FF)
        jobs += _window_jobs(w_out_hbm, w_out_ref, D_FF, D_MODEL, fix=_half)
        for col0, n_cols, fix in ((0, Q_COLS, _to_slot_order), (Q_COLS, 2 * KV_COLS, None),
                                  (QKV_COLS // 2, Q_COLS, _to_slot_order),
                                  (QKV_COLS // 2 + Q_COLS, 2 * KV_COLS, None)):
            jobs += _window_jobs(w_mix_hbm, wqkv_ref, D_MODEL, n_cols, col0, col0, fix)
        _stream_cast(jobs, stage_ref, sem_ref)

    x = x_ref[...]
    f = _swiglu(x.astype(BF16), w_in_ref, w_out_ref, act_ref)
    x1 = _layer_norm(ALPHA * x + f, g_ref[...], b_ref[...])
    x1_ref[...] = x1
    h = jnp.dot(x1.astype(BF16), wqkv_ref[...], preferred_element_type=F32)
    qa_ref[...] = (h[:, 0:512] * Q_SCALE).astype(BF16)
    ka_ref[...] = h[:, 512:640].astype(BF16)
    va_ref[...] = h[:, 640:768].astype(BF16)
    qb_ref[...] = (h[:, 768:1280] * Q_SCALE).astype(BF16)
    kb_ref[...] = h[:, 1280:1408].astype(BF16)
    vb_ref[...] = h[:, 1408:1536].astype(BF16)


def _ffn_qkv(x, w_in, w_out, g, b, w_mix):
    tm = TOKEN_TILE
    row = lambda n: pl.BlockSpec((tm, n), lambda i: (i, 0))
    hbm = pl.BlockSpec(memory_space=pl.ANY)
    out_cols = (D_MODEL, Q_COLS, KV_COLS, KV_COLS, Q_COLS, KV_COLS, KV_COLS)
    out_dtypes = (F32,) + (BF16,) * 6
    return pl.pallas_call(
        _ffn_qkv_kernel,
        grid=(TOKENS // tm,),
        in_specs=[row(D_MODEL), hbm, hbm, _const_spec(g.shape), _const_spec(b.shape), hbm],
        out_specs=[row(n) for n in out_cols],
        out_shape=[jax.ShapeDtypeStruct((TOKENS, n), dt) for n, dt in zip(out_cols, out_dtypes)],
        scratch_shapes=[pltpu.VMEM((tm, D_FF), BF16),
                        pltpu.VMEM((D_MODEL, 2 * D_FF), BF16),
                        pltpu.VMEM((D_FF, D_MODEL), BF16),
                        pltpu.VMEM((D_MODEL, QKV_COLS), BF16),
                        pltpu.VMEM(STAGE_SHAPE, F32),
                        pltpu.SemaphoreType.DMA((STAGE_SLOTS,))],
        compiler_params=pltpu.CompilerParams(
            dimension_semantics=("arbitrary",), vmem_limit_bytes=V7X_VMEM_LIMIT_BYTES),
        name="ffn_qkv",
    )(x, w_in, w_out, g, b, w_mix)


SWA_ROW_SLOTS = (0, 2, 4, 6, 1, 3, 5, 7)


def _swa_consts(qconst_ref, bias_ref):
    w = SWA_WINDOW
    lane = lax.broadcasted_iota(jnp.int32, (w, 128), 1)
    rowf = lax.broadcasted_iota(jnp.int32, (w, 128), 0).astype(F32)
    for rb, slot in enumerate(SWA_ROW_SLOTS):
        slope = jnp.full((w, 128), ALIBI_SLOPES[HEAD_OF_SLOT[slot]], F32)
        qc = jnp.zeros((w, 128), F32)
        for first, terms in ((ROW_LANE, _split3(-slope * (rowf + w))), (COL_LANE, _split3(slope))):
            for n, term in enumerate(terms):
                qc = jnp.where(lane == first + n, term.astype(F32), qc)
        qconst_ref[rb * w:(rb + 1) * w, :] = qc.astype(BF16)
    dist = (lax.broadcasted_iota(jnp.int32, (w, 2 * w), 0) + w
            - lax.broadcasted_iota(jnp.int32, (w, 2 * w), 1))
    bias_ref[...] = jnp.where((dist >= 0) & (dist < w), 0.0, NEG)


def _swa_tile(first_tile, sink_ref, q_ref, kp_ref, kc_ref, vp_ref, vc_ref, o_ref, qconst_ref, bias_ref):
    w = SWA_WINDOW
    rows = N_HEADS * w
    lane = lax.broadcasted_iota(jnp.int32, (w, 128), 1)
    k_all = jnp.concatenate([kp_ref[...], kc_ref[...]], axis=0)
    v_all = jnp.concatenate([vp_ref[...], vc_ref[...]], axis=0)
    key_col = lax.broadcasted_iota(jnp.int32, (2 * w, 128), 0).astype(F32)
    lane2 = lax.broadcasted_iota(jnp.int32, (2 * w, 128), 1)
    k_extra = jnp.where((lane2 >= ROW_LANE) & (lane2 < ROW_LANE + 3), 1.0,
                        jnp.where((lane2 >= COL_LANE) & (lane2 < COL_LANE + 3), key_col, 0.0)).astype(BF16)
    sink = jnp.concatenate([jnp.full((w, 128), sink_ref[HEAD_OF_SLOT[slot]] * LOG2E, F32)
                            for slot in SWA_ROW_SLOTS], axis=0)
    ones = jnp.ones((2 * w, 128), BF16)
    bias = bias_ref[...]
    kcol = lax.broadcasted_iota(jnp.int32, (w, 2 * w), 1)
    bias_first = jnp.where(kcol >= w, bias, NEG) if first_tile else bias

    def sub_block(r):
        q = q_ref[r * w:(r + 1) * w, :]
        q_aug = jnp.concatenate(
            [jnp.concatenate([q[:, (slot // 2) * 128:(slot // 2 + 1) * 128],
                              qconst_ref[rb * w:(rb + 1) * w, :]], axis=1)
             for rb, slot in enumerate(SWA_ROW_SLOTS)], axis=0)
        k = k_all[r * w:(r + 2) * w, :]
        zero = jnp.zeros_like(k)
        nt = (((1,), (1,)), ((), ()))
        s0 = lax.dot_general(q_aug[:rows // 2], jnp.concatenate([jnp.where(lane2 < 64, k, zero), k_extra], axis=1),
                             nt, preferred_element_type=F32)
        s1 = lax.dot_general(q_aug[rows // 2:], jnp.concatenate([jnp.where(lane2 >= 64, k, zero), k_extra], axis=1),
                             nt, preferred_element_type=F32)
        b = bias_first if r == 0 else bias
        s = (jnp.concatenate([s0, s1], axis=0).reshape(N_HEADS, w, 2 * w) + b[None]).reshape(rows, 2 * w)
        row_max = jnp.max(jnp.maximum(s[:, :128], s[:, 128:]), axis=-1, keepdims=True)
        m = jnp.maximum(jnp.broadcast_to(row_max, (rows, 128)), sink)
        p = jnp.exp2(s - jnp.concatenate([m, m], axis=1)).astype(BF16)
        v_aug = jnp.concatenate([v_all[r * w:(r + 2) * w, :], ones], axis=1)
        pv = jnp.dot(p, v_aug, preferred_element_type=F32)
        out = pv[:, :128] / (pv[:, 128:] + jnp.exp2(sink - m))
        tiles = []
        for pair in range(4):
            lo = out[pair * w:(pair + 1) * w, :]
            hi = out[(4 + pair) * w:(5 + pair) * w, :]
            tiles.append(jnp.where(lane < 64, lo, hi))
        o_ref[r * w:(r + 1) * w, :] = jnp.concatenate(tiles, axis=1).astype(BF16)

    return [functools.partial(sub_block, r) for r in range(SWA_TILE // w)]


def _attn_kernel(sink_ref, qa_ref, kap_ref, kac_ref, vap_ref, vac_ref, q_ref, k_ref, v_ref,
                 oa_ref, o_ref,
                 selb_ref, qconst_ref, causal_ref, qaug_ref, u_ref, mpart_ref, acc_ref,
                 swa_qconst_ref, swa_bias_ref):
    step = pl.program_id(1)
    bl = MOBA_BLOCK
    nbk = N_MOBA_BLOCKS
    rows = N_HEADS * bl
    lane = lax.broadcasted_iota(jnp.int32, (bl, 128), 1)
    lane_blk = lane % nbk
    lane_slot = lane // nbk
    sel_lane = lane < SEL_LANES
    in3 = lambda first: (lane >= first) & (lane < first + 3)

    def lanes3(first, terms):
        out = jnp.zeros((bl, 128), F32)
        for i, term in enumerate(terms):
            out = jnp.where(lane == first + i, term.astype(F32), out)
        return out

    @pl.when((pl.program_id(0) == 0) & (step == 0))
    def _():
        _swa_consts(swa_qconst_ref, swa_bias_ref)
        causal_ref[...] = jnp.where(lax.broadcasted_iota(jnp.int32, (bl, bl), 0)
                                    >= lax.broadcasted_iota(jnp.int32, (bl, bl), 1), 0.0, NEG)
        rowf = lax.broadcasted_iota(jnp.int32, (bl, 128), 0).astype(F32)
        for slot in range(N_HEADS):
            slope = jnp.full((bl, 128), ALIBI_SLOPES[N_HEADS + HEAD_OF_SLOT[slot]], F32)
            qc = (lanes3(ROW_LANE, _split3(-slope * rowf)) + lanes3(COL_LANE, _split3(slope))
                  + lanes3(BLK_LANE, _split3(-slope)))
            qconst_ref[slot * bl:(slot + 1) * bl, :] = qc.astype(BF16)

    @pl.when(step == 0)
    def _():
        rsel = lax.broadcasted_iota(jnp.int32, (128, SEQ), 0)
        tsel = lax.broadcasted_iota(jnp.int32, (128, SEQ), 1)
        avg = jnp.where((rsel % nbk) == (tsel // bl), 1.0 / bl, 0.0).astype(BF16)
        kmean_rows = jnp.dot(avg, k_ref[...], preferred_element_type=F32)
        kmean_t = kmean_rows.T
        r128 = lax.broadcasted_iota(jnp.int32, (128, 128), 0)
        c128 = lax.broadcasted_iota(jnp.int32, (128, 128), 1)
        km = jnp.concatenate(
            [jnp.where((c128 // nbk) == 2 * pair + (r128 >= 64).astype(jnp.int32), kmean_t, 0.0)
             for pair in range(4)], axis=0)
        q_all = q_ref[...]
        gate = sum(jnp.dot(q_all, term, preferred_element_type=F32) for term in _split3(km)[:2])
        gate_t = gate.T[:SEL_LANES].reshape(N_HEADS, nbk, SEQ)
        blk = lax.broadcasted_iota(jnp.int32, (N_HEADS, nbk, SEQ), 1)
        own = lax.broadcasted_iota(jnp.int32, (N_HEADS, nbk, SEQ), 2) // bl
        past = blk < own
        gm = jnp.where(past, gate_t, -jnp.inf)
        rank = jnp.zeros((N_HEADS, nbk, SEQ), jnp.int32)
        for d in range(1, nbk):
            partner = pltpu.roll(gm, nbk - d, 1)
            wrapped = blk + d >= nbk
            beats = (partner > gm) | ((partner == gm) & wrapped)
            rank = rank + beats.astype(jnp.int32)
        selected = past & (rank < MOBA_TOPK)
        selb_t = jnp.where(selected, 0.0, NEG).reshape(SEL_LANES, SEQ)
        selb_ref[...] = jnp.concatenate([selb_t, jnp.zeros_like(selb_t)], axis=0).T

    def fold(s):
        return jnp.maximum(s[:, :128], s[:, 128:])

    def run_step(t_a):
        tiles_t = (t_a, t_a + 1)
        past = [(side, j) for j in range(t_a + 1) for side in range(2) if j < tiles_t[side]]

        swa_todo = _swa_tile(t_a == 0, sink_ref, qa_ref, kap_ref, kac_ref, vap_ref, vac_ref, oa_ref,
                             swa_qconst_ref, swa_bias_ref)
        if t_a == 0:
            while swa_todo:
                swa_todo.pop(0)()

        for side, t in enumerate(tiles_t):
            q = q_ref[t * bl:(t + 1) * bl, :]
            selb = selb_ref[t * bl:(t + 1) * bl, :]
            for slot in range(N_HEADS):
                pair, half = divmod(slot, 2)
                qp = q[:, pair * 128:(pair + 1) * 128]
                keep = (lane < 64) if half == 0 else (lane >= 64)
                qm = jnp.where(keep, qp, jnp.zeros_like(qp))
                extra = jnp.where(sel_lane & (lane_slot == slot), selb,
                                  qconst_ref[slot * bl:(slot + 1) * bl, :].astype(F32))
                qaug_ref[side, slot * bl:(slot + 1) * bl, :] = jnp.concatenate(
                    [qm, extra.astype(BF16)], axis=1)

        key_col = lax.broadcasted_iota(jnp.int32, (bl, 128), 0).astype(F32)
        alibi_lanes = jnp.where(in3(ROW_LANE), 1.0, jnp.where(in3(COL_LANE), key_col, 0.0))

        def scores(side, j):
            t = tiles_t[side]
            if j == t:
                extra = alibi_lanes
            else:
                extra = jnp.where(sel_lane, jnp.where(lane_blk == j, 1.0, 0.0),
                                  jnp.where(in3(BLK_LANE), float((t - j) * bl), alibi_lanes))
            kaug = jnp.concatenate([k_ref[j * bl:(j + 1) * bl, :], extra.astype(BF16)], axis=1)
            return lax.dot_general(qaug_ref[side], kaug, (((1,), (1,)), ((), ())),
                                   preferred_element_type=F32)

        def probs_v(side, blocks):
            m = mpart_ref[side]
            pb = jnp.concatenate([jnp.exp2(u_ref[n, :, c:c + 128] - m) for n, _ in blocks for c in (0, 128)],
                                 axis=1).astype(BF16)
            vj = jnp.concatenate([v_ref[j * bl:(j + 1) * bl, :] for _, j in blocks], axis=0)
            return jnp.dot(pb, jnp.concatenate([vj, jnp.ones_like(vj)], axis=1),
                           preferred_element_type=F32)

        causal = causal_ref[...]
        for side, t in enumerate(tiles_t):
            s_own = (scores(side, t).reshape(N_HEADS, bl, bl) + causal[None]).reshape(rows, bl)
            u_ref[side] = s_own
            mpart_ref[side] = fold(s_own)
            if swa_todo:
                swa_todo.pop(0)()
        for n, (side, j) in enumerate(past):
            sj = scores(side, j)
            u_ref[2 + n] = sj
            mpart_ref[side] = jnp.maximum(mpart_ref[side], fold(sj))
            if swa_todo:
                swa_todo.pop(0)()
        while swa_todo:
            swa_todo.pop(0)()

        for side in range(2):
            mpart_ref[side] = jnp.broadcast_to(jnp.max(mpart_ref[side], axis=-1, keepdims=True),
                                               (rows, 128))

        stored = [[(side, tiles_t[side])] + [(2 + n, j) for n, (s, j) in enumerate(past) if s == side]
                  for side in range(2)]
        pairs = [[blocks[i:i + PV_GROUP] for i in range(0, len(blocks), PV_GROUP)] for blocks in stored]
        for g in range(max(len(p) for p in pairs)):
            for side in range(2):
                if g < len(pairs[side]):
                    if g == 0:
                        acc_ref[side] = probs_v(side, pairs[side][g])
                    else:
                        acc_ref[side] += probs_v(side, pairs[side][g])

        for side, t in enumerate(tiles_t):
            out = acc_ref[side, :, :128] / acc_ref[side, :, 128:]
            heads = []
            for pair in range(4):
                lo = out[(2 * pair) * bl:(2 * pair + 1) * bl, :]
                hi = out[(2 * pair + 1) * bl:(2 * pair + 2) * bl, :]
                heads.append(jnp.where(lane < 64, lo, hi))
            o_ref[t * bl:(t + 1) * bl, :] = jnp.concatenate(heads, axis=1).astype(BF16)

    for pair_step in range(N_MOBA_BLOCKS // 2):
        pl.when(step == pair_step)(functools.partial(run_step, 2 * pair_step))


def _attention(sinks, qa, ka, va, qb, kb, vb):
    bl = MOBA_BLOCK
    w = SWA_WINDOW
    assert SWA_TILE == 2 * bl
    rows = N_HEADS * bl
    step_blocks = 2 * N_MOBA_BLOCKS - 1
    whole = lambda b, s: (b, 0, 0)
    cur = lambda b, s: (b, s, 0)
    prev = lambda b, s: (b, jnp.maximum(s * (SWA_TILE // w) - 1, 0), 0)
    out = jax.ShapeDtypeStruct((BATCH, SEQ, Q_COLS), BF16)
    return pl.pallas_call(
        _attn_kernel,
        grid=(BATCH, SEQ // SWA_TILE),
        in_specs=[pl.BlockSpec(memory_space=pltpu.SMEM),
                  pl.BlockSpec((None, SWA_TILE, Q_COLS), cur),
                  pl.BlockSpec((None, w, KV_COLS), prev),
                  pl.BlockSpec((None, SWA_TILE, KV_COLS), cur),
                  pl.BlockSpec((None, w, KV_COLS), prev),
                  pl.BlockSpec((None, SWA_TILE, KV_COLS), cur),
                  pl.BlockSpec((None, SEQ, Q_COLS), whole),
                  pl.BlockSpec((None, SEQ, KV_COLS), whole),
                  pl.BlockSpec((None, SEQ, KV_COLS), whole)],
        out_specs=[pl.BlockSpec((None, SWA_TILE, Q_COLS), cur),
                   pl.BlockSpec((None, SEQ, Q_COLS), whole)],
        out_shape=[out, out],
        scratch_shapes=[pltpu.VMEM((SEQ, 128), F32),
                        pltpu.VMEM((rows, 128), BF16),
                        pltpu.VMEM((bl, bl), F32),
                        pltpu.VMEM((2, rows, 256), BF16),
                        pltpu.VMEM((step_blocks, rows, bl), F32),
                        pltpu.VMEM((2, rows, 128), F32),
                        pltpu.VMEM((2, rows, 256), F32),
                        pltpu.VMEM((N_HEADS * w, 128), BF16),
                        pltpu.VMEM((w, 2 * w), F32)],
        compiler_params=pltpu.CompilerParams(
            dimension_semantics=("arbitrary", "arbitrary"), vmem_limit_bytes=V7X_VMEM_LIMIT_BYTES),
        name="attention",
    )(sinks, qa, ka, ka, va, va, qb, kb, vb)


def _mix_ffn_kernel(x1_ref, ya_ref, yb_ref, w_mix_hbm, wa_hbm, wb_hbm, wo_hbm, g2_ref, b2_ref,
                    w_in_hbm, w_out_hbm, g3_ref, b3_ref, o_ref,
                    act_ref, wg_ref, wa_ref, wb_ref, wo_ref, w_in_ref, w_out_ref, stage_ref, sem_ref):
    @pl.when(pl.program_id(0) == 0)
    def _():
        jobs = _window_jobs(w_mix_hbm, wg_ref, D_MODEL, 2 * D_MODEL, QKV_COLS)
        slot_row = lambda r: HEAD_OF_SLOT.index(r // HEAD_DIM) * HEAD_DIM
        jobs += _window_jobs(wa_hbm, wa_ref, Q_COLS, D_MODEL, dst_row=slot_row)
        jobs += _window_jobs(wb_hbm, wb_ref, Q_COLS, D_MODEL, dst_row=slot_row)
        jobs += _window_jobs(wo_hbm, wo_ref, D_MODEL, D_MODEL)
        jobs += _window_jobs(w_in_hbm, w_in_ref, D_MODEL, 2 * D_FF)
        jobs += _window_jobs(w_out_hbm, w_out_ref, D_FF, D_MODEL, fix=_half)
        _stream_cast(jobs, stage_ref, sem_ref)

    x1 = x1_ref[...]
    x1b = x1.astype(BF16)
    ga = jnp.dot(x1b, wg_ref[:, :D_MODEL], preferred_element_type=F32)
    ya = jnp.dot(ya_ref[...], wa_ref[...], preferred_element_type=F32)
    y = jax.nn.sigmoid(ga) * ya
    gb = jnp.dot(x1b, wg_ref[:, D_MODEL:], preferred_element_type=F32)
    yb = jnp.dot(yb_ref[...], wb_ref[...], preferred_element_type=F32)
    y = y + jax.nn.sigmoid(gb) * yb
    z = jnp.dot(y.astype(BF16), wo_ref[...], preferred_element_type=F32)
    x2 = _layer_norm(ALPHA * x1 + z, g2_ref[...], b2_ref[...])
    f = _swiglu(x2.astype(BF16), w_in_ref, w_out_ref, act_ref)
    o_ref[...] = _layer_norm(ALPHA * x2 + f, g3_ref[...], b3_ref[...])


def _mix_ffn(x1, ya, yb, w_mix, wa, wb, wo, g2, b2, w_in, w_out, g3, b3):
    tm = TOKEN_TILE
    row = lambda n: pl.BlockSpec((tm, n), lambda i: (i, 0))
    hbm = pl.BlockSpec(memory_space=pl.ANY)
    ln = _const_spec(g2.shape)
    return pl.pallas_call(
        _mix_ffn_kernel,
        grid=(TOKENS // tm,),
        in_specs=[row(D_MODEL), row(Q_COLS), row(Q_COLS), hbm, hbm, hbm, hbm, ln, ln, hbm, hbm, ln, ln],
        out_specs=row(D_MODEL),
        out_shape=jax.ShapeDtypeStruct((TOKENS, D_MODEL), F32),
        scratch_shapes=[pltpu.VMEM((tm, D_FF), BF16),
                        pltpu.VMEM((D_MODEL, 2 * D_MODEL), BF16),
                        pltpu.VMEM((Q_COLS, D_MODEL), BF16),
                        pltpu.VMEM((Q_COLS, D_MODEL), BF16),
                        pltpu.VMEM((D_MODEL, D_MODEL), BF16),
                        pltpu.VMEM((D_MODEL, 2 * D_FF), BF16),
                        pltpu.VMEM((D_FF, D_MODEL), BF16),
                        pltpu.VMEM(STAGE_SHAPE, F32),
                        pltpu.SemaphoreType.DMA((STAGE_SLOTS,))],
        compiler_params=pltpu.CompilerParams(
            dimension_semantics=("arbitrary",), vmem_limit_bytes=V7X_VMEM_LIMIT_BYTES),
        name="mix_ffn",
    )(x1, ya, yb, w_mix, wa, wb, wo, g2, b2, w_in, w_out, g3, b3)


def kernel(x, ffn1_w_in, ffn1_w_out, ln1_g, ln1_b, mix_w_in, swa_sinks, w_branch_a, w_branch_b,
           mix_w_o, ln2_g, ln2_b, ffn2_w_in, ffn2_w_out, ln3_g, ln3_b):
    assert x.shape == (BATCH, SEQ, D_MODEL) and ffn1_w_in.shape == (1, D_MODEL, 2 * D_FF)
    assert mix_w_in.shape == (1, D_MODEL, MIX_IN_COLS)
    x1, qa, ka, va, qb, kb, vb = _ffn_qkv(x.reshape(TOKENS, D_MODEL), ffn1_w_in, ffn1_w_out,
                                          ln1_g, ln1_b, mix_w_in)
    seq3 = lambda a: a.reshape(BATCH, SEQ, a.shape[-1])
    ya, yb = _attention(swa_sinks[0], seq3(qa), seq3(ka), seq3(va), seq3(qb), seq3(kb), seq3(vb))
    out = _mix_ffn(x1, ya.reshape(TOKENS, Q_COLS), yb.reshape(TOKENS, Q_COLS), mix_w_in,
                   w_branch_a, w_branch_b, mix_w_o, ln2_g, ln2_b, ffn2_w_in, ffn2_w_out, ln3_g, ln3_b)
    return out.reshape(BATCH, SEQ, D_MODEL)
```
